```python
import jax, jax.numpy as jnp
from jax import lax
import numpy as np

D_MODEL = 1024
BATCH = 2
SEQ = 8192
DEPTH = 1
DEC_BATCH = 128
DEC_SEQ = 8
PAST_LEN = 16384
PAGE_SIZE = 128

HEAD_DIM = 64
ATT_HEADS = 8
KV_HEADS = 2
Q_PER_KV = ATT_HEADS // KV_HEADS
ATT_WIDTH = ATT_HEADS * HEAD_DIM
KV_WIDTH = KV_HEADS * HEAD_DIM
RWKV_HEADS = 8
RWKV_WIDTH = RWKV_HEADS * HEAD_DIM
MIX_WIDTH = ATT_WIDTH + RWKV_WIDTH
WINDOW = 128
BLOCK = 128
DECAY_LORA = 64
ICL_LORA = 64
GATE_LORA = 128
RWKV_BLOCK = 3 * RWKV_WIDTH + DECAY_LORA + ICL_LORA + GATE_LORA
ATT_PROJ = ATT_WIDTH + 2 * KV_WIDTH
PROJ_WIDTH = ATT_PROJ + RWKV_BLOCK
D_FF = -(-8 * D_MODEL // (3 * 256)) * 256
RMS_EPS = 1e-6
GN_EPS = 64e-5
L2_EPS = 1e-12
NEG_INF = -1e30

PROJ_SPLITS = [ATT_WIDTH, ATT_WIDTH + KV_WIDTH, ATT_PROJ]
RWKV_SPLITS = [RWKV_WIDTH, 2 * RWKV_WIDTH, 3 * RWKV_WIDTH,
               3 * RWKV_WIDTH + DECAY_LORA, 3 * RWKV_WIDTH + DECAY_LORA + ICL_LORA]

kernel_name = "hymba_swa_sink_rwkv7_decode_step"


def rmsnorm(x, g):
    xf = x.astype(jnp.float32)
    y = xf * lax.rsqrt(jnp.mean(xf * xf, axis=-1, keepdims=True) + RMS_EPS)
    return (y * g.astype(jnp.float32)).astype(x.dtype)


def attend_with_sinks(q, k, v, mask, sinks):
    s = jnp.einsum('...qkgd,...skd->...kgqs', q, k).astype(jnp.float32) * (HEAD_DIM ** -0.5)
    s = jnp.where(mask, s, NEG_INF)
    sink = jnp.broadcast_to(sinks.astype(jnp.float32).reshape(KV_HEADS, Q_PER_KV, 1, 1),
                            s.shape[:-1] + (1,))
    p = jax.nn.softmax(jnp.concatenate([s, sink], axis=-1), axis=-1)[..., :-1]
    return jnp.einsum('...kgqs,...skd->...qkgd', p.astype(v.dtype), v)


def swa_prompt(q, k, v, sinks):
    B, T = q.shape[0], q.shape[1]
    nb = T // BLOCK
    qb = q.reshape(B, nb, BLOCK, KV_HEADS, Q_PER_KV, HEAD_DIM)
    kb = k.reshape(B, nb, BLOCK, KV_HEADS, HEAD_DIM)
    vb = v.reshape(B, nb, BLOCK, KV_HEADS, HEAD_DIM)
    pad = ((0, 0), (1, 0), (0, 0), (0, 0), (0, 0))
    kk = jnp.concatenate([jnp.pad(kb, pad)[:, :-1], kb], axis=2)
    vv = jnp.concatenate([jnp.pad(vb, pad)[:, :-1], vb], axis=2)
    qi = jnp.arange(BLOCK)[:, None] + BLOCK
    kj = jnp.arange(2 * BLOCK)[None, :]
    band = (kj <= qi) & (qi - kj < WINDOW)
    first = (jnp.arange(nb) == 0)[:, None, None]
    mask = band[None] & ~(first & (kj < BLOCK)[None])
    o = attend_with_sinks(qb, kk, vv, mask[:, None, None], sinks)
    return o.reshape(B, T, ATT_WIDTH)


def swa_sample(q, k, v, k_buf, v_buf, sinks):
    B, T = q.shape[0], q.shape[1]
    wb = k_buf.shape[1]
    kk = jnp.concatenate([k_buf.astype(k.dtype), k], axis=1)
    vv = jnp.concatenate([v_buf.astype(v.dtype), v], axis=1)
    q_rel = wb + jnp.arange(T)[:, None]
    k_rel = jnp.arange(wb + T)[None, :]
    mask = (k_rel <= q_rel) & (q_rel - k_rel < WINDOW)
    o = attend_with_sinks(q.reshape(B, T, KV_HEADS, Q_PER_KV, HEAD_DIM), kk, vv, mask, sinks)
    return o.reshape(B, T, ATT_WIDTH), kk[:, -wb:], vv[:, -wb:]


def wkv_scan(S0, r, decay, k, v, a, b):
    def step(S, inp):
        r_t, w_t, k_t, v_t, a_t, b_t = inp
        sa = jnp.einsum('bhvk,bhk->bhv', S, a_t)
        S = S * w_t[:, :, None, :] + sa[..., None] * b_t[:, :, None, :] + v_t[..., None] * k_t[:, :, None, :]
        return S, jnp.einsum('bhvk,bhk->bhv', S, r_t)
    xs = tuple(jnp.moveaxis(t, 1, 0) for t in (r, decay, k, v, a, b))
    S, o = lax.scan(step, S0, xs)
    return jnp.moveaxis(o, 0, 1), S


def rwkv_mix(feat, feat_prev, S0, mu, w0, w2, a0, a2, g2, k_k, k_a, r_k, gn_g, gn_b):
    B, T = feat.shape[0], feat.shape[1]
    f = feat.astype(jnp.float32)
    prev = jnp.concatenate([feat_prev.astype(jnp.float32)[:, None], f[:, :-1]], axis=1)
    xs = f + (prev - f) * mu
    r, k, v, wl, al, gl = jnp.split(xs, RWKV_SPLITS, axis=-1)
    w = -jax.nn.softplus(-(w0 + jnp.tanh(wl) @ w2)) - 0.5
    decay = jnp.exp(-jnp.exp(w))
    a = jax.nn.sigmoid(a0 + al @ a2)
    g = jax.nn.sigmoid(gl) @ g2
    heads = lambda t: t.reshape(B, T, RWKV_HEADS, HEAD_DIM)
    kk = heads(k * k_k)
    kk = kk * lax.rsqrt(jnp.sum(kk * kk, axis=-1, keepdims=True) + L2_EPS)
    k = k * (1.0 + (a - 1.0) * k_a)
    rh, kh, vh, ah = heads(r), heads(k), heads(v), heads(a)
    o, S = wkv_scan(S0.astype(jnp.float32), rh, heads(decay), kh, vh, -kk, kk * ah)
    mean = jnp.mean(o, axis=-1, keepdims=True)
    var = jnp.mean(jnp.square(o - mean), axis=-1, keepdims=True)
    on = ((o - mean) * lax.rsqrt(var + GN_EPS)).reshape(B, T, RWKV_WIDTH) * gn_g + gn_b
    bonus = (jnp.sum(rh * kh * r_k, axis=-1, keepdims=True) * vh).reshape(B, T, RWKV_WIDTH)
    return ((on + bonus) * g).astype(feat.dtype), S


def hybrid_layer(x, h_prev, k_buf, v_buf, S0, g_mix, w_in, attn_sinks, rwkv_mu, w0, w2, a0, a2, g2,
                 k_k, k_a, r_k, gn_g, gn_b, w_out, g_ffn, w_gate, w_up, w_down):
    B, T = x.shape[0], x.shape[1]
    h = rmsnorm(x, g_mix)
    proj = h @ w_in
    q, k, v, feat = jnp.split(proj, PROJ_SPLITS, axis=-1)
    k = k.reshape(B, T, KV_HEADS, HEAD_DIM)
    v = v.reshape(B, T, KV_HEADS, HEAD_DIM)
    if k_buf is None:
        att = swa_prompt(q, k, v, attn_sinks)
        wp = min(WINDOW, T)
        k_win, v_win = k[:, T - wp:], v[:, T - wp:]
        feat_prev = jnp.zeros((B, RWKV_BLOCK), feat.dtype)
        S0 = jnp.zeros((B, RWKV_HEADS, HEAD_DIM, HEAD_DIM), jnp.float32)
    else:
        att, k_win, v_win = swa_sample(q, k, v, k_buf, v_buf, attn_sinks)
        feat_prev = h_prev.astype(h.dtype) @ w_in[:, ATT_PROJ:]
    rw, S = rwkv_mix(feat, feat_prev, S0, rwkv_mu, w0, w2, a0, a2, g2, k_k, k_a, r_k, gn_g, gn_b)
    x = x + jnp.concatenate([att, rw], axis=-1) @ w_out
    u = rmsnorm(x, g_ffn)
    x = x + (jax.nn.silu(u @ w_gate) * (u @ w_up)) @ w_down
    return x, k_win, v_win, S, h[:, -1]


def setup_inputs(seed: int = 0) -> dict:
    key = jax.random.key(seed)
    ks = jax.random.split(key, 32)
    nrm = lambda i, shape, scale: jax.random.normal(ks[i], shape, jnp.float32) * scale
    wb = min(WINDOW, PAST_LEN)
    return {
        "x_prompt": nrm(0, (BATCH, SEQ, D_MODEL), 1.0),
        "x_sample": nrm(1, (DEC_BATCH, DEC_SEQ, D_MODEL), 1.0),
        "cache_k": nrm(2, (DEPTH, DEC_BATCH, wb, KV_HEADS, HEAD_DIM), 1.0),
        "cache_v": nrm(3, (DEPTH, DEC_BATCH, wb, KV_HEADS, HEAD_DIM), 1.0),
        "state_wkv": nrm(4, (DEPTH, DEC_BATCH, RWKV_HEADS, HEAD_DIM, HEAD_DIM), 0.3),
        "state_shift": nrm(5, (DEPTH, DEC_BATCH, D_MODEL), 1.0),
        "g_mix": 1.0 + nrm(6, (DEPTH, D_MODEL), 0.05),
        "w_in": nrm(7, (DEPTH, D_MODEL, PROJ_WIDTH), D_MODEL ** -0.5),
        "attn_sinks": nrm(8, (DEPTH, ATT_HEADS), 1.0),
        "rwkv_mu": jax.random.uniform(ks[9], (DEPTH, RWKV_BLOCK), jnp.float32),
        "w0": jax.random.uniform(ks[10], (DEPTH, RWKV_WIDTH), jnp.float32, -6.0, -1.0),
        "w2": nrm(11, (DEPTH, DECAY_LORA, RWKV_WIDTH), 0.5 * DECAY_LORA ** -0.5),
        "a0": nrm(12, (DEPTH, RWKV_WIDTH), 0.1),
        "a2": nrm(13, (DEPTH, ICL_LORA, RWKV_WIDTH), 0.5 * ICL_LORA ** -0.5),
        "g2": nrm(14, (DEPTH, GATE_LORA, RWKV_WIDTH), GATE_LORA ** -0.5),
        "k_k": 0.85 + nrm(15, (DEPTH, RWKV_WIDTH), 0.05),
        "k_a": 1.0 + nrm(16, (DEPTH, RWKV_WIDTH), 0.05),
        "r_k": nrm(17, (DEPTH, RWKV_HEADS, HEAD_DIM), 0.1),
        "gn_g": 1.0 + nrm(18, (DEPTH, RWKV_WIDTH), 0.05),
        "gn_b": nrm(19, (DEPTH, RWKV_WIDTH), 0.02),
        "w_out": nrm(20, (DEPTH, MIX_WIDTH, D_MODEL), MIX_WIDTH ** -0.5),
        "g_ffn": 1.0 + nrm(21, (DEPTH, D_MODEL), 0.05),
        "w_gate": nrm(22, (DEPTH, D_MODEL, D_FF), D_MODEL ** -0.5),
        "w_up": nrm(23, (DEPTH, D_MODEL, D_FF), D_MODEL ** -0.5),
        "w_down": nrm(24, (DEPTH, D_FF, D_MODEL), D_FF ** -0.5),
        "g_final": 1.0 + nrm(25, (D_MODEL,), 0.05),
    }


def reference(x_prompt, x_sample, cache_k, cache_v, state_wkv, state_shift, g_mix, w_in, attn_sinks,
              rwkv_mu, w0, w2, a0, a2, g2, k_k, k_a, r_k, gn_g, gn_b, w_out, g_ffn, w_gate, w_up,
              w_down, g_final):
    xp, xs = x_prompt, x_sample
    kp_l, vp_l, sp_l, hp_l = [], [], [], []
    ks_l, vs_l, ss_l, hs_l = [], [], [], []
    for l in range(DEPTH):
        ws = (g_mix[l], w_in[l], attn_sinks[l], rwkv_mu[l], w0[l], w2[l], a0[l], a2[l], g2[l],
              k_k[l], k_a[l], r_k[l], gn_g[l], gn_b[l], w_out[l], g_ffn[l], w_gate[l], w_up[l], w_down[l])
        xp, kp, vp, sp, hp = hybrid_layer(xp, None, None, None, None, *ws)
        xs, kn, vn, sn, hn = hybrid_layer(xs, state_shift[l], cache_k[l], cache_v[l], state_wkv[l], *ws)
        kp_l.append(kp); vp_l.append(vp); sp_l.append(sp.astype(x_prompt.dtype)); hp_l.append(hp)
        ks_l.append(kn); vs_l.append(vn); ss_l.append(sn.astype(state_wkv.dtype)); hs_l.append(hn)
    y_prompt = rmsnorm(xp, g_final)
    y_sample = rmsnorm(xs, g_final)
    return (y_prompt, y_sample,
            jnp.stack(kp_l), jnp.stack(vp_l), jnp.stack(sp_l), jnp.stack(hp_l),
            jnp.stack(ks_l), jnp.stack(vs_l), jnp.stack(ss_l), jnp.stack(hs_l))
```

```python
import functools

import jax
import jax.numpy as jnp
from jax import lax
from jax.experimental import pallas as pl
from jax.experimental.pallas import tpu as pltpu

F32 = jnp.float32
BF16 = jnp.bfloat16

D_MODEL = 1024
HEAD_DIM = 64
ATT_HEADS = 8
KV_HEADS = 2
ATT_WIDTH = ATT_HEADS * HEAD_DIM
KV_WIDTH = KV_HEADS * HEAD_DIM
RWKV_HEADS = 8
RWKV_WIDTH = RWKV_HEADS * HEAD_DIM
WINDOW = 128
DECAY_LORA = 64
ICL_LORA = 64
GATE_LORA = 128
RWKV_BLOCK = 3 * RWKV_WIDTH + DECAY_LORA + ICL_LORA + GATE_LORA
ATT_PROJ = ATT_WIDTH + 2 * KV_WIDTH
PROJ_WIDTH = ATT_PROJ + RWKV_BLOCK
RMS_EPS = 1e-6
GN_EPS = 64e-5
L2_EPS = 1e-12
NEG_INF = -1e30

V7X_VMEM_BYTES = 64 * 1024 * 1024
VMEM_LIMIT_BYTES = V7X_VMEM_BYTES * 3 // 4

NN = (((1,), (0,)), ((), ()))
NT = (((1,), (1,)), ((), ()))
TN = (((0,), (0,)), ((), ()))


def _dot(a, b, dn=NN):
    return lax.dot_general(a, b, dn, preferred_element_type=F32)


def _split2(x):
    hi = x.astype(BF16)
    lo = (x - hi.astype(F32)).astype(BF16)
    return hi, lo


def _split3(x):
    hi = x.astype(BF16)
    r1 = x - hi.astype(F32)
    mid = r1.astype(BF16)
    lo = (r1 - mid.astype(F32)).astype(BF16)
    return hi, mid, lo


def _dot3(xs, ys, dn=NN):
    xh, xl = xs
    yh, yl = ys
    return _dot(xh, yh, dn) + (_dot(xh, yl, dn) + _dot(xl, yh, dn))


def _seg_sum(x, ones_bd):
    hi, lo = _split2(x)
    return _dot(hi, ones_bd) + _dot(lo, ones_bd)


def _rms(x, g):
    return x * lax.rsqrt(jnp.mean(x * x, axis=-1, keepdims=True) + RMS_EPS) * g


def _const_spec(shape):
    return pl.BlockSpec(shape, lambda *_: (0,) * len(shape))


def _params(*sem):
    return pltpu.CompilerParams(dimension_semantics=sem, vmem_limit_bytes=VMEM_LIMIT_BYTES)


def _rms_rows_kernel(x_ref, g_ref, o_ref):
    o_ref[...] = _rms(x_ref[...], g_ref[...])


def _rms_rows(x, g):
    n, d = x.shape
    return pl.pallas_call(
        _rms_rows_kernel,
        out_shape=jax.ShapeDtypeStruct((n, d), F32),
        name="rms_rows",
    )(x, g.reshape(1, d))


def _inproj_kernel(x_ref, g_ref, w_ref, q_ref, k_ref, v_ref, f_ref, *, normalize):
    x = x_ref[...]
    h = _rms(x, g_ref[...]) if normalize else x
    proj = _dot(h.astype(BF16), w_ref[...])
    q_ref[...] = proj[:, :ATT_WIDTH].astype(q_ref.dtype)
    k_ref[...] = proj[:, ATT_WIDTH:ATT_WIDTH + KV_WIDTH]
    v_ref[...] = proj[:, ATT_WIDTH + KV_WIDTH:ATT_PROJ]
    f_ref[...] = proj[:, ATT_PROJ:]


def _inproj(x, g_mix, w_in_bf, *, normalize, tm, q_dtype=BF16):
    n = x.shape[0]
    row = lambda w: pl.BlockSpec((tm, w), lambda i: (i, 0))
    return pl.pallas_call(
        functools.partial(_inproj_kernel, normalize=normalize),
        grid=(n // tm,),
        in_specs=[row(D_MODEL), _const_spec((1, D_MODEL)), _const_spec((D_MODEL, PROJ_WIDTH))],
        out_specs=[row(ATT_WIDTH), row(KV_WIDTH), row(KV_WIDTH), row(RWKV_BLOCK)],
        out_shape=[
            jax.ShapeDtypeStruct((n, ATT_WIDTH), q_dtype),
            jax.ShapeDtypeStruct((n, KV_WIDTH), F32),
            jax.ShapeDtypeStruct((n, KV_WIDTH), F32),
            jax.ShapeDtypeStruct((n, RWKV_BLOCK), F32),
        ],
        compiler_params=_params("arbitrary"),
        name="inproj",
    )(x, g_mix.reshape(1, D_MODEL), w_in_bf)


def _softplus(z):
    return jnp.maximum(z, 0.0) + jnp.log(1.0 + jnp.exp(-jnp.abs(z)))


def _sigmoid(z):
    return 1.0 / (1.0 + jnp.exp(-z))


def _prep_kernel(*refs, blocks_per_seq, has_prev):
    if has_prev:
        f_ref, p_ref = refs[:2]
        refs = refs[2:]
    else:
        f_ref = refs[0]
        refs = refs[1:]
    (mu_ref, w0_ref, w2_ref, a0_ref, a2_ref, g2_ref, kk_ref, ka_ref, rk_ref, ones_ref,
     r_out, l_out, k_out, v_out, kk_out, kb_out, g_out, bo_out) = refs[:18]
    f = f_ref[...]
    tm = f.shape[0]
    if has_prev:
        prev = p_ref[...]
    else:
        carry_ref = refs[18]
        j = pl.program_id(0)

        @pl.when(j % blocks_per_seq == 0)
        def _():
            carry_ref[...] = jnp.zeros_like(carry_ref)

        rolled = pltpu.roll(f, 1, 0)
        row = lax.broadcasted_iota(jnp.int32, f.shape, 0)
        prev = jnp.where(row == 0, carry_ref[0:1, :], rolled)
        carry_ref[0:1, :] = f[tm - 1:tm, :]
    xs = f + (prev - f) * mu_ref[...]
    w3 = 3 * RWKV_WIDTH
    r = xs[:, :RWKV_WIDTH]
    k = xs[:, RWKV_WIDTH:2 * RWKV_WIDTH]
    v = xs[:, 2 * RWKV_WIDTH:w3]
    wl = xs[:, w3:w3 + DECAY_LORA]
    al = xs[:, w3 + DECAY_LORA:w3 + DECAY_LORA + ICL_LORA]
    gl = xs[:, w3 + DECAY_LORA + ICL_LORA:]
    wr = -_softplus(-(w0_ref[...] + _dot(jnp.tanh(wl).astype(BF16), w2_ref[...]))) - 0.5
    a = _sigmoid(a0_ref[...] + _dot(al.astype(BF16), a2_ref[...]))
    g = _dot(_sigmoid(gl).astype(BF16), g2_ref[...])
    ones_bd = ones_ref[...]
    kk = k * kk_ref[...]
    kk = kk * lax.rsqrt(_seg_sum(kk * kk, ones_bd) + L2_EPS)
    k2 = k * (1.0 + (a - 1.0) * ka_ref[...])
    r_out[...] = r
    l_out[...] = -jnp.exp(wr)
    k_out[...] = k2
    v_out[...] = v
    kk_out[...] = kk
    kb_out[...] = kk * a
    g_out[...] = g
    bo_out[...] = _seg_sum(r * k2 * rk_ref[...], ones_bd) * v


def _prep(feat, prev, p, *, tm, blocks_per_seq):
    n = feat.shape[0]
    has_prev = prev is not None
    row = lambda w: pl.BlockSpec((tm, w), lambda i: (i, 0))
    w = RWKV_WIDTH
    ins = [feat] + ([prev] if has_prev else [])
    in_specs = [row(RWKV_BLOCK)] * len(ins)
    ins += [p["mu"], p["w0"], p["w2"], p["a0"], p["a2"], p["g2"], p["k_k"], p["k_a"], p["r_k"], p["ones_bd"]]
    in_specs += [_const_spec((1, RWKV_BLOCK)), _const_spec((1, w)), _const_spec((DECAY_LORA, w)),
                 _const_spec((1, w)), _const_spec((ICL_LORA, w)), _const_spec((GATE_LORA, w)),
                 _const_spec((1, w)), _const_spec((1, w)), _const_spec((1, w)), _const_spec((w, w))]
    return pl.pallas_call(
        functools.partial(_prep_kernel, blocks_per_seq=blocks_per_seq, has_prev=has_prev),
        grid=(n // tm,),
        in_specs=in_specs,
        out_specs=[row(w)] * 8,
        out_shape=[jax.ShapeDtypeStruct((n, w), F32)] * 8,
        scratch_shapes=[] if has_prev else [pltpu.VMEM((8, RWKV_BLOCK), F32)],
        compiler_params=_params("arbitrary"),
        name="rwkv_prep",
    )(*ins)


def _attend(q, kall, vall, mask, sink_ref, o_ref):
    nq = q.shape[0]
    lane = lax.broadcasted_iota(jnp.int32, (1, 2 * HEAD_DIM), 1)
    lo_m = lane < HEAD_DIM
    k_sw = pltpu.roll(kall, HEAD_DIM, 1)
    v_sw = pltpu.roll(vall, HEAD_DIM, 1)
    for kv in range(KV_HEADS):
        own = lo_m if kv == 0 else ~lo_m
        kdup = jnp.where(own, kall, k_sw).astype(BF16)
        v_own = jnp.where(own, vall, 0.0).astype(BF16)
        v_oth = jnp.where(own, 0.0, v_sw).astype(BF16)
        v_lo, v_hi = (v_own, v_oth) if kv == 0 else (v_oth, v_own)
        for pr in range(2):
            c0 = (kv * 2 + pr) * 2 * HEAD_DIM
            qp = q[:, c0:c0 + 2 * HEAD_DIM].astype(F32)
            qs = jnp.concatenate([jnp.where(lo_m, qp, 0.0), jnp.where(lo_m, 0.0, qp)], axis=0)
            s = _dot(qs.astype(BF16), kdup, NT) * (HEAD_DIM ** -0.5)
            s = jnp.where(jnp.concatenate([mask, mask], axis=0), s, NEG_INF)
            h0 = (kv * 2 + pr) * 2
            rowi = lax.broadcasted_iota(jnp.int32, (2 * nq, 1), 0)
            sink = jnp.where(rowi < nq, sink_ref[h0], sink_ref[h0 + 1])
            m = jnp.maximum(jnp.max(s, axis=-1, keepdims=True), sink)
            e = jnp.exp(s - m)
            den = jnp.sum(e, axis=-1, keepdims=True) + jnp.exp(sink - m)
            pr_f = e / den
            out = (_dot(pr_f[:nq].astype(BF16), v_lo)
                   + _dot(pr_f[nq:].astype(BF16), v_hi))
            o_ref[:, c0:c0 + 2 * HEAD_DIM] = out.astype(o_ref.dtype)


def _band_mask(nq, q0):
    qi = lax.broadcasted_iota(jnp.int32, (nq, 2 * WINDOW), 0) + q0
    kj = lax.broadcasted_iota(jnp.int32, (nq, 2 * WINDOW), 1)
    return (kj <= qi) & (qi - kj < WINDOW), kj


def _attn_prompt_kernel(sink_ref, q_ref, kp_ref, kc_ref, vp_ref, vc_ref, o_ref):
    j = pl.program_id(1)
    band, kj = _band_mask(WINDOW, WINDOW)
    mask = band & ((j > 0) | (kj >= WINDOW))
    kall = jnp.concatenate([kp_ref[0], kc_ref[0]], axis=0)
    vall = jnp.concatenate([vp_ref[0], vc_ref[0]], axis=0)
    _attend(q_ref[0], kall, vall, mask, sink_ref, o_ref.at[0])


def _attn_prompt(q, k, v, sinks):
    b, t, _ = q.shape
    nb = t // WINDOW
    cur = lambda w: pl.BlockSpec((1, WINDOW, w), lambda bi, j: (bi, j, 0))
    prv = lambda w: pl.BlockSpec((1, WINDOW, w), lambda bi, j: (bi, jnp.maximum(j - 1, 0), 0))
    return pl.pallas_call(
        _attn_prompt_kernel,
        grid=(b, nb),
        in_specs=[pl.BlockSpec(memory_space=pltpu.SMEM), cur(ATT_WIDTH), prv(KV_WIDTH), cur(KV_WIDTH),
                  prv(KV_WIDTH), cur(KV_WIDTH)],
        out_specs=cur(ATT_WIDTH),
        out_shape=jax.ShapeDtypeStruct((b, t, ATT_WIDTH), BF16),
        compiler_params=_params("arbitrary", "arbitrary"),
        name="attn_prompt",
    )(sinks, q, k, k, v, v)


def _attn_sample_kernel(sink_ref, q_ref, kn_ref, vn_ref, ck_ref, cv_ref, o_ref, kw_ref, vw_ref,
                        kall_ref, vall_ref, *, seqs, t):
    wb = ck_ref.shape[1]

    @pl.when(pl.program_id(0) == 0)
    def _():
        kall_ref[...] = jnp.zeros_like(kall_ref)
        vall_ref[...] = jnp.zeros_like(vall_ref)

    mask, _ = _band_mask(t, wb)
    for s in range(seqs):
        rows = slice(s * t, (s + 1) * t)
        kall_ref[0:wb, :] = ck_ref[s]
        vall_ref[0:wb, :] = cv_ref[s]
        kall_ref[wb:wb + t, :] = kn_ref[rows, :]
        vall_ref[wb:wb + t, :] = vn_ref[rows, :]
        _attend(q_ref[rows, :], kall_ref[...], vall_ref[...], mask, sink_ref, o_ref.at[rows, :])
        kw_ref[s] = kall_ref[t:t + wb, :]
        vw_ref[s] = vall_ref[t:t + wb, :]


def _attn_sample(q, k, v, cache_k, cache_v, sinks, *, t, seqs):
    n = q.shape[0]
    b, wb, _ = cache_k.shape
    row = lambda w: pl.BlockSpec((seqs * t, w), lambda i: (i, 0))
    win = pl.BlockSpec((seqs, wb, KV_WIDTH), lambda i: (i, 0, 0))
    return pl.pallas_call(
        functools.partial(_attn_sample_kernel, seqs=seqs, t=t),
        grid=(b // seqs,),
        in_specs=[pl.BlockSpec(memory_space=pltpu.SMEM), row(ATT_WIDTH), row(KV_WIDTH), row(KV_WIDTH), win, win],
        out_specs=[row(ATT_WIDTH), win, win],
        out_shape=[jax.ShapeDtypeStruct((n, ATT_WIDTH), F32),
                   jax.ShapeDtypeStruct((b, wb, KV_WIDTH), F32),
                   jax.ShapeDtypeStruct((b, wb, KV_WIDTH), F32)],
        scratch_shapes=[pltpu.VMEM((2 * WINDOW, KV_WIDTH), F32), pltpu.VMEM((2 * WINDOW, KV_WIDTH), F32)],
        compiler_params=_params("arbitrary"),
        name="attn_sample",
    )(sinks, q, k, v, cache_k, cache_v)


def _wkv_kernel(r_ref, l_ref, k_ref, v_ref, kk_ref, kb_ref, s0_ref, o_ref, st_ref, z_ref, *, chunk):
    c = chunk
    j = pl.program_id(1)
    eye_h = (lax.broadcasted_iota(jnp.int32, (HEAD_DIM, HEAD_DIM), 0)
             == lax.broadcasted_iota(jnp.int32, (HEAD_DIM, HEAD_DIM), 1))

    @pl.when(j == 0)
    def _():
        for h in range(RWKV_HEADS):
            z_ref[h] = s0_ref[0, h].T

    ri = lax.broadcasted_iota(jnp.int32, (c, c), 0)
    ci = lax.broadcasted_iota(jnp.int32, (c, c), 1)
    strict = ri > ci
    incl = ri >= ci
    eye_c = (ri == ci).astype(F32)
    tri = incl.astype(BF16)

    lw = l_ref[0]
    cum = sum(_dot(tri, part) for part in _split3(lw))
    tot = cum[c - 1:c, :]
    en = jnp.exp(-cum)
    ed = jnp.exp(tot - cum)
    e_tot = jnp.exp(tot)
    kk = kk_ref[0]
    kb = kb_ref[0]
    kx = k_ref[0]
    a_t = -kk * jnp.exp(cum - lw)
    r_t = r_ref[0] * jnp.exp(cum)
    b_t = kb * en
    k_t = kx * en
    b_h = kb * ed
    k_h = kx * ed
    vv = v_ref[0]

    for h in range(RWKV_HEADS):
        sl = slice(h * HEAD_DIM, (h + 1) * HEAD_DIM)
        a_s = _split2(a_t[:, sl])
        r_s = _split2(r_t[:, sl])
        b_s = _split2(b_t[:, sl])
        k_s = _split2(k_t[:, sl])
        v_s = _split2(vv[:, sl])
        a_ab = jnp.where(strict, _dot3(a_s, b_s, NT), 0.0)
        a_ak = jnp.where(strict, _dot3(a_s, k_s, NT), 0.0)
        a_rb = jnp.where(incl, _dot3(r_s, b_s, NT), 0.0)
        a_rk = jnp.where(incl, _dot3(r_s, k_s, NT), 0.0)
        t_m = eye_c + a_ab
        p_m = a_ab
        n = 2
        while n < c:
            p_s = _split2(p_m)
            p_m = _dot3(p_s, p_s)
            t_m = t_m + _dot3(_split2(t_m), _split2(p_m))
            n *= 2
        t_s = _split2(t_m)
        w_m = _dot3(t_s, a_s)
        u_m = _dot3(t_s, _split2(_dot3(_split2(a_ak), v_s)))
        w_s = _split2(w_m)
        u_s = _split2(u_m)
        rb_s = _split2(a_rb)
        r_p = r_t[:, sl] + _dot3(rb_s, w_s)
        o_i = _dot3(rb_s, u_s) + _dot3(_split2(a_rk), v_s)
        bh_s = _split2(b_h[:, sl])
        kh_s = _split2(k_h[:, sl])
        m_m = _dot3(bh_s, w_s, TN) + jnp.where(eye_h, e_tot[:, sl], 0.0)
        n_m = _dot3(bh_s, u_s, TN) + _dot3(kh_s, v_s, TN)
        z_s = _split2(z_ref[h])
        o_ref[0, :, sl] = _dot3(_split2(r_p), z_s) + o_i
        z_ref[h] = _dot3(_split2(m_m), z_s) + n_m

    @pl.when(j == pl.num_programs(1) - 1)
    def _():
        for h in range(RWKV_HEADS):
            st_ref[0, h] = z_ref[h].T


def _wkv(r, lw, k, v, kk, kb, s0, *, chunk):
    b, t, w = r.shape
    tok = pl.BlockSpec((1, chunk, w), lambda bi, j: (bi, j, 0))
    st = pl.BlockSpec((1, RWKV_HEADS, HEAD_DIM, HEAD_DIM), lambda bi, j: (bi, 0, 0, 0))
    return pl.pallas_call(
        functools.partial(_wkv_kernel, chunk=chunk),
        grid=(b, t // chunk),
        in_specs=[tok] * 6 + [st],
        out_specs=[tok, st],
        out_shape=[jax.ShapeDtypeStruct((b, t, w), F32),
                   jax.ShapeDtypeStruct((b, RWKV_HEADS, HEAD_DIM, HEAD_DIM), F32)],
        scratch_shapes=[pltpu.VMEM((RWKV_HEADS, HEAD_DIM, HEAD_DIM), F32)],
        compiler_params=_params("arbitrary", "arbitrary"),
        name="wkv_scan",
    )(r, lw, k, v, kk, kb, s0)


def _post_kernel(o_ref, bo_ref, g_ref, att_ref, x_ref, gng_ref, gnb_ref, ones_ref, wo_ref, gf_ref,
                 wg_ref, wu_ref, wd_ref, gfin_ref, y_ref, *, final):
    ones_bd = ones_ref[...]
    o = o_ref[...]
    mean = _seg_sum(o, ones_bd) * (1.0 / HEAD_DIM)
    oc = o - mean
    var = _seg_sum(oc * oc, ones_bd) * (1.0 / HEAD_DIM)
    on = oc * lax.rsqrt(var + GN_EPS) * gng_ref[...] + gnb_ref[...]
    rw = (on + bo_ref[...]) * g_ref[...]
    mix = _dot(att_ref[...].astype(BF16), wo_ref[:ATT_WIDTH, :]) + _dot(rw.astype(BF16), wo_ref[ATT_WIDTH:, :])
    x1 = x_ref[...] + mix
    u = _rms(x1, gf_ref[...]).astype(BF16)
    gate = _dot(u, wg_ref[...])
    up = _dot(u, wu_ref[...])
    hid = (gate * _sigmoid(gate) * up).astype(BF16)
    x2 = x1 + _dot(hid, wd_ref[...])
    y_ref[...] = _rms(x2, gfin_ref[...]) if final else x2


def _post(o, bonus, g, att, x, p, *, tm, final):
    n = x.shape[0]
    d_ff = p["w_gate"].shape[1]
    row = lambda w: pl.BlockSpec((tm, w), lambda i: (i, 0))
    once = lambda shape: pl.BlockSpec(shape, lambda i: (0,) * len(shape), pipeline_mode=pl.Buffered(1))
    w = RWKV_WIDTH
    return pl.pallas_call(
        functools.partial(_post_kernel, final=final),
        grid=(n // tm,),
        in_specs=[row(w), row(w), row(w), row(ATT_WIDTH), row(D_MODEL),
                  once((1, w)), once((1, w)), once((w, w)), once((ATT_WIDTH + w, D_MODEL)), once((1, D_MODEL)),
                  once((D_MODEL, d_ff)), once((D_MODEL, d_ff)), once((d_ff, D_MODEL)), once((1, D_MODEL))],
        out_specs=row(D_MODEL),
        out_shape=jax.ShapeDtypeStruct((n, D_MODEL), F32),
        compiler_params=_params("arbitrary"),
        name="post_ffn",
    )(o, bonus, g, att, x, p["gn_g"], p["gn_b"], p["ones_bd"], p["w_out"], p["g_ffn"],
      p["w_gate"], p["w_up"], p["w_down"], p["g_final"])


def _layer_params(l, g_mix, w_in, attn_sinks, rwkv_mu, w0, w2, a0, a2, g2, k_k, k_a, r_k, gn_g, gn_b,
                  w_out, g_ffn, w_gate, w_up, w_down, g_final):
    vec = lambda a: a.reshape(1, -1).astype(F32)
    hd = jnp.arange(RWKV_WIDTH) // HEAD_DIM
    return dict(
        g_mix=g_mix[l], w_in=w_in[l].astype(BF16), sinks=attn_sinks[l].astype(F32),
        mu=vec(rwkv_mu[l]), w0=vec(w0[l]), w2=w2[l].astype(BF16), a0=vec(a0[l]), a2=a2[l].astype(BF16),
        g2=g2[l].astype(BF16), k_k=vec(k_k[l]), k_a=vec(k_a[l]), r_k=vec(r_k[l]),
        gn_g=vec(gn_g[l]), gn_b=vec(gn_b[l]), w_out=w_out[l].astype(BF16), g_ffn=vec(g_ffn[l]),
        w_gate=w_gate[l].astype(BF16), w_up=w_up[l].astype(BF16), w_down=w_down[l].astype(BF16),
        g_final=vec(g_final), ones_bd=(hd[:, None] == hd[None, :]).astype(BF16),
    )


def _pick(n, pref):
    t = pref
    while n % t:
        t //= 2
    return t


def _prompt_layer(x, p, final):
    b, t, d = x.shape
    n = b * t
    x2 = x.reshape(n, d)
    tm = _pick(t, 512)
    q, k, v, feat = _inproj(x2, p["g_mix"], p["w_in"], normalize=True, tm=tm)
    r, lw, k2, vv, kk, kb, g, bonus = _prep(feat, None, p, tm=tm, blocks_per_seq=t // tm)
    att = _attn_prompt(q.reshape(b, t, -1), k.reshape(b, t, -1), v.reshape(b, t, -1), p["sinks"])
    s0 = jnp.zeros((b, RWKV_HEADS, HEAD_DIM, HEAD_DIM), F32)
    sh = lambda a: a.reshape(b, t, RWKV_WIDTH)
    o, s_new = _wkv(sh(r), sh(lw), sh(k2), sh(vv), sh(kk), sh(kb), s0, chunk=_pick(t, 64))
    y = _post(o.reshape(n, -1), bonus, g, att.reshape(n, -1), x2, p, tm=_pick(n, 256), final=final)
    wp = min(WINDOW, t)
    k_win = k.reshape(b, t, KV_HEADS, HEAD_DIM)[:, t - wp:]
    v_win = v.reshape(b, t, KV_HEADS, HEAD_DIM)[:, t - wp:]
    shift = _rms_rows(x[:, -1], p["g_mix"])
    return y.reshape(b, t, d), k_win, v_win, s_new, shift


def _sample_layer(x, h_prev, k_buf, v_buf, s0, p, final, *, chunk=64):
    b, t, d = x.shape
    n = b * t
    x2 = x.reshape(n, d)
    wb = k_buf.shape[1]
    q, k, v, feat = _inproj(x2, p["g_mix"], p["w_in"], normalize=True, tm=_pick(n, 512), q_dtype=F32)
    _, _, _, feat_prev = _inproj(h_prev, p["g_mix"], p["w_in"], normalize=False, tm=_pick(b, 512))
    f3 = feat.reshape(b, t, -1)
    prev = jnp.concatenate([feat_prev[:, None], f3[:, :-1]], axis=1).reshape(n, -1)
    r, lw, k2, vv, kk, kb, g, bonus = _prep(feat, prev, p, tm=_pick(n, 512), blocks_per_seq=1)
    att, k_win, v_win = _attn_sample(q, k, v, k_buf.reshape(b, wb, -1), v_buf.reshape(b, wb, -1),
                                     p["sinks"], t=t, seqs=_pick(b, 8))
    tp = -(-t // chunk) * chunk
    pad = lambda a: jnp.pad(a.reshape(b, t, RWKV_WIDTH), ((0, 0), (0, tp - t), (0, 0)))
    o, s_new = _wkv(pad(r), pad(lw), pad(k2), pad(vv), pad(kk), pad(kb), s0, chunk=chunk)
    y = _post(o[:, :t].reshape(n, -1), bonus, g, att, x2, p, tm=_pick(n, 256), final=final)
    shift = _rms_rows(x[:, -1], p["g_mix"])
    return (y.reshape(b, t, d), k_win.reshape(b, wb, KV_HEADS, HEAD_DIM), v_win.reshape(b, wb, KV_HEADS, HEAD_DIM),
            s_new, shift)


def kernel(x_prompt, x_sample, cache_k, cache_v, state_wkv, state_shift, g_mix, w_in, attn_sinks, rwkv_mu, w0, w2,
           a0, a2, g2, k_k, k_a, r_k, gn_g, gn_b, w_out, g_ffn, w_gate, w_up, w_down, g_final):
    depth = w_in.shape[0]
    xp, xs = x_prompt, x_sample
    outs_p, outs_s = [], []
    for l in range(depth):
        p = _layer_params(l, g_mix, w_in, attn_sinks, rwkv_mu, w0, w2, a0, a2, g2, k_k, k_a, r_k, gn_g, gn_b,
                          w_out, g_ffn, w_gate, w_up, w_down, g_final)
        final = l == depth - 1
        xp, kp, vp, sp, hp = _prompt_layer(xp, p, final)
        xs, kn, vn, sn, hn = _sample_layer(xs, state_shift[l], cache_k[l], cache_v[l], state_wkv[l], p, final)
        outs_p.append((kp, vp, sp, hp))
        outs_s.append((kn, vn, sn, hn))
    stack = lambda outs, i: jnp.stack([o[i] for o in outs])
    return (xp, xs,
            stack(outs_p, 0), stack(outs_p, 1), stack(outs_p, 2), stack(outs_p, 3),
            stack(outs_s, 0), stack(outs_s, 1), stack(outs_s, 2), stack(outs_s, 3))
```

```python
import functools

import jax
import jax.numpy as jnp
from jax import lax
from jax.experimental import pallas as pl
from jax.experimental.pallas import tpu as pltpu

F32 = jnp.float32
BF16 = jnp.bfloat16

D_MODEL = 1024
HEAD_DIM = 64
ATT_HEADS = 8
KV_HEADS = 2
ATT_WIDTH = ATT_HEADS * HEAD_DIM
KV_WIDTH = KV_HEADS * HEAD_DIM
RWKV_HEADS = 8
RWKV_WIDTH = RWKV_HEADS * HEAD_DIM
WINDOW = 128
DECAY_LORA = 64
ICL_LORA = 64
GATE_LORA = 128
RWKV_BLOCK = 3 * RWKV_WIDTH + DECAY_LORA + ICL_LORA + GATE_LORA
ATT_PROJ = ATT_WIDTH + 2 * KV_WIDTH
PROJ_WIDTH = ATT_PROJ + RWKV_BLOCK
RMS_EPS = 1e-6
GN_EPS = 64e-5
L2_EPS = 1e-12
NEG_INF = -1e30

V7X_VMEM_BYTES = 64 * 1024 * 1024
VMEM_LIMIT_BYTES = V7X_VMEM_BYTES * 3 // 4

NN = (((1,), (0,)), ((), ()))
NT = (((1,), (1,)), ((), ()))
TN = (((0,), (0,)), ((), ()))


def _dot(a, b, dn=NN):
    return lax.dot_general(a, b, dn, preferred_element_type=F32)


def _split2(x):
    hi = x.astype(BF16)
    lo = (x - hi.astype(F32)).astype(BF16)
    return hi, lo


def _split3(x):
    hi = x.astype(BF16)
    r1 = x - hi.astype(F32)
    mid = r1.astype(BF16)
    lo = (r1 - mid.astype(F32)).astype(BF16)
    return hi, mid, lo


def _dot3(xs, ys, dn=NN):
    xh, xl = xs
    yh, yl = ys
    return _dot(xh, yh, dn) + (_dot(xh, yl, dn) + _dot(xl, yh, dn))


def _seg_sum(x, ones_bd):
    hi, lo = _split2(x)
    return _dot(hi, ones_bd) + _dot(lo, ones_bd)


def _rms(x, g):
    return x * lax.rsqrt(jnp.mean(x * x, axis=-1, keepdims=True) + RMS_EPS) * g


def _const_spec(shape):
    return pl.BlockSpec(shape, lambda *_: (0,) * len(shape))


def _params(*sem):
    return pltpu.CompilerParams(dimension_semantics=sem, vmem_limit_bytes=VMEM_LIMIT_BYTES)


def _rms_rows_kernel(x_ref, g_ref, o_ref):
    o_ref[...] = _rms(x_ref[...], g_ref[...])


def _rms_rows(x, g):
    n, d = x.shape
    return pl.pallas_call(
        _rms_rows_kernel,
        out_shape=jax.ShapeDtypeStruct((n, d), F32),
        name="rms_rows",
    )(x, g.reshape(1, d))


def _inproj_kernel(x_ref, g_ref, w_ref, q_ref, k_ref, v_ref, f_ref, *, normalize):
    x = x_ref[...]
    h = _rms(x, g_ref[...]) if normalize else x
    proj = _dot(h.astype(BF16), w_ref[...])
    q_ref[...] = proj[:, :ATT_WIDTH].astype(q_ref.dtype)
    k_ref[...] = proj[:, ATT_WIDTH:ATT_WIDTH + KV_WIDTH]
    v_ref[...] = proj[:, ATT_WIDTH + KV_WIDTH:ATT_PROJ]
    f_ref[...] = proj[:, ATT_PROJ:]


def _inproj(x, g_mix, w_in_bf, *, normalize, tm, q_dtype=BF16):
    n = x.shape[0]
    row = lambda w: pl.BlockSpec((tm, w), lambda i: (i, 0))
    return pl.pallas_call(
        functools.partial(_inproj_kernel, normalize=normalize),
        grid=(n // tm,),
        in_specs=[row(D_MODEL), _const_spec((1, D_MODEL)), _const_spec((D_MODEL, PROJ_WIDTH))],
        out_specs=[row(ATT_WIDTH), row(KV_WIDTH), row(KV_WIDTH), row(RWKV_BLOCK)],
        out_shape=[
            jax.ShapeDtypeStruct((n, ATT_WIDTH), q_dtype),
            jax.ShapeDtypeStruct((n, KV_WIDTH), F32),
            jax.ShapeDtypeStruct((n, KV_WIDTH), F32),
            jax.ShapeDtypeStruct((n, RWKV_BLOCK), F32),
        ],
        compiler_params=_params("arbitrary"),
        name="inproj",
    )(x, g_mix.reshape(1, D_MODEL), w_in_bf)


def _softplus(z):
    return jnp.maximum(z, 0.0) + jnp.log(1.0 + jnp.exp(-jnp.abs(z)))


def _sigmoid(z):
    return 1.0 / (1.0 + jnp.exp(-z))


def _prep_kernel(*refs, blocks_per_seq, has_prev):
    if has_prev:
        f_ref, p_ref = refs[:2]
        refs = refs[2:]
    else:
        f_ref = refs[0]
        refs = refs[1:]
    (mu_ref, w0_ref, w2_ref, a0_ref, a2_ref, g2_ref, kk_ref, ka_ref, rk_ref, ones_ref,
     r_out, l_out, k_out, v_out, kk_out, kb_out, g_out, bo_out) = refs[:18]
    f = f_ref[...]
    tm = f.shape[0]
    if has_prev:
        prev = p_ref[...]
    else:
        carry_ref = refs[18]
        j = pl.program_id(0)

        @pl.when(j % blocks_per_seq == 0)
        def _():
            carry_ref[...] = jnp.zeros_like(carry_ref)

        rolled = pltpu.roll(f, 1, 0)
        row = lax.broadcasted_iota(jnp.int32, f.shape, 0)
        prev = jnp.where(row == 0, carry_ref[0:1, :], rolled)
        carry_ref[0:1, :] = f[tm - 1:tm, :]
    xs = f + (prev - f) * mu_ref[...]
    w3 = 3 * RWKV_WIDTH
    r = xs[:, :RWKV_WIDTH]
    k = xs[:, RWKV_WIDTH:2 * RWKV_WIDTH]
    v = xs[:, 2 * RWKV_WIDTH:w3]
    wl = xs[:, w3:w3 + DECAY_LORA]
    al = xs[:, w3 + DECAY_LORA:w3 + DECAY_LORA + ICL_LORA]
    gl = xs[:, w3 + DECAY_LORA + ICL_LORA:]
    wr = -_softplus(-(w0_ref[...] + _dot(jnp.tanh(wl).astype(BF16), w2_ref[...]))) - 0.5
    a = _sigmoid(a0_ref[...] + _dot(al.astype(BF16), a2_ref[...]))
    g = _dot(_sigmoid(gl).astype(BF16), g2_ref[...])
    ones_bd = ones_ref[...]
    kk = k * kk_ref[...]
    kk = kk * lax.rsqrt(_seg_sum(kk * kk, ones_bd) + L2_EPS)
    k2 = k * (1.0 + (a - 1.0) * ka_ref[...])
    r_out[...] = r
    l_out[...] = -jnp.exp(wr)
    k_out[...] = k2
    v_out[...] = v
    kk_out[...] = kk
    kb_out[...] = kk * a
    g_out[...] = g
    bo_out[...] = _seg_sum(r * k2 * rk_ref[...], ones_bd) * v


def _prep(feat, prev, p, *, tm, blocks_per_seq):
    n = feat.shape[0]
    has_prev = prev is not None
    row = lambda w: pl.BlockSpec((tm, w), lambda i: (i, 0))
    w = RWKV_WIDTH
    ins = [feat] + ([prev] if has_prev else [])
    in_specs = [row(RWKV_BLOCK)] * len(ins)
    ins += [p["mu"], p["w0"], p["w2"], p["a0"], p["a2"], p["g2"], p["k_k"], p["k_a"], p["r_k"], p["ones_bd"]]
    in_specs += [_const_spec((1, RWKV_BLOCK)), _const_spec((1, w)), _const_spec((DECAY_LORA, w)),
                 _const_spec((1, w)), _const_spec((ICL_LORA, w)), _const_spec((GATE_LORA, w)),
                 _const_spec((1, w)), _const_spec((1, w)), _const_spec((1, w)), _const_spec((w, w))]
    return pl.pallas_call(
        functools.partial(_prep_kernel, blocks_per_seq=blocks_per_seq, has_prev=has_prev),
        grid=(n // tm,),
        in_specs=in_specs,
        out_specs=[row(w)] * 8,
        out_shape=[jax.ShapeDtypeStruct((n, w), F32)] * 8,
        scratch_shapes=[] if has_prev else [pltpu.VMEM((8, RWKV_BLOCK), F32)],
        compiler_params=_params("arbitrary"),
        name="rwkv_prep",
    )(*ins)


def _attend(q, kall, vall, mask, sink_ref, o_ref):
    nq = q.shape[0]
    lane = lax.broadcasted_iota(jnp.int32, (1, 2 * HEAD_DIM), 1)
    lo_m = lane < HEAD_DIM
    k_sw = pltpu.roll(kall, HEAD_DIM, 1)
    v_sw = pltpu.roll(vall, HEAD_DIM, 1)
    for kv in range(KV_HEADS):
        own = lo_m if kv == 0 else ~lo_m
        kdup = jnp.where(own, kall, k_sw).astype(BF16)
        v_own = jnp.where(own, vall, 0.0).astype(BF16)
        v_oth = jnp.where(own, 0.0, v_sw).astype(BF16)
        v_lo, v_hi = (v_own, v_oth) if kv == 0 else (v_oth, v_own)
        for pr in range(2):
            c0 = (kv * 2 + pr) * 2 * HEAD_DIM
            qp = q[:, c0:c0 + 2 * HEAD_DIM].astype(F32)
            qs = jnp.concatenate([jnp.where(lo_m, qp, 0.0), jnp.where(lo_m, 0.0, qp)], axis=0)
            s = _dot(qs.astype(BF16), kdup, NT) * (HEAD_DIM ** -0.5)
            s = jnp.where(jnp.concatenate([mask, mask], axis=0), s, NEG_INF)
            h0 = (kv * 2 + pr) * 2
            rowi = lax.broadcasted_iota(jnp.int32, (2 * nq, 1), 0)
            sink = jnp.where(rowi < nq, sink_ref[h0], sink_ref[h0 + 1])
            m = jnp.maximum(jnp.max(s, axis=-1, keepdims=True), sink)
            e = jnp.exp(s - m)
            den = jnp.sum(e, axis=-1, keepdims=True) + jnp.exp(sink - m)
            pr_f = e / den
            out = (_dot(pr_f[:nq].astype(BF16), v_lo)
                   + _dot(pr_f[nq:].astype(BF16), v_hi))
            o_ref[:, c0:c0 + 2 * HEAD_DIM] = out.astype(o_ref.dtype)


def _band_mask(nq, q0):
    qi = lax.broadcasted_iota(jnp.int32, (nq, 2 * WINDOW), 0) + q0
    kj = lax.broadcasted_iota(jnp.int32, (nq, 2 * WINDOW), 1)
    return (kj <= qi) & (qi - kj < WINDOW), kj


def _attn_prompt_kernel(sink_ref, q_ref, kp_ref, kc_ref, vp_ref, vc_ref, o_ref):
    j = pl.program_id(1)
    band, kj = _band_mask(WINDOW, WINDOW)
    mask = band & ((j > 0) | (kj >= WINDOW))
    kall = jnp.concatenate([kp_ref[0], kc_ref[0]], axis=0)
    vall = jnp.concatenate([vp_ref[0], vc_ref[0]], axis=0)
    _attend(q_ref[0], kall, vall, mask, sink_ref, o_ref.at[0])


def _attn_prompt(q, k, v, sinks):
    b, t, _ = q.shape
    nb = t // WINDOW
    cur = lambda w: pl.BlockSpec((1, WINDOW, w), lambda bi, j: (bi, j, 0))
    prv = lambda w: pl.BlockSpec((1, WINDOW, w), lambda bi, j: (bi, jnp.maximum(j - 1, 0), 0))
    return pl.pallas_call(
        _attn_prompt_kernel,
        grid=(b, nb),
        in_specs=[pl.BlockSpec(memory_space=pltpu.SMEM), cur(ATT_WIDTH), prv(KV_WIDTH), cur(KV_WIDTH),
                  prv(KV_WIDTH), cur(KV_WIDTH)],
        out_specs=cur(ATT_WIDTH),
        out_shape=jax.ShapeDtypeStruct((b, t, ATT_WIDTH), BF16),
        compiler_params=_params("arbitrary", "arbitrary"),
        name="attn_prompt",
    )(sinks, q, k, k, v, v)


def _attn_sample_kernel(sink_ref, q_ref, kn_ref, vn_ref, ck_ref, cv_ref, o_ref, kw_ref, vw_ref,
                        kall_ref, vall_ref, *, seqs, t):
    wb = ck_ref.shape[1]

    @pl.when(pl.program_id(0) == 0)
    def _():
        kall_ref[...] = jnp.zeros_like(kall_ref)
        vall_ref[...] = jnp.zeros_like(vall_ref)

    mask, _ = _band_mask(t, wb)
    for s in range(seqs):
        rows = slice(s * t, (s + 1) * t)
        kall_ref[0:wb, :] = ck_ref[s]
        vall_ref[0:wb, :] = cv_ref[s]
        kall_ref[wb:wb + t, :] = kn_ref[rows, :]
        vall_ref[wb:wb + t, :] = vn_ref[rows, :]
        _attend(q_ref[rows, :], kall_ref[...], vall_ref[...], mask, sink_ref, o_ref.at[rows, :])
        kw_ref[s] = kall_ref[t:t + wb, :]
        vw_ref[s] = vall_ref[t:t + wb, :]


def _attn_sample(q, k, v, cache_k, cache_v, sinks, *, t, seqs):
    n = q.shape[0]
    b, wb, _ = cache_k.shape
    row = lambda w: pl.BlockSpec((seqs * t, w), lambda i: (i, 0))
    win = pl.BlockSpec((seqs, wb, KV_WIDTH), lambda i: (i, 0, 0))
    return pl.pallas_call(
        functools.partial(_attn_sample_kernel, seqs=seqs, t=t),
        grid=(b // seqs,),
        in_specs=[pl.BlockSpec(memory_space=pltpu.SMEM), row(ATT_WIDTH), row(KV_WIDTH), row(KV_WIDTH), win, win],
        out_specs=[row(ATT_WIDTH), win, win],
        out_shape=[jax.ShapeDtypeStruct((n, ATT_WIDTH), F32),
                   jax.ShapeDtypeStruct((b, wb, KV_WIDTH), F32),
                   jax.ShapeDtypeStruct((b, wb, KV_WIDTH), F32)],
        scratch_shapes=[pltpu.VMEM((2 * WINDOW, KV_WIDTH), F32), pltpu.VMEM((2 * WINDOW, KV_WIDTH), F32)],
        compiler_params=_params("arbitrary"),
        name="attn_sample",
    )(sinks, q, k, v, cache_k, cache_v)


def _wkv_kernel(r_ref, l_ref, k_ref, v_ref, kk_ref, kb_ref, s0_ref, o_ref, st_ref, z_ref, *, chunk):
    c = chunk
    j = pl.program_id(1)
    eye_h = (lax.broadcasted_iota(jnp.int32, (HEAD_DIM, HEAD_DIM), 0)
             == lax.broadcasted_iota(jnp.int32, (HEAD_DIM, HEAD_DIM), 1))

    @pl.when(j == 0)
    def _():
        for h in range(RWKV_HEADS):
            z_ref[h] = s0_ref[0, h].T

    ri = lax.broadcasted_iota(jnp.int32, (c, c), 0)
    ci = lax.broadcasted_iota(jnp.int32, (c, c), 1)
    strict = ri > ci
    incl = ri >= ci
    eye_c = (ri == ci).astype(F32)
    tri = incl.astype(BF16)

    lw = l_ref[0]
    cum = sum(_dot(tri, part) for part in _split3(lw))
    tot = cum[c - 1:c, :]
    en = jnp.exp(-cum)
    ed = jnp.exp(tot - cum)
    e_tot = jnp.exp(tot)
    kk = kk_ref[0]
    kb = kb_ref[0]
    kx = k_ref[0]
    a_t = -kk * jnp.exp(cum - lw)
    r_t = r_ref[0] * jnp.exp(cum)
    b_t = kb * en
    k_t = kx * en
    b_h = kb * ed
    k_h = kx * ed
    vv = v_ref[0]

    heads = range(RWKV_HEADS)
    hs = lambda x, h: x[:, h * HEAD_DIM:(h + 1) * HEAD_DIM]
    each = lambda f: [f(h) for h in heads]
    lower2 = jnp.concatenate([strict, incl], axis=0)
    ar_s = each(lambda h: _split2(jnp.concatenate([hs(a_t, h), hs(r_t, h)], axis=0)))
    b_s = each(lambda h: _split2(hs(b_t, h)))
    k_s = each(lambda h: _split2(hs(k_t, h)))
    v_s = each(lambda h: _split2(hs(vv, h)))
    g_b = each(lambda h: jnp.where(lower2, _dot3(ar_s[h], b_s[h], NT), 0.0))
    g_k = each(lambda h: jnp.where(lower2, _dot3(ar_s[h], k_s[h], NT), 0.0))
    p_m = each(lambda h: g_b[h][:c])
    t_m = each(lambda h: eye_c + p_m[h])
    n = 2
    while n < c:
        p_s = each(lambda h: _split2(p_m[h]))
        p_m = each(lambda h: _dot3(p_s[h], p_s[h]))
        t_s = each(lambda h: _split2(t_m[h]))
        p_s = each(lambda h: _split2(p_m[h]))
        t_m = each(lambda h: t_m[h] + _dot3(t_s[h], p_s[h]))
        n *= 2
    t_s = each(lambda h: _split2(t_m[h]))
    av_s = each(lambda h: _split2(_dot3(_split2(g_k[h][:c]), v_s[h])))
    w_m = each(lambda h: _dot3(t_s[h], (ar_s[h][0][:c], ar_s[h][1][:c])))
    u_m = each(lambda h: _dot3(t_s[h], av_s[h]))
    w_s = each(lambda h: _split2(w_m[h]))
    u_s = each(lambda h: _split2(u_m[h]))
    rb_s = each(lambda h: _split2(g_b[h][c:]))
    rk_s = each(lambda h: _split2(g_k[h][c:]))
    r_p = each(lambda h: hs(r_t, h) + _dot3(rb_s[h], w_s[h]))
    o_i = each(lambda h: _dot3(rb_s[h], u_s[h]) + _dot3(rk_s[h], v_s[h]))
    bh_s = each(lambda h: _split2(hs(b_h, h)))
    kh_s = each(lambda h: _split2(hs(k_h, h)))
    m_m = each(lambda h: _dot3(bh_s[h], w_s[h], TN) + jnp.where(eye_h, hs(e_tot, h), 0.0))
    n_m = each(lambda h: _dot3(bh_s[h], u_s[h], TN) + _dot3(kh_s[h], v_s[h], TN))
    rm_s = each(lambda h: _split2(jnp.concatenate([r_p[h], m_m[h]], axis=0)))
    z_s = each(lambda h: _split2(z_ref[h]))
    oz = each(lambda h: _dot3(rm_s[h], z_s[h]))
    for h in heads:
        o_ref[0, :, h * HEAD_DIM:(h + 1) * HEAD_DIM] = oz[h][:c] + o_i[h]
        z_ref[h] = oz[h][c:] + n_m[h]

    @pl.when(j == pl.num_programs(1) - 1)
    def _():
        for h in range(RWKV_HEADS):
            st_ref[0, h] = z_ref[h].T


def _wkv(r, lw, k, v, kk, kb, s0, *, chunk):
    b, t, w = r.shape
    tok = pl.BlockSpec((1, chunk, w), lambda bi, j: (bi, j, 0))
    st = pl.BlockSpec((1, RWKV_HEADS, HEAD_DIM, HEAD_DIM), lambda bi, j: (bi, 0, 0, 0))
    return pl.pallas_call(
        functools.partial(_wkv_kernel, chunk=chunk),
        grid=(b, t // chunk),
        in_specs=[tok] * 6 + [st],
        out_specs=[tok, st],
        out_shape=[jax.ShapeDtypeStruct((b, t, w), F32),
                   jax.ShapeDtypeStruct((b, RWKV_HEADS, HEAD_DIM, HEAD_DIM), F32)],
        scratch_shapes=[pltpu.VMEM((RWKV_HEADS, HEAD_DIM, HEAD_DIM), F32)],
        compiler_params=_params("arbitrary", "arbitrary"),
        name="wkv_scan",
    )(r, lw, k, v, kk, kb, s0)


def _post_kernel(o_ref, bo_ref, g_ref, att_ref, x_ref, gng_ref, gnb_ref, ones_ref, wo_ref, gf_ref,
                 wg_ref, wu_ref, wd_ref, gfin_ref, y_ref, *, final):
    ones_bd = ones_ref[...]
    o = o_ref[...]
    mean = _seg_sum(o, ones_bd) * (1.0 / HEAD_DIM)
    oc = o - mean
    var = _seg_sum(oc * oc, ones_bd) * (1.0 / HEAD_DIM)
    on = oc * lax.rsqrt(var + GN_EPS) * gng_ref[...] + gnb_ref[...]
    rw = (on + bo_ref[...]) * g_ref[...]
    mix = _dot(att_ref[...].astype(BF16), wo_ref[:ATT_WIDTH, :]) + _dot(rw.astype(BF16), wo_ref[ATT_WIDTH:, :])
    x1 = x_ref[...] + mix
    u = _rms(x1, gf_ref[...]).astype(BF16)
    gate = _dot(u, wg_ref[...])
    up = _dot(u, wu_ref[...])
    hid = (gate * _sigmoid(gate) * up).astype(BF16)
    x2 = x1 + _dot(hid, wd_ref[...])
    y_ref[...] = _rms(x2, gfin_ref[...]) if final else x2


def _post(o, bonus, g, att, x, p, *, tm, final):
    n = x.shape[0]
    d_ff = p["w_gate"].shape[1]
    row = lambda w: pl.BlockSpec((tm, w), lambda i: (i, 0))
    once = lambda shape: pl.BlockSpec(shape, lambda i: (0,) * len(shape), pipeline_mode=pl.Buffered(1))
    w = RWKV_WIDTH
    return pl.pallas_call(
        functools.partial(_post_kernel, final=final),
        grid=(n // tm,),
        in_specs=[row(w), row(w), row(w), row(ATT_WIDTH), row(D_MODEL),
                  once((1, w)), once((1, w)), once((w, w)), once((ATT_WIDTH + w, D_MODEL)), once((1, D_MODEL)),
                  once((D_MODEL, d_ff)), once((D_MODEL, d_ff)), once((d_ff, D_MODEL)), once((1, D_MODEL))],
        out_specs=row(D_MODEL),
        out_shape=jax.ShapeDtypeStruct((n, D_MODEL), F32),
        compiler_params=_params("arbitrary"),
        name="post_ffn",
    )(o, bonus, g, att, x, p["gn_g"], p["gn_b"], p["ones_bd"], p["w_out"], p["g_ffn"],
      p["w_gate"], p["w_up"], p["w_down"], p["g_final"])


def _layer_params(l, g_mix, w_in, attn_sinks, rwkv_mu, w0, w2, a0, a2, g2, k_k, k_a, r_k, gn_g, gn_b,
                  w_out, g_ffn, w_gate, w_up, w_down, g_final):
    vec = lambda a: a.reshape(1, -1).astype(F32)
    hd = jnp.arange(RWKV_WIDTH) // HEAD_DIM
    return dict(
        g_mix=g_mix[l], w_in=w_in[l].astype(BF16), sinks=attn_sinks[l].astype(F32),
        mu=vec(rwkv_mu[l]), w0=vec(w0[l]), w2=w2[l].astype(BF16), a0=vec(a0[l]), a2=a2[l].astype(BF16),
        g2=g2[l].astype(BF16), k_k=vec(k_k[l]), k_a=vec(k_a[l]), r_k=vec(r_k[l]),
        gn_g=vec(gn_g[l]), gn_b=vec(gn_b[l]), w_out=w_out[l].astype(BF16), g_ffn=vec(g_ffn[l]),
        w_gate=w_gate[l].astype(BF16), w_up=w_up[l].astype(BF16), w_down=w_down[l].astype(BF16),
        g_final=vec(g_final), ones_bd=(hd[:, None] == hd[None, :]).astype(BF16),
    )


def _pick(n, pref):
    t = pref
    while n % t:
        t //= 2
    return t


def _prompt_layer(x, p, final):
    b, t, d = x.shape
    n = b * t
    x2 = x.reshape(n, d)
    tm = _pick(t, 512)
    q, k, v, feat = _inproj(x2, p["g_mix"], p["w_in"], normalize=True, tm=tm)
    r, lw, k2, vv, kk, kb, g, bonus = _prep(feat, None, p, tm=tm, blocks_per_seq=t // tm)
    att = _attn_prompt(q.reshape(b, t, -1), k.reshape(b, t, -1), v.reshape(b, t, -1), p["sinks"])
    s0 = jnp.zeros((b, RWKV_HEADS, HEAD_DIM, HEAD_DIM), F32)
    sh = lambda a: a.reshape(b, t, RWKV_WIDTH)
    o, s_new = _wkv(sh(r), sh(lw), sh(k2), sh(vv), sh(kk), sh(kb), s0, chunk=_pick(t, 64))
    y = _post(o.reshape(n, -1), bonus, g, att.reshape(n, -1), x2, p, tm=_pick(n, 256), final=final)
    wp = min(WINDOW, t)
    k_win = k.reshape(b, t, KV_HEADS, HEAD_DIM)[:, t - wp:]
    v_win = v.reshape(b, t, KV_HEADS, HEAD_DIM)[:, t - wp:]
    shift = _rms_rows(x[:, -1], p["g_mix"])
    return y.reshape(b, t, d), k_win, v_win, s_new, shift


def _sample_layer(x, h_prev, k_buf, v_buf, s0, p, final, *, chunk=16):
    b, t, d = x.shape
    n = b * t
    x2 = x.reshape(n, d)
    wb = k_buf.shape[1]
    q, k, v, feat = _inproj(x2, p["g_mix"], p["w_in"], normalize=True, tm=_pick(n, 512), q_dtype=F32)
    _, _, _, feat_prev = _inproj(h_prev, p["g_mix"], p["w_in"], normalize=False, tm=_pick(b, 512))
    f3 = feat.reshape(b, t, -1)
    prev = jnp.concatenate([feat_prev[:, None], f3[:, :-1]], axis=1).reshape(n, -1)
    r, lw, k2, vv, kk, kb, g, bonus = _prep(feat, prev, p, tm=_pick(n, 512), blocks_per_seq=1)
    att, k_win, v_win = _attn_sample(q, k, v, k_buf.reshape(b, wb, -1), v_buf.reshape(b, wb, -1),
                                     p["sinks"], t=t, seqs=_pick(b, 8))
    tp = -(-t // chunk) * chunk
    pad = lambda a: jnp.pad(a.reshape(b, t, RWKV_WIDTH), ((0, 0), (0, tp - t), (0, 0)))
    o, s_new = _wkv(pad(r), pad(lw), pad(k2), pad(vv), pad(kk), pad(kb), s0, chunk=chunk)
    y = _post(o[:, :t].reshape(n, -1), bonus, g, att, x2, p, tm=_pick(n, 256), final=final)
    shift = _rms_rows(x[:, -1], p["g_mix"])
    return (y.reshape(b, t, d), k_win.reshape(b, wb, KV_HEADS, HEAD_DIM), v_win.reshape(b, wb, KV_HEADS, HEAD_DIM),
            s_new, shift)


def kernel(x_prompt, x_sample, cache_k, cache_v, state_wkv, state_shift, g_mix, w_in, attn_sinks, rwkv_mu, w0, w2,
           a0, a2, g2, k_k, k_a, r_k, gn_g, gn_b, w_out, g_ffn, w_gate, w_up, w_down, g_final):
    depth = w_in.shape[0]
    xp, xs = x_prompt, x_sample
    outs_p, outs_s = [], []
    for l in range(depth):
        p = _layer_params(l, g_mix, w_in, attn_sinks, rwkv_mu, w0, w2, a0, a2, g2, k_k, k_a, r_k, gn_g, gn_b,
                          w_out, g_ffn, w_gate, w_up, w_down, g_final)
        final = l == depth - 1
        xp, kp, vp, sp, hp = _prompt_layer(xp, p, final)
        xs, kn, vn, sn, hn = _sample_layer(xs, state_shift[l], cache_k[l], cache_v[l], state_wkv[l], p, final)
        outs_p.append((kp, vp, sp, hp))
        outs_s.append((kn, vn, sn, hn))
    stack = lambda outs, i: jnp.stack([o[i] for o in outs])
    return (xp, xs,
            stack(outs_p, 0), stack(outs_p, 1), stack(outs_p, 2), stack(outs_p, 3),
            stack(outs_s, 0), stack(outs_s, 1), stack(outs_s, 2), stack(outs_s, 3))
```

```python
import functools

import jax
import jax.numpy as jnp
from jax import lax
from jax.experimental import pallas as pl
from jax.experimental.pallas import tpu as pltpu

F32 = jnp.float32
BF16 = jnp.bfloat16

D_MODEL = 1024
HEAD_DIM = 64
ATT_HEADS = 8
KV_HEADS = 2
ATT_WIDTH = ATT_HEADS * HEAD_DIM
KV_WIDTH = KV_HEADS * HEAD_DIM
RWKV_HEADS = 8
RWKV_WIDTH = RWKV_HEADS * HEAD_DIM
WINDOW = 128
DECAY_LORA = 64
ICL_LORA = 64
GATE_LORA = 128
RWKV_BLOCK = 3 * RWKV_WIDTH + DECAY_LORA + ICL_LORA + GATE_LORA
ATT_PROJ = ATT_WIDTH + 2 * KV_WIDTH
PROJ_WIDTH = ATT_PROJ + RWKV_BLOCK
RMS_EPS = 1e-6
GN_EPS = 64e-5
L2_EPS = 1e-12
NEG_INF = -1e30

V7X_VMEM_BYTES = 64 * 1024 * 1024
VMEM_LIMIT_BYTES = V7X_VMEM_BYTES * 3 // 4

WKV_PROMPT_SEQS = 2
WKV_PROMPT_CHUNKS = 2
WKV_SAMPLE_SEQS = 8
BF16_ROWS = 16

NN = (((1,), (0,)), ((), ()))
NT = (((1,), (1,)), ((), ()))
TN = (((0,), (0,)), ((), ()))


def _dot(a, b, dn=NN):
    return lax.dot_general(a, b, dn, preferred_element_type=F32)


def _split2(x):
    hi = x.astype(BF16)
    lo = (x - hi.astype(F32)).astype(BF16)
    return hi, lo


def _split3(x):
    hi = x.astype(BF16)
    r1 = x - hi.astype(F32)
    mid = r1.astype(BF16)
    lo = (r1 - mid.astype(F32)).astype(BF16)
    return hi, mid, lo


def _dot3(xs, ys, dn=NN):
    xh, xl = xs
    yh, yl = ys
    return _dot(xh, yh, dn) + (_dot(xh, yl, dn) + _dot(xl, yh, dn))


def _cat3(x, axis, order):
    hi, lo = _split2(x)
    return jnp.concatenate([hi if ch == "h" else lo for ch in order], axis=axis)


def _seg_sum(x, ones_bd):
    hi, lo = _split2(x)
    return _dot(hi, ones_bd) + _dot(lo, ones_bd)


def _rms(x, g):
    return x * lax.rsqrt(jnp.mean(x * x, axis=-1, keepdims=True) + RMS_EPS) * g


def _const_spec(shape):
    return pl.BlockSpec(shape, lambda *_: (0,) * len(shape))


def _params(*sem):
    return pltpu.CompilerParams(dimension_semantics=sem, vmem_limit_bytes=VMEM_LIMIT_BYTES)


def _rms_rows_kernel(x_ref, g_ref, o_ref):
    o_ref[...] = _rms(x_ref[...], g_ref[...])


def _rms_rows(x, g):
    n, d = x.shape
    return pl.pallas_call(
        _rms_rows_kernel,
        out_shape=jax.ShapeDtypeStruct((n, d), F32),
        name="rms_rows",
    )(x, g.reshape(1, d))


def _inproj_kernel(x_ref, g_ref, w_ref, q_ref, k_ref, v_ref, f_ref, *, normalize):
    x = x_ref[...]
    h = _rms(x, g_ref[...]) if normalize else x
    proj = _dot(h.astype(BF16), w_ref[...])
    q_ref[...] = proj[:, :ATT_WIDTH].astype(q_ref.dtype)
    k_ref[...] = proj[:, ATT_WIDTH:ATT_WIDTH + KV_WIDTH]
    v_ref[...] = proj[:, ATT_WIDTH + KV_WIDTH:ATT_PROJ]
    f_ref[...] = proj[:, ATT_PROJ:]


def _inproj(x, g_mix, w_in_bf, *, normalize, tm, q_dtype=BF16):
    n = x.shape[0]
    row = lambda w: pl.BlockSpec((tm, w), lambda i: (i, 0))
    return pl.pallas_call(
        functools.partial(_inproj_kernel, normalize=normalize),
        grid=(n // tm,),
        in_specs=[row(D_MODEL), _const_spec((1, D_MODEL)), _const_spec((D_MODEL, PROJ_WIDTH))],
        out_specs=[row(ATT_WIDTH), row(KV_WIDTH), row(KV_WIDTH), row(RWKV_BLOCK)],
        out_shape=[
            jax.ShapeDtypeStruct((n, ATT_WIDTH), q_dtype),
            jax.ShapeDtypeStruct((n, KV_WIDTH), F32),
            jax.ShapeDtypeStruct((n, KV_WIDTH), F32),
            jax.ShapeDtypeStruct((n, RWKV_BLOCK), F32),
        ],
        compiler_params=_params("arbitrary"),
        name="inproj",
    )(x, g_mix.reshape(1, D_MODEL), w_in_bf)


def _softplus(z):
    return jnp.maximum(z, 0.0) + jnp.log(1.0 + jnp.exp(-jnp.abs(z)))


def _sigmoid(z):
    return 1.0 / (1.0 + jnp.exp(-z))


def _prep_kernel(*refs, blocks_per_seq, has_prev):
    if has_prev:
        f_ref, p_ref = refs[:2]
        refs = refs[2:]
    else:
        f_ref = refs[0]
        refs = refs[1:]
    (mu_ref, w0_ref, w2_ref, a0_ref, a2_ref, g2_ref, kk_ref, ka_ref, rk_ref, ones_ref,
     r_out, l_out, k_out, v_out, kk_out, kb_out, g_out, bo_out) = refs[:18]
    f = f_ref[...]
    tm = f.shape[0]
    if has_prev:
        prev = p_ref[...]
    else:
        carry_ref = refs[18]
        j = pl.program_id(0)

        @pl.when(j % blocks_per_seq == 0)
        def _():
            carry_ref[...] = jnp.zeros_like(carry_ref)

        rolled = pltpu.roll(f, 1, 0)
        row = lax.broadcasted_iota(jnp.int32, f.shape, 0)
        prev = jnp.where(row == 0, carry_ref[0:1, :], rolled)
        carry_ref[0:1, :] = f[tm - 1:tm, :]
    xs = f + (prev - f) * mu_ref[...]
    w3 = 3 * RWKV_WIDTH
    r = xs[:, :RWKV_WIDTH]
    k = xs[:, RWKV_WIDTH:2 * RWKV_WIDTH]
    v = xs[:, 2 * RWKV_WIDTH:w3]
    wl = xs[:, w3:w3 + DECAY_LORA]
    al = xs[:, w3 + DECAY_LORA:w3 + DECAY_LORA + ICL_LORA]
    gl = xs[:, w3 + DECAY_LORA + ICL_LORA:]
    wr = -_softplus(-(w0_ref[...] + _dot(jnp.tanh(wl).astype(BF16), w2_ref[...]))) - 0.5
    a = _sigmoid(a0_ref[...] + _dot(al.astype(BF16), a2_ref[...]))
    g = _dot(_sigmoid(gl).astype(BF16), g2_ref[...])
    ones_bd = ones_ref[...]
    kk = k * kk_ref[...]
    kk = kk * lax.rsqrt(_seg_sum(kk * kk, ones_bd) + L2_EPS)
    k2 = k * (1.0 + (a - 1.0) * ka_ref[...])
    r_out[...] = r
    l_out[...] = -jnp.exp(wr)
    k_out[...] = k2
    v_out[...] = v
    kk_out[...] = kk
    kb_out[...] = kk * a
    g_out[...] = g
    bo_out[...] = _seg_sum(r * k2 * rk_ref[...], ones_bd) * v


def _prep(feat, prev, p, *, tm, blocks_per_seq):
    n = feat.shape[0]
    has_prev = prev is not None
    row = lambda w: pl.BlockSpec((tm, w), lambda i: (i, 0))
    w = RWKV_WIDTH
    ins = [feat] + ([prev] if has_prev else [])
    in_specs = [row(RWKV_BLOCK)] * len(ins)
    ins += [p["mu"], p["w0"], p["w2"], p["a0"], p["a2"], p["g2"], p["k_k"], p["k_a"], p["r_k"], p["ones_bd"]]
    in_specs += [_const_spec((1, RWKV_BLOCK)), _const_spec((1, w)), _const_spec((DECAY_LORA, w)),
                 _const_spec((1, w)), _const_spec((ICL_LORA, w)), _const_spec((GATE_LORA, w)),
                 _const_spec((1, w)), _const_spec((1, w)), _const_spec((1, w)), _const_spec((w, w))]
    return pl.pallas_call(
        functools.partial(_prep_kernel, blocks_per_seq=blocks_per_seq, has_prev=has_prev),
        grid=(n // tm,),
        in_specs=in_specs,
        out_specs=[row(w)] * 8,
        out_shape=[jax.ShapeDtypeStruct((n, w), F32)] * 8,
        scratch_shapes=[] if has_prev else [pltpu.VMEM((8, RWKV_BLOCK), F32)],
        compiler_params=_params("arbitrary"),
        name="rwkv_prep",
    )(*ins)


def _attend(q, kall, vall, mask, sink_ref, o_ref):
    nq = q.shape[0]
    lane = lax.broadcasted_iota(jnp.int32, (1, 2 * HEAD_DIM), 1)
    lo_m = lane < HEAD_DIM
    k_sw = pltpu.roll(kall, HEAD_DIM, 1)
    v_sw = pltpu.roll(vall, HEAD_DIM, 1)
    for kv in range(KV_HEADS):
        own = lo_m if kv == 0 else ~lo_m
        kdup = jnp.where(own, kall, k_sw).astype(BF16)
        v_own = jnp.where(own, vall, 0.0).astype(BF16)
        v_oth = jnp.where(own, 0.0, v_sw).astype(BF16)
        v_lo, v_hi = (v_own, v_oth) if kv == 0 else (v_oth, v_own)
        for pr in range(2):
            c0 = (kv * 2 + pr) * 2 * HEAD_DIM
            qp = q[:, c0:c0 + 2 * HEAD_DIM].astype(F32)
            qs = jnp.concatenate([jnp.where(lo_m, qp, 0.0), jnp.where(lo_m, 0.0, qp)], axis=0)
            s = _dot(qs.astype(BF16), kdup, NT) * (HEAD_DIM ** -0.5)
            s = jnp.where(jnp.concatenate([mask, mask], axis=0), s, NEG_INF)
            h0 = (kv * 2 + pr) * 2
            rowi = lax.broadcasted_iota(jnp.int32, (2 * nq, 1), 0)
            sink = jnp.where(rowi < nq, sink_ref[h0], sink_ref[h0 + 1])
            m = jnp.maximum(jnp.max(s, axis=-1, keepdims=True), sink)
            e = jnp.exp(s - m)
            den = jnp.sum(e, axis=-1, keepdims=True) + jnp.exp(sink - m)
            pr_f = e / den
            out = (_dot(pr_f[:nq].astype(BF16), v_lo)
                   + _dot(pr_f[nq:].astype(BF16), v_hi))
            o_ref[:, c0:c0 + 2 * HEAD_DIM] = out.astype(o_ref.dtype)


def _band_mask(nq, q0):
    qi = lax.broadcasted_iota(jnp.int32, (nq, 2 * WINDOW), 0) + q0
    kj = lax.broadcasted_iota(jnp.int32, (nq, 2 * WINDOW), 1)
    return (kj <= qi) & (qi - kj < WINDOW), kj


def _attn_prompt_kernel(sink_ref, q_ref, kp_ref, kc_ref, vp_ref, vc_ref, o_ref):
    j = pl.program_id(1)
    band, kj = _band_mask(WINDOW, WINDOW)
    mask = band & ((j > 0) | (kj >= WINDOW))
    kall = jnp.concatenate([kp_ref[0], kc_ref[0]], axis=0)
    vall = jnp.concatenate([vp_ref[0], vc_ref[0]], axis=0)
    _attend(q_ref[0], kall, vall, mask, sink_ref, o_ref.at[0])


def _attn_prompt(q, k, v, sinks):
    b, t, _ = q.shape
    nb = t // WINDOW
    cur = lambda w: pl.BlockSpec((1, WINDOW, w), lambda bi, j: (bi, j, 0))
    prv = lambda w: pl.BlockSpec((1, WINDOW, w), lambda bi, j: (bi, jnp.maximum(j - 1, 0), 0))
    return pl.pallas_call(
        _attn_prompt_kernel,
        grid=(b, nb),
        in_specs=[pl.BlockSpec(memory_space=pltpu.SMEM), cur(ATT_WIDTH), prv(KV_WIDTH), cur(KV_WIDTH),
                  prv(KV_WIDTH), cur(KV_WIDTH)],
        out_specs=cur(ATT_WIDTH),
        out_shape=jax.ShapeDtypeStruct((b, t, ATT_WIDTH), BF16),
        compiler_params=_params("arbitrary", "arbitrary"),
        name="attn_prompt",
    )(sinks, q, k, k, v, v)


def _attn_sample_kernel(sink_ref, q_ref, kn_ref, vn_ref, ck_ref, cv_ref, o_ref, kw_ref, vw_ref,
                        kall_ref, vall_ref, *, seqs, t):
    wb = ck_ref.shape[1]

    @pl.when(pl.program_id(0) == 0)
    def _():
        kall_ref[...] = jnp.zeros_like(kall_ref)
        vall_ref[...] = jnp.zeros_like(vall_ref)

    mask, _ = _band_mask(t, wb)
    for s in range(seqs):
        rows = slice(s * t, (s + 1) * t)
        kall_ref[0:wb, :] = ck_ref[s]
        vall_ref[0:wb, :] = cv_ref[s]
        kall_ref[wb:wb + t, :] = kn_ref[rows, :]
        vall_ref[wb:wb + t, :] = vn_ref[rows, :]
        _attend(q_ref[rows, :], kall_ref[...], vall_ref[...], mask, sink_ref, o_ref.at[rows, :])
        kw_ref[s] = kall_ref[t:t + wb, :]
        vw_ref[s] = vall_ref[t:t + wb, :]


def _attn_sample(q, k, v, cache_k, cache_v, sinks, *, t, seqs):
    n = q.shape[0]
    b, wb, _ = cache_k.shape
    row = lambda w: pl.BlockSpec((seqs * t, w), lambda i: (i, 0))
    win = pl.BlockSpec((seqs, wb, KV_WIDTH), lambda i: (i, 0, 0))
    return pl.pallas_call(
        functools.partial(_attn_sample_kernel, seqs=seqs, t=t),
        grid=(b // seqs,),
        in_specs=[pl.BlockSpec(memory_space=pltpu.SMEM), row(ATT_WIDTH), row(KV_WIDTH), row(KV_WIDTH), win, win],
        out_specs=[row(ATT_WIDTH), win, win],
        out_shape=[jax.ShapeDtypeStruct((n, ATT_WIDTH), F32),
                   jax.ShapeDtypeStruct((b, wb, KV_WIDTH), F32),
                   jax.ShapeDtypeStruct((b, wb, KV_WIDTH), F32)],
        scratch_shapes=[pltpu.VMEM((2 * WINDOW, KV_WIDTH), F32), pltpu.VMEM((2 * WINDOW, KV_WIDTH), F32)],
        compiler_params=_params("arbitrary"),
        name="attn_sample",
    )(sinks, q, k, v, cache_k, cache_v)


def _wkv_kernel(r_ref, l_ref, k_ref, v_ref, kk_ref, kb_ref, s0_ref, o_ref, st_ref, z_ref, *, nseq, nchunks, rows):
    c = rows
    pw = 2 * HEAD_DIM
    j = pl.program_id(1)
    npairs = RWKV_HEADS // 2
    pairs = range(npairs)

    @pl.when(j == 0)
    def _():
        for s in range(nseq):
            for p in pairs:
                z_ref[s * npairs + p] = jnp.concatenate([s0_ref[s, 2 * p], s0_ref[s, 2 * p + 1]], axis=0).T

    def pair_iota(nrow):
        return (lax.broadcasted_iota(jnp.int32, (nrow, pw), 0),
                lax.broadcasted_iota(jnp.int32, (nrow, pw), 1) & (HEAD_DIM - 1))

    row, col = pair_iota(c)
    lo = lax.broadcasted_iota(jnp.int32, (1, pw), 1) < HEAD_DIM
    lower2 = jnp.concatenate([row > col, row >= col], axis=0)
    eye = (row == col).astype(F32)
    krow, kcol = pair_iota(HEAD_DIM)
    eye_ch = (krow == kcol).astype(F32)
    ri = lax.broadcasted_iota(jnp.int32, (c, c), 0)
    tri = (ri >= lax.broadcasted_iota(jnp.int32, (c, c), 1)).astype(BF16)

    def bd(y):
        zero = jnp.zeros_like(y)
        parts = [jnp.where(lo, y, zero), jnp.where(lo, zero, y)]
        if y.shape[0] < HEAD_DIM:
            fill = jnp.zeros((HEAD_DIM - y.shape[0], pw), y.dtype)
            parts = [parts[0], fill, parts[1], fill]
        return jnp.concatenate(parts, axis=0)

    def lhs3(x, axis=1):
        hi, low = _split2(x)
        return jnp.concatenate([hi, low, hi], axis=axis)

    def rhs3(y, axis=0):
        hi, low = _split2(y)
        bh = bd(hi)
        return jnp.concatenate([bh, bh, bd(low)], axis=axis)

    def tn3(x, y):
        yh, yl = _split2(y)
        full = _dot(lhs3(x, 0), jnp.concatenate([yh, yh, yl], axis=0), TN)
        return jnp.where(lo, full[:HEAD_DIM], full[HEAD_DIM:])

    prep = {}
    for s in range(nseq):
        for q in range(nchunks):
            tr = slice(q * c, (q + 1) * c)
            lw = l_ref[s, tr, :]
            cum = sum(_dot(tri, part) for part in _split3(lw))
            tot = cum[c - 1:c, :]
            en = jnp.exp(-cum)
            ed = jnp.exp(tot - cum)
            kk = kk_ref[s, tr, :]
            kb = kb_ref[s, tr, :]
            kx = k_ref[s, tr, :]
            prep[s, q] = dict(a=-kk * jnp.exp(cum - lw), r=r_ref[s, tr, :] * jnp.exp(cum), b=kb * en, k=kx * en,
                              bh=kb * ed, kh=kx * ed, v=v_ref[s, tr, :], etot=jnp.exp(tot))

    items = [(s, q, p) for s in range(nseq) for q in range(nchunks) for p in pairs]
    ps = lambda name, it: prep[it[0], it[1]][name][:, it[2] * pw:(it[2] + 1) * pw]
    each = lambda f: [f(i) for i in range(len(items))]
    ar_l = each(lambda i: lhs3(jnp.concatenate([ps("a", items[i]), ps("r", items[i])], axis=0)))
    v_r = each(lambda i: rhs3(ps("v", items[i])))
    g_b = each(lambda i: jnp.where(lower2, _dot(ar_l[i], rhs3(ps("b", items[i]), 1), NT), 0.0))
    g_k = each(lambda i: jnp.where(lower2, _dot(ar_l[i], rhs3(ps("k", items[i]), 1), NT), 0.0))
    p_m = each(lambda i: g_b[i][:c])
    t_m = each(lambda i: eye + p_m[i])
    p_l = each(lambda i: lhs3(p_m[i]))
    n = 2
    while n < c:
        p_m = each(lambda i: _dot(p_l[i], rhs3(p_m[i])))
        p_l = each(lambda i: lhs3(p_m[i]))
        t_m = each(lambda i: t_m[i] + _dot(p_l[i], rhs3(t_m[i])))
        n *= 2
    t_l = each(lambda i: lhs3(t_m[i]))
    av = each(lambda i: _dot(lhs3(g_k[i][:c]), v_r[i]))
    w_m = each(lambda i: _dot(t_l[i], rhs3(ps("a", items[i]))))
    u_m = each(lambda i: _dot(t_l[i], rhs3(av[i])))
    rb_l = each(lambda i: lhs3(g_b[i][c:]))
    r_p = each(lambda i: ps("r", items[i]) + _dot(rb_l[i], rhs3(w_m[i])))
    o_i = each(lambda i: _dot(rb_l[i], rhs3(u_m[i])) + _dot(lhs3(g_k[i][c:]), v_r[i]))
    m_m = each(lambda i: tn3(ps("bh", items[i]), w_m[i]) + eye_ch * ps("etot", items[i]))
    n_m = each(lambda i: tn3(jnp.concatenate([ps("bh", items[i]), ps("kh", items[i])], axis=0),
                             jnp.concatenate([u_m[i], ps("v", items[i])], axis=0)))
    rm_l = each(lambda i: lhs3(jnp.concatenate([r_p[i], m_m[i]], axis=0)))
    z = [z_ref[sp] for sp in range(nseq * npairs)]
    for q in range(nchunks):
        for s in range(nseq):
            for p in pairs:
                i = items.index((s, q, p))
                sp = s * npairs + p
                oz = _dot(rm_l[i], rhs3(z[sp]))
                o_ref[s, q * c:(q + 1) * c, p * pw:(p + 1) * pw] = oz[:c] + o_i[i]
                z[sp] = oz[c:] + n_m[i]
    for sp in range(nseq * npairs):
        z_ref[sp] = z[sp]

    @pl.when(j == pl.num_programs(1) - 1)
    def _():
        for s in range(nseq):
            for p in pairs:
                zt = z_ref[s * npairs + p].T
                st_ref[s, 2 * p] = zt[:HEAD_DIM]
                st_ref[s, 2 * p + 1] = zt[HEAD_DIM:]


def _wkv(r, lw, k, v, kk, kb, s0, *, nseq, nchunks, rows):
    b, t, w = r.shape
    blk = nchunks * rows
    tok = pl.BlockSpec((nseq, blk, w), lambda bi, j: (bi, j, 0))
    st = pl.BlockSpec((nseq, RWKV_HEADS, HEAD_DIM, HEAD_DIM), lambda bi, j: (bi, 0, 0, 0))
    return pl.pallas_call(
        functools.partial(_wkv_kernel, nseq=nseq, nchunks=nchunks, rows=rows),
        grid=(b // nseq, t // blk),
        in_specs=[tok] * 6 + [st],
        out_specs=[tok, st],
        out_shape=[jax.ShapeDtypeStruct((b, t, w), F32),
                   jax.ShapeDtypeStruct((b, RWKV_HEADS, HEAD_DIM, HEAD_DIM), F32)],
        scratch_shapes=[pltpu.VMEM((nseq * RWKV_HEADS // 2, HEAD_DIM, 2 * HEAD_DIM), F32)],
        compiler_params=_params("arbitrary", "arbitrary"),
        name="wkv_scan",
    )(r, lw, k, v, kk, kb, s0)


def _post_kernel(o_ref, bo_ref, g_ref, att_ref, x_ref, gng_ref, gnb_ref, ones_ref, wo_ref, gf_ref,
                 wg_ref, wu_ref, wd_ref, gfin_ref, y_ref, *, final):
    ones_bd = ones_ref[...]
    o = o_ref[...]
    mean = _seg_sum(o, ones_bd) * (1.0 / HEAD_DIM)
    oc = o - mean
    var = _seg_sum(oc * oc, ones_bd) * (1.0 / HEAD_DIM)
    on = oc * lax.rsqrt(var + GN_EPS) * gng_ref[...] + gnb_ref[...]
    rw = (on + bo_ref[...]) * g_ref[...]
    mix = _dot(att_ref[...].astype(BF16), wo_ref[:ATT_WIDTH, :]) + _dot(rw.astype(BF16), wo_ref[ATT_WIDTH:, :])
    x1 = x_ref[...] + mix
    u = _rms(x1, gf_ref[...]).astype(BF16)
    gate = _dot(u, wg_ref[...])
    up = _dot(u, wu_ref[...])
    hid = (gate * _sigmoid(gate) * up).astype(BF16)
    x2 = x1 + _dot(hid, wd_ref[...])
    y_ref[...] = _rms(x2, gfin_ref[...]) if final else x2


def _post(o, bonus, g, att, x, p, *, tm, final):
    n = x.shape[0]
    d_ff = p["w_gate"].shape[1]
    row = lambda w: pl.BlockSpec((tm, w), lambda i: (i, 0))
    once = lambda shape: pl.BlockSpec(shape, lambda i: (0,) * len(shape), pipeline_mode=pl.Buffered(1))
    w = RWKV_WIDTH
    return pl.pallas_call(
        functools.partial(_post_kernel, final=final),
        grid=(n // tm,),
        in_specs=[row(w), row(w), row(w), row(ATT_WIDTH), row(D_MODEL),
                  once((1, w)), once((1, w)), once((w, w)), once((ATT_WIDTH + w, D_MODEL)), once((1, D_MODEL)),
                  once((D_MODEL, d_ff)), once((D_MODEL, d_ff)), once((d_ff, D_MODEL)), once((1, D_MODEL))],
        out_specs=row(D_MODEL),
        out_shape=jax.ShapeDtypeStruct((n, D_MODEL), F32),
        compiler_params=_params("arbitrary"),
        name="post_ffn",
    )(o, bonus, g, att, x, p["gn_g"], p["gn_b"], p["ones_bd"], p["w_out"], p["g_ffn"],
      p["w_gate"], p["w_up"], p["w_down"], p["g_final"])


def _layer_params(l, g_mix, w_in, attn_sinks, rwkv_mu, w0, w2, a0, a2, g2, k_k, k_a, r_k, gn_g, gn_b,
                  w_out, g_ffn, w_gate, w_up, w_down, g_final):
    vec = lambda a: a.reshape(1, -1).astype(F32)
    hd = jnp.arange(RWKV_WIDTH) // HEAD_DIM
    return dict(
        g_mix=g_mix[l], w_in=w_in[l].astype(BF16), sinks=attn_sinks[l].astype(F32),
        mu=vec(rwkv_mu[l]), w0=vec(w0[l]), w2=w2[l].astype(BF16), a0=vec(a0[l]), a2=a2[l].astype(BF16),
        g2=g2[l].astype(BF16), k_k=vec(k_k[l]), k_a=vec(k_a[l]), r_k=vec(r_k[l]),
        gn_g=vec(gn_g[l]), gn_b=vec(gn_b[l]), w_out=w_out[l].astype(BF16), g_ffn=vec(g_ffn[l]),
        w_gate=w_gate[l].astype(BF16), w_up=w_up[l].astype(BF16), w_down=w_down[l].astype(BF16),
        g_final=vec(g_final), ones_bd=(hd[:, None] == hd[None, :]).astype(BF16),
    )


def _pick(n, pref):
    t = pref
    while n % t:
        t //= 2
    return t


def _prompt_layer(x, p, final):
    b, t, d = x.shape
    n = b * t
    x2 = x.reshape(n, d)
    tm = _pick(t, 512)
    q, k, v, feat = _inproj(x2, p["g_mix"], p["w_in"], normalize=True, tm=tm)
    r, lw, k2, vv, kk, kb, g, bonus = _prep(feat, None, p, tm=tm, blocks_per_seq=t // tm)
    att = _attn_prompt(q.reshape(b, t, -1), k.reshape(b, t, -1), v.reshape(b, t, -1), p["sinks"])
    s0 = jnp.zeros((b, RWKV_HEADS, HEAD_DIM, HEAD_DIM), F32)
    sh = lambda a: a.reshape(b, t, RWKV_WIDTH)
    o, s_new = _wkv(sh(r), sh(lw), sh(k2), sh(vv), sh(kk), sh(kb), s0, nseq=_pick(b, WKV_PROMPT_SEQS),
                    nchunks=_pick(t // HEAD_DIM, WKV_PROMPT_CHUNKS), rows=HEAD_DIM)
    y = _post(o.reshape(n, -1), bonus, g, att.reshape(n, -1), x2, p, tm=_pick(n, 256), final=final)
    wp = min(WINDOW, t)
    k_win = k.reshape(b, t, KV_HEADS, HEAD_DIM)[:, t - wp:]
    v_win = v.reshape(b, t, KV_HEADS, HEAD_DIM)[:, t - wp:]
    shift = _rms_rows(x[:, -1], p["g_mix"])
    return y.reshape(b, t, d), k_win, v_win, s_new, shift


def _sample_layer(x, h_prev, k_buf, v_buf, s0, p, final):
    b, t, d = x.shape
    n = b * t
    x2 = x.reshape(n, d)
    wb = k_buf.shape[1]
    q, k, v, feat = _inproj(x2, p["g_mix"], p["w_in"], normalize=True, tm=_pick(n, 512), q_dtype=F32)
    _, _, _, feat_prev = _inproj(h_prev, p["g_mix"], p["w_in"], normalize=False, tm=_pick(b, 512))
    f3 = feat.reshape(b, t, -1)
    prev = jnp.concatenate([feat_prev[:, None], f3[:, :-1]], axis=1).reshape(n, -1)
    r, lw, k2, vv, kk, kb, g, bonus = _prep(feat, prev, p, tm=_pick(n, 512), blocks_per_seq=1)
    att, k_win, v_win = _attn_sample(q, k, v, k_buf.reshape(b, wb, -1), v_buf.reshape(b, wb, -1),
                                     p["sinks"], t=t, seqs=_pick(b, 8))
    tp = -(-t // BF16_ROWS) * BF16_ROWS
    assert tp <= HEAD_DIM
    pad = lambda a: jnp.pad(a.reshape(b, t, RWKV_WIDTH), ((0, 0), (0, tp - t), (0, 0)))
    o, s_new = _wkv(pad(r), pad(lw), pad(k2), pad(vv), pad(kk), pad(kb), s0, nseq=_pick(b, WKV_SAMPLE_SEQS),
                    nchunks=1, rows=tp)
    y = _post(o[:, :t].reshape(n, -1), bonus, g, att, x2, p, tm=_pick(n, 256), final=final)
    shift = _rms_rows(x[:, -1], p["g_mix"])
    return (y.reshape(b, t, d), k_win.reshape(b, wb, KV_HEADS, HEAD_DIM), v_win.reshape(b, wb, KV_HEADS, HEAD_DIM),
            s_new, shift)


def kernel(x_prompt, x_sample, cache_k, cache_v, state_wkv, state_shift, g_mix, w_in, attn_sinks, rwkv_mu, w0, w2,
           a0, a2, g2, k_k, k_a, r_k, gn_g, gn_b, w_out, g_ffn, w_gate, w_up, w_down, g_final):
    depth = w_in.shape[0]
    xp, xs = x_prompt, x_sample
    outs_p, outs_s = [], []
    for l in range(depth):
        p = _layer_params(l, g_mix, w_in, attn_sinks, rwkv_mu, w0, w2, a0, a2, g2, k_k, k_a, r_k, gn_g, gn_b,
                          w_out, g_ffn, w_gate, w_up, w_down, g_final)
        final = l == depth - 1
        xp, kp, vp, sp, hp = _prompt_layer(xp, p, final)
        xs, kn, vn, sn, hn = _sample_layer(xs, state_shift[l], cache_k[l], cache_v[l], state_wkv[l], p, final)
        outs_p.append((kp, vp, sp, hp))
        outs_s.append((kn, vn, sn, hn))
    stack = lambda outs, i: jnp.stack([o[i] for o in outs])
    return (xp, xs,
            stack(outs_p, 0), stack(outs_p, 1), stack(outs_p, 2), stack(outs_p, 3),
            stack(outs_s, 0), stack(outs_s, 1), stack(outs_s, 2), stack(outs_s, 3))
```

```python
import functools

import jax
import jax.numpy as jnp
from jax import lax
from jax.experimental import pallas as pl
from jax.experimental.pallas import tpu as pltpu

F32 = jnp.float32
BF16 = jnp.bfloat16

D_MODEL = 1024
HEAD_DIM = 64
ATT_HEADS = 8
KV_HEADS = 2
ATT_WIDTH = ATT_HEADS * HEAD_DIM
KV_WIDTH = KV_HEADS * HEAD_DIM
RWKV_HEADS = 8
RWKV_WIDTH = RWKV_HEADS * HEAD_DIM
WINDOW = 128
DECAY_LORA = 64
ICL_LORA = 64
GATE_LORA = 128
RWKV_BLOCK = 3 * RWKV_WIDTH + DECAY_LORA + ICL_LORA + GATE_LORA
ATT_PROJ = ATT_WIDTH + 2 * KV_WIDTH
PROJ_WIDTH = ATT_PROJ + RWKV_BLOCK
RMS_EPS = 1e-6
GN_EPS = 64e-5
L2_EPS = 1e-12
NEG_INF = -1e30

V7X_VMEM_BYTES = 64 * 1024 * 1024
VMEM_LIMIT_BYTES = V7X_VMEM_BYTES * 3 // 4

WKV_PROMPT_SEQS = 2
WKV_PROMPT_CHUNKS = 2
WKV_SAMPLE_SEQS = 8
BF16_ROWS = 16
ATTN_QUERY_BLOCKS = 4
POST_ROWS = 512
POST_SUB_ROWS = 256

NN = (((1,), (0,)), ((), ()))
NT = (((1,), (1,)), ((), ()))
TN = (((0,), (0,)), ((), ()))


def _dot(a, b, dn=NN):
    return lax.dot_general(a, b, dn, preferred_element_type=F32)


def _split2(x):
    hi = x.astype(BF16)
    lo = (x - hi.astype(F32)).astype(BF16)
    return hi, lo


def _split3(x):
    hi = x.astype(BF16)
    r1 = x - hi.astype(F32)
    mid = r1.astype(BF16)
    lo = (r1 - mid.astype(F32)).astype(BF16)
    return hi, mid, lo


def _dot3(xs, ys, dn=NN):
    xh, xl = xs
    yh, yl = ys
    return _dot(xh, yh, dn) + (_dot(xh, yl, dn) + _dot(xl, yh, dn))


def _cat3(x, axis, order):
    hi, lo = _split2(x)
    return jnp.concatenate([hi if ch == "h" else lo for ch in order], axis=axis)


def _seg_sum(x, ones_bd):
    hi, lo = _split2(x)
    return _dot(hi, ones_bd) + _dot(lo, ones_bd)


def _rms(x, g):
    return x * lax.rsqrt(jnp.mean(x * x, axis=-1, keepdims=True) + RMS_EPS) * g


def _const_spec(shape):
    return pl.BlockSpec(shape, lambda *_: (0,) * len(shape))


def _params(*sem):
    return pltpu.CompilerParams(dimension_semantics=sem, vmem_limit_bytes=VMEM_LIMIT_BYTES)


def _rms_rows_kernel(x_ref, g_ref, o_ref):
    o_ref[...] = _rms(x_ref[...], g_ref[...])


def _rms_rows(x, g):
    n, d = x.shape
    return pl.pallas_call(
        _rms_rows_kernel,
        out_shape=jax.ShapeDtypeStruct((n, d), F32),
        name="rms_rows",
    )(x, g.reshape(1, d))


def _inproj_kernel(x_ref, g_ref, w_ref, q_ref, k_ref, v_ref, f_ref, *, normalize):
    x = x_ref[...]
    h = _rms(x, g_ref[...]) if normalize else x
    proj = _dot(h.astype(BF16), w_ref[...])
    q_ref[...] = proj[:, :ATT_WIDTH].astype(q_ref.dtype)
    k_ref[...] = proj[:, ATT_WIDTH:ATT_WIDTH + KV_WIDTH]
    v_ref[...] = proj[:, ATT_WIDTH + KV_WIDTH:ATT_PROJ]
    f_ref[...] = proj[:, ATT_PROJ:]


def _inproj(x, g_mix, w_in_bf, *, normalize, tm, q_dtype=BF16):
    n = x.shape[0]
    row = lambda w: pl.BlockSpec((tm, w), lambda i: (i, 0))
    return pl.pallas_call(
        functools.partial(_inproj_kernel, normalize=normalize),
        grid=(n // tm,),
        in_specs=[row(D_MODEL), _const_spec((1, D_MODEL)), _const_spec((D_MODEL, PROJ_WIDTH))],
        out_specs=[row(ATT_WIDTH), row(KV_WIDTH), row(KV_WIDTH), row(RWKV_BLOCK)],
        out_shape=[
            jax.ShapeDtypeStruct((n, ATT_WIDTH), q_dtype),
            jax.ShapeDtypeStruct((n, KV_WIDTH), F32),
            jax.ShapeDtypeStruct((n, KV_WIDTH), F32),
            jax.ShapeDtypeStruct((n, RWKV_BLOCK), F32),
        ],
        compiler_params=_params("arbitrary"),
        name="inproj",
    )(x, g_mix.reshape(1, D_MODEL), w_in_bf)


def _softplus(z):
    return jnp.maximum(z, 0.0) + jnp.log(1.0 + jnp.exp(-jnp.abs(z)))


def _sigmoid(z):
    return 1.0 / (1.0 + jnp.exp(-z))


def _prep_kernel(*refs, blocks_per_seq, has_prev):
    if has_prev:
        f_ref, p_ref = refs[:2]
        refs = refs[2:]
    else:
        f_ref = refs[0]
        refs = refs[1:]
    (mu_ref, w0_ref, w2_ref, a0_ref, a2_ref, g2_ref, kk_ref, ka_ref, rk_ref, ones_ref,
     r_out, l_out, k_out, v_out, kk_out, kb_out, g_out, bo_out) = refs[:18]
    f = f_ref[...]
    tm = f.shape[0]
    if has_prev:
        prev = p_ref[...]
    else:
        carry_ref = refs[18]
        j = pl.program_id(0)

        @pl.when(j % blocks_per_seq == 0)
        def _():
            carry_ref[...] = jnp.zeros_like(carry_ref)

        rolled = pltpu.roll(f, 1, 0)
        row = lax.broadcasted_iota(jnp.int32, f.shape, 0)
        prev = jnp.where(row == 0, carry_ref[0:1, :], rolled)
        carry_ref[0:1, :] = f[tm - 1:tm, :]
    xs = f + (prev - f) * mu_ref[...]
    w3 = 3 * RWKV_WIDTH
    r = xs[:, :RWKV_WIDTH]
    k = xs[:, RWKV_WIDTH:2 * RWKV_WIDTH]
    v = xs[:, 2 * RWKV_WIDTH:w3]
    wl = xs[:, w3:w3 + DECAY_LORA]
    al = xs[:, w3 + DECAY_LORA:w3 + DECAY_LORA + ICL_LORA]
    gl = xs[:, w3 + DECAY_LORA + ICL_LORA:]
    wr = -_softplus(-(w0_ref[...] + _dot(jnp.tanh(wl).astype(BF16), w2_ref[...]))) - 0.5
    a = _sigmoid(a0_ref[...] + _dot(al.astype(BF16), a2_ref[...]))
    g = _dot(_sigmoid(gl).astype(BF16), g2_ref[...])
    ones_bd = ones_ref[...]
    kk = k * kk_ref[...]
    kk = kk * lax.rsqrt(_seg_sum(kk * kk, ones_bd) + L2_EPS)
    k2 = k * (1.0 + (a - 1.0) * ka_ref[...])
    r_out[...] = r
    l_out[...] = -jnp.exp(wr)
    k_out[...] = k2
    v_out[...] = v
    kk_out[...] = kk
    kb_out[...] = kk * a
    g_out[...] = g
    bo_out[...] = _seg_sum(r * k2 * rk_ref[...], ones_bd) * v


def _prep(feat, prev, p, *, tm, blocks_per_seq):
    n = feat.shape[0]
    has_prev = prev is not None
    row = lambda w: pl.BlockSpec((tm, w), lambda i: (i, 0))
    w = RWKV_WIDTH
    ins = [feat] + ([prev] if has_prev else [])
    in_specs = [row(RWKV_BLOCK)] * len(ins)
    ins += [p["mu"], p["w0"], p["w2"], p["a0"], p["a2"], p["g2"], p["k_k"], p["k_a"], p["r_k"], p["ones_bd"]]
    in_specs += [_const_spec((1, RWKV_BLOCK)), _const_spec((1, w)), _const_spec((DECAY_LORA, w)),
                 _const_spec((1, w)), _const_spec((ICL_LORA, w)), _const_spec((GATE_LORA, w)),
                 _const_spec((1, w)), _const_spec((1, w)), _const_spec((1, w)), _const_spec((w, w))]
    return pl.pallas_call(
        functools.partial(_prep_kernel, blocks_per_seq=blocks_per_seq, has_prev=has_prev),
        grid=(n // tm,),
        in_specs=in_specs,
        out_specs=[row(w)] * 8,
        out_shape=[jax.ShapeDtypeStruct((n, w), F32)] * 8,
        scratch_shapes=[] if has_prev else [pltpu.VMEM((8, RWKV_BLOCK), F32)],
        compiler_params=_params("arbitrary"),
        name="rwkv_prep",
    )(*ins)


def _kv_operands(kblk, vblk):
    lo_m = lax.broadcasted_iota(jnp.int32, (1, 2 * HEAD_DIM), 1) < HEAD_DIM
    k_sw = pltpu.roll(kblk, HEAD_DIM, 1)
    v_sw = pltpu.roll(vblk, HEAD_DIM, 1)
    ops = []
    for kv in range(KV_HEADS):
        own = lo_m if kv == 0 else ~lo_m
        kdup = jnp.where(own, kblk, k_sw).astype(BF16)
        v_own = jnp.where(own, vblk, 0.0).astype(BF16)
        v_oth = jnp.where(own, 0.0, v_sw).astype(BF16)
        ops.append((kdup,) + ((v_own, v_oth) if kv == 0 else (v_oth, v_own)))
    return ops


def _attend(q, kv_ops, mask, sink_ref, o_ref):
    nq = q.shape[0]
    lo_m = lax.broadcasted_iota(jnp.int32, (1, 2 * HEAD_DIM), 1) < HEAD_DIM
    mask2 = jnp.concatenate([mask, mask], axis=0)
    rowi = lax.broadcasted_iota(jnp.int32, (2 * nq, 1), 0)
    for kv in range(KV_HEADS):
        kdup, v_lo, v_hi = kv_ops[kv]
        for pr in range(2):
            c0 = (kv * 2 + pr) * 2 * HEAD_DIM
            qp = q[:, c0:c0 + 2 * HEAD_DIM].astype(F32) * (HEAD_DIM ** -0.5)
            qs = jnp.concatenate([jnp.where(lo_m, qp, 0.0), jnp.where(lo_m, 0.0, qp)], axis=0)
            s = jnp.where(mask2, _dot(qs.astype(BF16), kdup, NT), NEG_INF)
            h0 = (kv * 2 + pr) * 2
            sink = jnp.where(rowi < nq, sink_ref[h0], sink_ref[h0 + 1])
            m = jnp.maximum(jnp.max(s, axis=-1, keepdims=True), sink)
            e = jnp.exp(s - m)
            rden = 1.0 / (jnp.sum(e, axis=-1, keepdims=True) + jnp.exp(sink - m))
            out = _dot(e[:nq].astype(BF16), v_lo) + _dot(e[nq:].astype(BF16), v_hi)
            out = out * jnp.where(lo_m, rden[:nq], rden[nq:])
            o_ref[:, c0:c0 + 2 * HEAD_DIM] = out.astype(o_ref.dtype)


def _band_mask(nq, q0):
    qi = lax.broadcasted_iota(jnp.int32, (nq, 2 * WINDOW), 0) + q0
    kj = lax.broadcasted_iota(jnp.int32, (nq, 2 * WINDOW), 1)
    return (kj <= qi) & (qi - kj < WINDOW), kj


def _attn_prompt_kernel(sink_ref, q_ref, kp_ref, kc_ref, vp_ref, vc_ref, o_ref, *, qblocks):
    j = pl.program_id(1)
    band, kj = _band_mask(WINDOW, WINDOW)
    blk = lambda ref, i: ref[0, i * WINDOW:(i + 1) * WINDOW, :]
    ops = [_kv_operands(kp_ref[0], vp_ref[0])] + [_kv_operands(blk(kc_ref, i), blk(vc_ref, i))
                                                  for i in range(qblocks)]
    for i in range(qblocks):
        mask = band & ((j > 0) | (kj >= WINDOW)) if i == 0 else band
        kv_ops = [tuple(jnp.concatenate([ops[i][kv][n], ops[i + 1][kv][n]], axis=0) for n in range(3))
                  for kv in range(KV_HEADS)]
        rows = slice(i * WINDOW, (i + 1) * WINDOW)
        _attend(q_ref[0, rows, :], kv_ops, mask, sink_ref, o_ref.at[0, rows, :])


def _attn_prompt(q, k, v, sinks):
    b, t, _ = q.shape
    qblocks = _pick(t // WINDOW, ATTN_QUERY_BLOCKS)
    nb = t // (qblocks * WINDOW)
    cur = lambda w: pl.BlockSpec((1, qblocks * WINDOW, w), lambda bi, j: (bi, j, 0))
    prv = lambda w: pl.BlockSpec((1, WINDOW, w), lambda bi, j: (bi, jnp.maximum(j * qblocks - 1, 0), 0))
    return pl.pallas_call(
        functools.partial(_attn_prompt_kernel, qblocks=qblocks),
        grid=(b, nb),
        in_specs=[pl.BlockSpec(memory_space=pltpu.SMEM), cur(ATT_WIDTH), prv(KV_WIDTH), cur(KV_WIDTH),
                  prv(KV_WIDTH), cur(KV_WIDTH)],
        out_specs=cur(ATT_WIDTH),
        out_shape=jax.ShapeDtypeStruct((b, t, ATT_WIDTH), BF16),
        compiler_params=_params("arbitrary", "arbitrary"),
        name="attn_prompt",
    )(sinks, q, k, k, v, v)


def _attn_sample_kernel(sink_ref, q_ref, kn_ref, vn_ref, ck_ref, cv_ref, o_ref, kw_ref, vw_ref,
                        kall_ref, vall_ref, *, seqs, t):
    wb = ck_ref.shape[1]

    @pl.when(pl.program_id(0) == 0)
    def _():
        kall_ref[...] = jnp.zeros_like(kall_ref)
        vall_ref[...] = jnp.zeros_like(vall_ref)

    mask, _ = _band_mask(t, wb)
    for s in range(seqs):
        rows = slice(s * t, (s + 1) * t)
        kall_ref[0:wb, :] = ck_ref[s]
        vall_ref[0:wb, :] = cv_ref[s]
        kall_ref[wb:wb + t, :] = kn_ref[rows, :]
        vall_ref[wb:wb + t, :] = vn_ref[rows, :]
        _attend(q_ref[rows, :], _kv_operands(kall_ref[...], vall_ref[...]), mask, sink_ref, o_ref.at[rows, :])
        kw_ref[s] = kall_ref[t:t + wb, :]
        vw_ref[s] = vall_ref[t:t + wb, :]


def _attn_sample(q, k, v, cache_k, cache_v, sinks, *, t, seqs):
    n = q.shape[0]
    b, wb, _ = cache_k.shape
    row = lambda w: pl.BlockSpec((seqs * t, w), lambda i: (i, 0))
    win = pl.BlockSpec((seqs, wb, KV_WIDTH), lambda i: (i, 0, 0))
    return pl.pallas_call(
        functools.partial(_attn_sample_kernel, seqs=seqs, t=t),
        grid=(b // seqs,),
        in_specs=[pl.BlockSpec(memory_space=pltpu.SMEM), row(ATT_WIDTH), row(KV_WIDTH), row(KV_WIDTH), win, win],
        out_specs=[row(ATT_WIDTH), win, win],
        out_shape=[jax.ShapeDtypeStruct((n, ATT_WIDTH), F32),
                   jax.ShapeDtypeStruct((b, wb, KV_WIDTH), F32),
                   jax.ShapeDtypeStruct((b, wb, KV_WIDTH), F32)],
        scratch_shapes=[pltpu.VMEM((2 * WINDOW, KV_WIDTH), F32), pltpu.VMEM((2 * WINDOW, KV_WIDTH), F32)],
        compiler_params=_params("arbitrary"),
        name="attn_sample",
    )(sinks, q, k, v, cache_k, cache_v)


def _wkv_kernel(r_ref, l_ref, k_ref, v_ref, kk_ref, kb_ref, s0_ref, o_ref, st_ref, z_ref, *, nseq, nchunks, rows):
    c = rows
    pw = 2 * HEAD_DIM
    j = pl.program_id(1)
    npairs = RWKV_HEADS // 2
    pairs = range(npairs)

    @pl.when(j == 0)
    def _():
        for s in range(nseq):
            for p in pairs:
                z_ref[s * npairs + p] = jnp.concatenate([s0_ref[s, 2 * p], s0_ref[s, 2 * p + 1]], axis=0).T

    def pair_iota(nrow):
        return (lax.broadcasted_iota(jnp.int32, (nrow, pw), 0),
                lax.broadcasted_iota(jnp.int32, (nrow, pw), 1) & (HEAD_DIM - 1))

    row, col = pair_iota(c)
    lo = lax.broadcasted_iota(jnp.int32, (1, pw), 1) < HEAD_DIM
    lower2 = jnp.concatenate([row > col, row >= col], axis=0)
    eye = (row == col).astype(F32)
    krow, kcol = pair_iota(HEAD_DIM)
    eye_ch = (krow == kcol).astype(F32)
    ri = lax.broadcasted_iota(jnp.int32, (c, c), 0)
    tri = (ri >= lax.broadcasted_iota(jnp.int32, (c, c), 1)).astype(BF16)

    def bd(y):
        zero = jnp.zeros_like(y)
        parts = [jnp.where(lo, y, zero), jnp.where(lo, zero, y)]
        if y.shape[0] < HEAD_DIM:
            fill = jnp.zeros((HEAD_DIM - y.shape[0], pw), y.dtype)
            parts = [parts[0], fill, parts[1], fill]
        return jnp.concatenate(parts, axis=0)

    def lhs3(x, axis=1):
        hi, low = _split2(x)
        return jnp.concatenate([hi, low, hi], axis=axis)

    def rhs3(y, axis=0):
        hi, low = _split2(y)
        bh = bd(hi)
        return jnp.concatenate([bh, bh, bd(low)], axis=axis)

    def tn3(x, y):
        yh, yl = _split2(y)
        full = _dot(lhs3(x, 0), jnp.concatenate([yh, yh, yl], axis=0), TN)
        return jnp.where(lo, full[:HEAD_DIM], full[HEAD_DIM:])

    prep = {}
    for s in range(nseq):
        for q in range(nchunks):
            tr = slice(q * c, (q + 1) * c)
            lw = l_ref[s, tr, :]
            cum = sum(_dot(tri, part) for part in _split3(lw))
            tot = cum[c - 1:c, :]
            en = jnp.exp(-cum)
            ed = jnp.exp(tot - cum)
            kk = kk_ref[s, tr, :]
            kb = kb_ref[s, tr, :]
            kx = k_ref[s, tr, :]
            prep[s, q] = dict(a=-kk * jnp.exp(cum - lw), r=r_ref[s, tr, :] * jnp.exp(cum), b=kb * en, k=kx * en,
                              bh=kb * ed, kh=kx * ed, v=v_ref[s, tr, :], etot=jnp.exp(tot))

    items = [(s, q, p) for s in range(nseq) for q in range(nchunks) for p in pairs]
    ps = lambda name, it: prep[it[0], it[1]][name][:, it[2] * pw:(it[2] + 1) * pw]
    each = lambda f: [f(i) for i in range(len(items))]
    ar_l = each(lambda i: lhs3(jnp.concatenate([ps("a", items[i]), ps("r", items[i])], axis=0)))
    v_r = each(lambda i: rhs3(ps("v", items[i])))
    g_b = each(lambda i: jnp.where(lower2, _dot(ar_l[i], rhs3(ps("b", items[i]), 1), NT), 0.0))
    g_k = each(lambda i: jnp.where(lower2, _dot(ar_l[i], rhs3(ps("k", items[i]), 1), NT), 0.0))
    p_m = each(lambda i: g_b[i][:c])
    t_m = each(lambda i: eye + p_m[i])
    p_l = each(lambda i: lhs3(p_m[i]))
    n = 2
    while n < c:
        p_m = each(lambda i: _dot(p_l[i], rhs3(p_m[i])))
        p_l = each(lambda i: lhs3(p_m[i]))
        t_m = each(lambda i: t_m[i] + _dot(p_l[i], rhs3(t_m[i])))
        n *= 2
    t_l = each(lambda i: lhs3(t_m[i]))
    av = each(lambda i: _dot(lhs3(g_k[i][:c]), v_r[i]))
    w_m = each(lambda i: _dot(t_l[i], rhs3(ps("a", items[i]))))
    u_m = each(lambda i: _dot(t_l[i], rhs3(av[i])))
    rb_l = each(lambda i: lhs3(g_b[i][c:]))
    r_p = each(lambda i: ps("r", items[i]) + _dot(rb_l[i], rhs3(w_m[i])))
    o_i = each(lambda i: _dot(rb_l[i], rhs3(u_m[i])) + _dot(lhs3(g_k[i][c:]), v_r[i]))
    m_m = each(lambda i: tn3(ps("bh", items[i]), w_m[i]) + eye_ch * ps("etot", items[i]))
    n_m = each(lambda i: tn3(jnp.concatenate([ps("bh", items[i]), ps("kh", items[i])], axis=0),
                             jnp.concatenate([u_m[i], ps("v", items[i])], axis=0)))
    rm_l = each(lambda i: lhs3(jnp.concatenate([r_p[i], m_m[i]], axis=0)))
    z = [z_ref[sp] for sp in range(nseq * npairs)]
    for q in range(nchunks):
        for s in range(nseq):
            for p in pairs:
                i = items.index((s, q, p))
                sp = s * npairs + p
                oz = _dot(rm_l[i], rhs3(z[sp]))
                o_ref[s, q * c:(q + 1) * c, p * pw:(p + 1) * pw] = oz[:c] + o_i[i]
                z[sp] = oz[c:] + n_m[i]
    for sp in range(nseq * npairs):
        z_ref[sp] = z[sp]

    @pl.when(j == pl.num_programs(1) - 1)
    def _():
        for s in range(nseq):
            for p in pairs:
                zt = z_ref[s * npairs + p].T
                st_ref[s, 2 * p] = zt[:HEAD_DIM]
                st_ref[s, 2 * p + 1] = zt[HEAD_DIM:]


def _wkv(r, lw, k, v, kk, kb, s0, *, nseq, nchunks, rows):
    b, t, w = r.shape
    blk = nchunks * rows
    tok = pl.BlockSpec((nseq, blk, w), lambda bi, j: (bi, j, 0))
    st = pl.BlockSpec((nseq, RWKV_HEADS, HEAD_DIM, HEAD_DIM), lambda bi, j: (bi, 0, 0, 0))
    return pl.pallas_call(
        functools.partial(_wkv_kernel, nseq=nseq, nchunks=nchunks, rows=rows),
        grid=(b // nseq, t // blk),
        in_specs=[tok] * 6 + [st],
        out_specs=[tok, st],
        out_shape=[jax.ShapeDtypeStruct((b, t, w), F32),
                   jax.ShapeDtypeStruct((b, RWKV_HEADS, HEAD_DIM, HEAD_DIM), F32)],
        scratch_shapes=[pltpu.VMEM((nseq * RWKV_HEADS // 2, HEAD_DIM, 2 * HEAD_DIM), F32)],
        compiler_params=_params("arbitrary", "arbitrary"),
        name="wkv_scan",
    )(r, lw, k, v, kk, kb, s0)


def _post_kernel(o_ref, bo_ref, g_ref, att_ref, x_ref, gng_ref, gnb_ref, ones_ref, wo_ref, gf_ref,
                 wg_ref, wu_ref, wd_ref, gfin_ref, y_ref, *, final):
    tm = x_ref.shape[0]
    sub = min(tm, POST_SUB_ROWS)
    parts = [slice(i * sub, (i + 1) * sub) for i in range(tm // sub)]
    each = lambda f: [f(i) for i in range(len(parts))]
    ones_bd = ones_ref[...]
    o = each(lambda i: o_ref[parts[i], :])
    mean = each(lambda i: _seg_sum(o[i], ones_bd) * (1.0 / HEAD_DIM))
    oc = each(lambda i: o[i] - mean[i])
    var = each(lambda i: _seg_sum(oc[i] * oc[i], ones_bd) * (1.0 / HEAD_DIM))
    rw = each(lambda i: ((oc[i] * lax.rsqrt(var[i] + GN_EPS) * gng_ref[...] + gnb_ref[...] + bo_ref[parts[i], :])
                         * g_ref[parts[i], :]).astype(BF16))
    x1 = each(lambda i: x_ref[parts[i], :] + _dot(att_ref[parts[i], :].astype(BF16), wo_ref[:ATT_WIDTH, :])
              + _dot(rw[i], wo_ref[ATT_WIDTH:, :]))
    u = each(lambda i: _rms(x1[i], gf_ref[...]).astype(BF16))
    gate = each(lambda i: _dot(u[i], wg_ref[...]))
    up = each(lambda i: _dot(u[i], wu_ref[...]))
    hid = each(lambda i: (gate[i] * _sigmoid(gate[i]) * up[i]).astype(BF16))
    x2 = each(lambda i: x1[i] + _dot(hid[i], wd_ref[...]))
    for i in range(len(parts)):
        y_ref[parts[i], :] = _rms(x2[i], gfin_ref[...]) if final else x2[i]


def _post(o, bonus, g, att, x, p, *, tm, final):
    n = x.shape[0]
    d_ff = p["w_gate"].shape[1]
    row = lambda w: pl.BlockSpec((tm, w), lambda i: (i, 0))
    once = lambda shape: pl.BlockSpec(shape, lambda i: (0,) * len(shape), pipeline_mode=pl.Buffered(1))
    w = RWKV_WIDTH
    return pl.pallas_call(
        functools.partial(_post_kernel, final=final),
        grid=(n // tm,),
        in_specs=[row(w), row(w), row(w), row(ATT_WIDTH), row(D_MODEL),
                  once((1, w)), once((1, w)), once((w, w)), once((ATT_WIDTH + w, D_MODEL)), once((1, D_MODEL)),
                  once((D_MODEL, d_ff)), once((D_MODEL, d_ff)), once((d_ff, D_MODEL)), once((1, D_MODEL))],
        out_specs=row(D_MODEL),
        out_shape=jax.ShapeDtypeStruct((n, D_MODEL), F32),
        compiler_params=_params("arbitrary"),
        name="post_ffn",
    )(o, bonus, g, att, x, p["gn_g"], p["gn_b"], p["ones_bd"], p["w_out"], p["g_ffn"],
      p["w_gate"], p["w_up"], p["w_down"], p["g_final"])


def _layer_params(l, g_mix, w_in, attn_sinks, rwkv_mu, w0, w2, a0, a2, g2, k_k, k_a, r_k, gn_g, gn_b,
                  w_out, g_ffn, w_gate, w_up, w_down, g_final):
    vec = lambda a: a.reshape(1, -1).astype(F32)
    hd = jnp.arange(RWKV_WIDTH) // HEAD_DIM
    return dict(
        g_mix=g_mix[l], w_in=w_in[l].astype(BF16), sinks=attn_sinks[l].astype(F32),
        mu=vec(rwkv_mu[l]), w0=vec(w0[l]), w2=w2[l].astype(BF16), a0=vec(a0[l]), a2=a2[l].astype(BF16),
        g2=g2[l].astype(BF16), k_k=vec(k_k[l]), k_a=vec(k_a[l]), r_k=vec(r_k[l]),
        gn_g=vec(gn_g[l]), gn_b=vec(gn_b[l]), w_out=w_out[l].astype(BF16), g_ffn=vec(g_ffn[l]),
        w_gate=w_gate[l].astype(BF16), w_up=w_up[l].astype(BF16), w_down=w_down[l].astype(BF16),
        g_final=vec(g_final), ones_bd=(hd[:, None] == hd[None, :]).astype(BF16),
    )


def _pick(n, pref):
    t = pref
    while n % t:
        t //= 2
    return t


def _prompt_layer(x, p, final):
    b, t, d = x.shape
    n = b * t
    x2 = x.reshape(n, d)
    tm = _pick(t, 512)
    q, k, v, feat = _inproj(x2, p["g_mix"], p["w_in"], normalize=True, tm=tm)
    r, lw, k2, vv, kk, kb, g, bonus = _prep(feat, None, p, tm=tm, blocks_per_seq=t // tm)
    att = _attn_prompt(q.reshape(b, t, -1), k.reshape(b, t, -1), v.reshape(b, t, -1), p["sinks"])
    s0 = jnp.zeros((b, RWKV_HEADS, HEAD_DIM, HEAD_DIM), F32)
    sh = lambda a: a.reshape(b, t, RWKV_WIDTH)
    o, s_new = _wkv(sh(r), sh(lw), sh(k2), sh(vv), sh(kk), sh(kb), s0, nseq=_pick(b, WKV_PROMPT_SEQS),
                    nchunks=_pick(t // HEAD_DIM, WKV_PROMPT_CHUNKS), rows=HEAD_DIM)
    y = _post(o.reshape(n, -1), bonus, g, att.reshape(n, -1), x2, p, tm=_pick(n, POST_ROWS), final=final)
    wp = min(WINDOW, t)
    k_win = k.reshape(b, t, KV_WIDTH)[:, t - wp:].reshape(b, wp, KV_HEADS, HEAD_DIM)
    v_win = v.reshape(b, t, KV_WIDTH)[:, t - wp:].reshape(b, wp, KV_HEADS, HEAD_DIM)
    shift = _rms_rows(x[:, -1], p["g_mix"])
    return y.reshape(b, t, d), k_win, v_win, s_new, shift


def _sample_layer(x, h_prev, k_buf, v_buf, s0, p, final):
    b, t, d = x.shape
    n = b * t
    x2 = x.reshape(n, d)
    wb = k_buf.shape[1]
    q, k, v, feat = _inproj(x2, p["g_mix"], p["w_in"], normalize=True, tm=_pick(n, 512), q_dtype=F32)
    _, _, _, feat_prev = _inproj(h_prev, p["g_mix"], p["w_in"], normalize=False, tm=_pick(b, 512))
    f3 = feat.reshape(b, t, -1)
    prev = jnp.concatenate([feat_prev[:, None], f3[:, :-1]], axis=1).reshape(n, -1)
    r, lw, k2, vv, kk, kb, g, bonus = _prep(feat, prev, p, tm=_pick(n, 512), blocks_per_seq=1)
    att, k_win, v_win = _attn_sample(q, k, v, k_buf.reshape(b, wb, -1), v_buf.reshape(b, wb, -1),
                                     p["sinks"], t=t, seqs=_pick(b, 8))
    tp = -(-t // BF16_ROWS) * BF16_ROWS
    assert tp <= HEAD_DIM
    pad = lambda a: jnp.pad(a.reshape(b, t, RWKV_WIDTH), ((0, 0), (0, tp - t), (0, 0)))
    o, s_new = _wkv(pad(r), pad(lw), pad(k2), pad(vv), pad(kk), pad(kb), s0, nseq=_pick(b, WKV_SAMPLE_SEQS),
                    nchunks=1, rows=tp)
    y = _post(o[:, :t].reshape(n, -1), bonus, g, att, x2, p, tm=_pick(n, POST_ROWS), final=final)
    shift = _rms_rows(x[:, -1], p["g_mix"])
    return (y.reshape(b, t, d), k_win.reshape(b, wb, KV_HEADS, HEAD_DIM), v_win.reshape(b, wb, KV_HEADS, HEAD_DIM),
            s_new, shift)


def kernel(x_prompt, x_sample, cache_k, cache_v, state_wkv, state_shift, g_mix, w_in, attn_sinks, rwkv_mu, w0, w2,
           a0, a2, g2, k_k, k_a, r_k, gn_g, gn_b, w_out, g_ffn, w_gate, w_up, w_down, g_final):
    depth = w_in.shape[0]
    xp, xs = x_prompt, x_sample
    outs_p, outs_s = [], []
    for l in range(depth):
        p = _layer_params(l, g_mix, w_in, attn_sinks, rwkv_mu, w0, w2, a0, a2, g2, k_k, k_a, r_k, gn_g, gn_b,
                          w_out, g_ffn, w_gate, w_up, w_down, g_final)
        final = l == depth - 1
        xp, kp, vp, sp, hp = _prompt_layer(xp, p, final)
        xs, kn, vn, sn, hn = _sample_layer(xs, state_shift[l], cache_k[l], cache_v[l], state_wkv[l], p, final)
        outs_p.append((kp, vp, sp, hp))
        outs_s.append((kn, vn, sn, hn))
    stack = lambda outs, i: jnp.stack([o[i] for o in outs])
    return (xp, xs,
            stack(outs_p, 0), stack(outs_p, 1), stack(outs_p, 2), stack(outs_p, 3),
            stack(outs_s, 0), stack(outs_s, 1), stack(outs_s, 2), stack(outs_s, 3))
```

```python
import functools

import jax
import jax.numpy as jnp
from jax import lax
from jax.experimental import pallas as pl
from jax.experimental.pallas import tpu as pltpu

F32 = jnp.float32
BF16 = jnp.bfloat16

D_MODEL = 1024
HEAD_DIM = 64
ATT_HEADS = 8
KV_HEADS = 2
ATT_WIDTH = ATT_HEADS * HEAD_DIM
KV_WIDTH = KV_HEADS * HEAD_DIM
RWKV_HEADS = 8
RWKV_WIDTH = RWKV_HEADS * HEAD_DIM
WINDOW = 128
DECAY_LORA = 64
ICL_LORA = 64
GATE_LORA = 128
RWKV_BLOCK = 3 * RWKV_WIDTH + DECAY_LORA + ICL_LORA + GATE_LORA
ATT_PROJ = ATT_WIDTH + 2 * KV_WIDTH
PROJ_WIDTH = ATT_PROJ + RWKV_BLOCK
RMS_EPS = 1e-6
GN_EPS = 64e-5
L2_EPS = 1e-12
NEG_INF = -1e30

V7X_VMEM_BYTES = 64 * 1024 * 1024
VMEM_LIMIT_BYTES = V7X_VMEM_BYTES * 3 // 4

WKV_PROMPT_SEQS = 2
WKV_PROMPT_CHUNKS = 2
WKV_SAMPLE_SEQS = 8
WKV_PASSES = dict(gram=1, inverse=1, apply=1, state=3, carry=3)
BF16_ROWS = 16
ATTN_QUERY_BLOCKS = 4
POST_ROWS = 512
POST_SUB_ROWS = 256

NN = (((1,), (0,)), ((), ()))
NT = (((1,), (1,)), ((), ()))
TN = (((0,), (0,)), ((), ()))


def _dot(a, b, dn=NN):
    return lax.dot_general(a, b, dn, preferred_element_type=F32)


def _split2(x):
    hi = x.astype(BF16)
    lo = (x - hi.astype(F32)).astype(BF16)
    return hi, lo


def _split3(x):
    hi = x.astype(BF16)
    r1 = x - hi.astype(F32)
    mid = r1.astype(BF16)
    lo = (r1 - mid.astype(F32)).astype(BF16)
    return hi, mid, lo


def _dot3(xs, ys, dn=NN):
    xh, xl = xs
    yh, yl = ys
    return _dot(xh, yh, dn) + (_dot(xh, yl, dn) + _dot(xl, yh, dn))


def _cat3(x, axis, order):
    hi, lo = _split2(x)
    return jnp.concatenate([hi if ch == "h" else lo for ch in order], axis=axis)


def _seg_sum(x, ones_bd):
    hi, lo = _split2(x)
    return _dot(hi, ones_bd) + _dot(lo, ones_bd)


def _rms(x, g):
    return x * lax.rsqrt(jnp.mean(x * x, axis=-1, keepdims=True) + RMS_EPS) * g


def _const_spec(shape):
    return pl.BlockSpec(shape, lambda *_: (0,) * len(shape))


def _params(*sem):
    return pltpu.CompilerParams(dimension_semantics=sem, vmem_limit_bytes=VMEM_LIMIT_BYTES)


def _rms_rows_kernel(x_ref, g_ref, o_ref):
    o_ref[...] = _rms(x_ref[...], g_ref[...])


def _rms_rows(x, g):
    n, d = x.shape
    return pl.pallas_call(
        _rms_rows_kernel,
        out_shape=jax.ShapeDtypeStruct((n, d), F32),
        name="rms_rows",
    )(x, g.reshape(1, d))


def _inproj_kernel(x_ref, g_ref, w_ref, q_ref, k_ref, v_ref, f_ref, *, normalize):
    x = x_ref[...]
    h = _rms(x, g_ref[...]) if normalize else x
    proj = _dot(h.astype(BF16), w_ref[...])
    q_ref[...] = proj[:, :ATT_WIDTH].astype(q_ref.dtype)
    k_ref[...] = proj[:, ATT_WIDTH:ATT_WIDTH + KV_WIDTH]
    v_ref[...] = proj[:, ATT_WIDTH + KV_WIDTH:ATT_PROJ]
    f_ref[...] = proj[:, ATT_PROJ:]


def _inproj(x, g_mix, w_in_bf, *, normalize, tm, q_dtype=BF16):
    n = x.shape[0]
    row = lambda w: pl.BlockSpec((tm, w), lambda i: (i, 0))
    return pl.pallas_call(
        functools.partial(_inproj_kernel, normalize=normalize),
        grid=(n // tm,),
        in_specs=[row(D_MODEL), _const_spec((1, D_MODEL)), _const_spec((D_MODEL, PROJ_WIDTH))],
        out_specs=[row(ATT_WIDTH), row(KV_WIDTH), row(KV_WIDTH), row(RWKV_BLOCK)],
        out_shape=[
            jax.ShapeDtypeStruct((n, ATT_WIDTH), q_dtype),
            jax.ShapeDtypeStruct((n, KV_WIDTH), F32),
            jax.ShapeDtypeStruct((n, KV_WIDTH), F32),
            jax.ShapeDtypeStruct((n, RWKV_BLOCK), F32),
        ],
        compiler_params=_params("arbitrary"),
        name="inproj",
    )(x, g_mix.reshape(1, D_MODEL), w_in_bf)


def _softplus(z):
    return jnp.maximum(z, 0.0) + jnp.log(1.0 + jnp.exp(-jnp.abs(z)))


def _sigmoid(z):
    return 1.0 / (1.0 + jnp.exp(-z))


def _prep_kernel(*refs, blocks_per_seq, has_prev):
    if has_prev:
        f_ref, p_ref = refs[:2]
        refs = refs[2:]
    else:
        f_ref = refs[0]
        refs = refs[1:]
    (mu_ref, w0_ref, w2_ref, a0_ref, a2_ref, g2_ref, kk_ref, ka_ref, rk_ref, ones_ref,
     r_out, l_out, k_out, v_out, kk_out, kb_out, g_out, bo_out) = refs[:18]
    f = f_ref[...]
    tm = f.shape[0]
    if has_prev:
        prev = p_ref[...]
    else:
        carry_ref = refs[18]
        j = pl.program_id(0)

        @pl.when(j % blocks_per_seq == 0)
        def _():
            carry_ref[...] = jnp.zeros_like(carry_ref)

        rolled = pltpu.roll(f, 1, 0)
        row = lax.broadcasted_iota(jnp.int32, f.shape, 0)
        prev = jnp.where(row == 0, carry_ref[0:1, :], rolled)
        carry_ref[0:1, :] = f[tm - 1:tm, :]
    xs = f + (prev - f) * mu_ref[...]
    w3 = 3 * RWKV_WIDTH
    r = xs[:, :RWKV_WIDTH]
    k = xs[:, RWKV_WIDTH:2 * RWKV_WIDTH]
    v = xs[:, 2 * RWKV_WIDTH:w3]
    wl = xs[:, w3:w3 + DECAY_LORA]
    al = xs[:, w3 + DECAY_LORA:w3 + DECAY_LORA + ICL_LORA]
    gl = xs[:, w3 + DECAY_LORA + ICL_LORA:]
    wr = -_softplus(-(w0_ref[...] + _dot(jnp.tanh(wl).astype(BF16), w2_ref[...]))) - 0.5
    a = _sigmoid(a0_ref[...] + _dot(al.astype(BF16), a2_ref[...]))
    g = _dot(_sigmoid(gl).astype(BF16), g2_ref[...])
    ones_bd = ones_ref[...]
    kk = k * kk_ref[...]
    kk = kk * lax.rsqrt(_seg_sum(kk * kk, ones_bd) + L2_EPS)
    k2 = k * (1.0 + (a - 1.0) * ka_ref[...])
    r_out[...] = r
    l_out[...] = -jnp.exp(wr)
    k_out[...] = k2
    v_out[...] = v
    kk_out[...] = kk
    kb_out[...] = kk * a
    g_out[...] = g
    bo_out[...] = _seg_sum(r * k2 * rk_ref[...], ones_bd) * v


def _prep(feat, prev, p, *, tm, blocks_per_seq):
    n = feat.shape[0]
    has_prev = prev is not None
    row = lambda w: pl.BlockSpec((tm, w), lambda i: (i, 0))
    w = RWKV_WIDTH
    ins = [feat] + ([prev] if has_prev else [])
    in_specs = [row(RWKV_BLOCK)] * len(ins)
    ins += [p["mu"], p["w0"], p["w2"], p["a0"], p["a2"], p["g2"], p["k_k"], p["k_a"], p["r_k"], p["ones_bd"]]
    in_specs += [_const_spec((1, RWKV_BLOCK)), _const_spec((1, w)), _const_spec((DECAY_LORA, w)),
                 _const_spec((1, w)), _const_spec((ICL_LORA, w)), _const_spec((GATE_LORA, w)),
                 _const_spec((1, w)), _const_spec((1, w)), _const_spec((1, w)), _const_spec((w, w))]
    return pl.pallas_call(
        functools.partial(_prep_kernel, blocks_per_seq=blocks_per_seq, has_prev=has_prev),
        grid=(n // tm,),
        in_specs=in_specs,
        out_specs=[row(w)] * 8,
        out_shape=[jax.ShapeDtypeStruct((n, w), F32)] * 8,
        scratch_shapes=[] if has_prev else [pltpu.VMEM((8, RWKV_BLOCK), F32)],
        compiler_params=_params("arbitrary"),
        name="rwkv_prep",
    )(*ins)


def _kv_operands(kblk, vblk):
    lo_m = lax.broadcasted_iota(jnp.int32, (1, 2 * HEAD_DIM), 1) < HEAD_DIM
    k_sw = pltpu.roll(kblk, HEAD_DIM, 1)
    v_sw = pltpu.roll(vblk, HEAD_DIM, 1)
    ops = []
    for kv in range(KV_HEADS):
        own = lo_m if kv == 0 else ~lo_m
        kdup = jnp.where(own, kblk, k_sw).astype(BF16)
        v_own = jnp.where(own, vblk, 0.0).astype(BF16)
        v_oth = jnp.where(own, 0.0, v_sw).astype(BF16)
        ops.append((kdup,) + ((v_own, v_oth) if kv == 0 else (v_oth, v_own)))
    return ops


def _attend(q, kv_ops, mask, sink_ref, o_ref):
    nq = q.shape[0]
    lo_m = lax.broadcasted_iota(jnp.int32, (1, 2 * HEAD_DIM), 1) < HEAD_DIM
    mask2 = jnp.concatenate([mask, mask], axis=0)
    rowi = lax.broadcasted_iota(jnp.int32, (2 * nq, 1), 0)
    for kv in range(KV_HEADS):
        kdup, v_lo, v_hi = kv_ops[kv]
        for pr in range(2):
            c0 = (kv * 2 + pr) * 2 * HEAD_DIM
            qp = q[:, c0:c0 + 2 * HEAD_DIM].astype(F32) * (HEAD_DIM ** -0.5)
            qs = jnp.concatenate([jnp.where(lo_m, qp, 0.0), jnp.where(lo_m, 0.0, qp)], axis=0)
            s = jnp.where(mask2, _dot(qs.astype(BF16), kdup, NT), NEG_INF)
            h0 = (kv * 2 + pr) * 2
            sink = jnp.where(rowi < nq, sink_ref[h0], sink_ref[h0 + 1])
            m = jnp.maximum(jnp.max(s, axis=-1, keepdims=True), sink)
            e = jnp.exp(s - m)
            rden = 1.0 / (jnp.sum(e, axis=-1, keepdims=True) + jnp.exp(sink - m))
            out = _dot(e[:nq].astype(BF16), v_lo) + _dot(e[nq:].astype(BF16), v_hi)
            out = out * jnp.where(lo_m, rden[:nq], rden[nq:])
            o_ref[:, c0:c0 + 2 * HEAD_DIM] = out.astype(o_ref.dtype)


def _band_mask(nq, q0):
    qi = lax.broadcasted_iota(jnp.int32, (nq, 2 * WINDOW), 0) + q0
    kj = lax.broadcasted_iota(jnp.int32, (nq, 2 * WINDOW), 1)
    return (kj <= qi) & (qi - kj < WINDOW), kj


def _attn_prompt_kernel(sink_ref, q_ref, kp_ref, kc_ref, vp_ref, vc_ref, o_ref, *, qblocks):
    j = pl.program_id(1)
    band, kj = _band_mask(WINDOW, WINDOW)
    blk = lambda ref, i: ref[0, i * WINDOW:(i + 1) * WINDOW, :]
    ops = [_kv_operands(kp_ref[0], vp_ref[0])] + [_kv_operands(blk(kc_ref, i), blk(vc_ref, i))
                                                  for i in range(qblocks)]
    for i in range(qblocks):
        mask = band & ((j > 0) | (kj >= WINDOW)) if i == 0 else band
        kv_ops = [tuple(jnp.concatenate([ops[i][kv][n], ops[i + 1][kv][n]], axis=0) for n in range(3))
                  for kv in range(KV_HEADS)]
        rows = slice(i * WINDOW, (i + 1) * WINDOW)
        _attend(q_ref[0, rows, :], kv_ops, mask, sink_ref, o_ref.at[0, rows, :])


def _attn_prompt(q, k, v, sinks):
    b, t, _ = q.shape
    qblocks = _pick(t // WINDOW, ATTN_QUERY_BLOCKS)
    nb = t // (qblocks * WINDOW)
    cur = lambda w: pl.BlockSpec((1, qblocks * WINDOW, w), lambda bi, j: (bi, j, 0))
    prv = lambda w: pl.BlockSpec((1, WINDOW, w), lambda bi, j: (bi, jnp.maximum(j * qblocks - 1, 0), 0))
    return pl.pallas_call(
        functools.partial(_attn_prompt_kernel, qblocks=qblocks),
        grid=(b, nb),
        in_specs=[pl.BlockSpec(memory_space=pltpu.SMEM), cur(ATT_WIDTH), prv(KV_WIDTH), cur(KV_WIDTH),
                  prv(KV_WIDTH), cur(KV_WIDTH)],
        out_specs=cur(ATT_WIDTH),
        out_shape=jax.ShapeDtypeStruct((b, t, ATT_WIDTH), BF16),
        compiler_params=_params("arbitrary", "arbitrary"),
        name="attn_prompt",
    )(sinks, q, k, k, v, v)


def _attn_sample_kernel(sink_ref, q_ref, kn_ref, vn_ref, ck_ref, cv_ref, o_ref, kw_ref, vw_ref,
                        kall_ref, vall_ref, *, seqs, t):
    wb = ck_ref.shape[1]

    @pl.when(pl.program_id(0) == 0)
    def _():
        kall_ref[...] = jnp.zeros_like(kall_ref)
        vall_ref[...] = jnp.zeros_like(vall_ref)

    mask, _ = _band_mask(t, wb)
    for s in range(seqs):
        rows = slice(s * t, (s + 1) * t)
        kall_ref[0:wb, :] = ck_ref[s]
        vall_ref[0:wb, :] = cv_ref[s]
        kall_ref[wb:wb + t, :] = kn_ref[rows, :]
        vall_ref[wb:wb + t, :] = vn_ref[rows, :]
        _attend(q_ref[rows, :], _kv_operands(kall_ref[...], vall_ref[...]), mask, sink_ref, o_ref.at[rows, :])
        kw_ref[s] = kall_ref[t:t + wb, :]
        vw_ref[s] = vall_ref[t:t + wb, :]


def _attn_sample(q, k, v, cache_k, cache_v, sinks, *, t, seqs):
    n = q.shape[0]
    b, wb, _ = cache_k.shape
    row = lambda w: pl.BlockSpec((seqs * t, w), lambda i: (i, 0))
    win = pl.BlockSpec((seqs, wb, KV_WIDTH), lambda i: (i, 0, 0))
    return pl.pallas_call(
        functools.partial(_attn_sample_kernel, seqs=seqs, t=t),
        grid=(b // seqs,),
        in_specs=[pl.BlockSpec(memory_space=pltpu.SMEM), row(ATT_WIDTH), row(KV_WIDTH), row(KV_WIDTH), win, win],
        out_specs=[row(ATT_WIDTH), win, win],
        out_shape=[jax.ShapeDtypeStruct((n, ATT_WIDTH), F32),
                   jax.ShapeDtypeStruct((b, wb, KV_WIDTH), F32),
                   jax.ShapeDtypeStruct((b, wb, KV_WIDTH), F32)],
        scratch_shapes=[pltpu.VMEM((2 * WINDOW, KV_WIDTH), F32), pltpu.VMEM((2 * WINDOW, KV_WIDTH), F32)],
        compiler_params=_params("arbitrary"),
        name="attn_sample",
    )(sinks, q, k, v, cache_k, cache_v)


def _wkv_kernel(r_ref, l_ref, k_ref, v_ref, kk_ref, kb_ref, s0_ref, o_ref, st_ref, z_ref, *, nseq, nchunks, rows):
    c = rows
    pw = 2 * HEAD_DIM
    j = pl.program_id(1)
    npairs = RWKV_HEADS // 2
    pairs = range(npairs)

    @pl.when(j == 0)
    def _():
        for s in range(nseq):
            for p in pairs:
                z_ref[s * npairs + p] = jnp.concatenate([s0_ref[s, 2 * p], s0_ref[s, 2 * p + 1]], axis=0).T

    def pair_iota(nrow):
        return (lax.broadcasted_iota(jnp.int32, (nrow, pw), 0),
                lax.broadcasted_iota(jnp.int32, (nrow, pw), 1) & (HEAD_DIM - 1))

    row, col = pair_iota(c)
    lo = lax.broadcasted_iota(jnp.int32, (1, pw), 1) < HEAD_DIM
    lower2 = jnp.concatenate([row > col, row >= col], axis=0)
    eye = (row == col).astype(F32)
    krow, kcol = pair_iota(HEAD_DIM)
    eye_ch = (krow == kcol).astype(F32)
    ri = lax.broadcasted_iota(jnp.int32, (c, c), 0)
    tri = (ri >= lax.broadcasted_iota(jnp.int32, (c, c), 1)).astype(BF16)

    def bd(y):
        zero = jnp.zeros_like(y)
        parts = [jnp.where(lo, y, zero), jnp.where(lo, zero, y)]
        if y.shape[0] < HEAD_DIM:
            fill = jnp.zeros((HEAD_DIM - y.shape[0], pw), y.dtype)
            parts = [parts[0], fill, parts[1], fill]
        return jnp.concatenate(parts, axis=0)

    memo = {}

    def cached(x, tag, build):
        key = (id(x), tag)
        if key not in memo:
            memo[key] = (x, build())
        return memo[key][1]

    def split(x):
        return cached(x, "split", lambda: _split2(x))

    def lhs(x, passes, axis=1):
        hi, low = split(x)
        return cached(x, ("lhs", passes, axis), lambda: jnp.concatenate([hi, low, hi][:passes], axis=axis))

    def rhs(y, passes, axis=0):
        hi, low = split(y)

        def build():
            bh = bd(hi)
            return jnp.concatenate([bh, bh, bd(low)][:passes], axis=axis)
        return cached(y, ("rhs", passes, axis), build)

    def mm(x, y, group):
        return _dot(lhs(x, WKV_PASSES[group]), rhs(y, WKV_PASSES[group]))

    def mm_nt(x, y, group):
        return _dot(lhs(x, WKV_PASSES[group]), rhs(y, WKV_PASSES[group], 1), NT)

    def mm_tn(x, y, group):
        passes = WKV_PASSES[group]
        yh, yl = split(y)
        full = _dot(lhs(x, passes, 0), jnp.concatenate([yh, yh, yl][:passes], axis=0), TN)
        return jnp.where(lo, full[:HEAD_DIM], full[HEAD_DIM:])

    prep = {}
    for s in range(nseq):
        for q in range(nchunks):
            tr = slice(q * c, (q + 1) * c)
            lw = l_ref[s, tr, :]
            cum = sum(_dot(tri, part) for part in _split3(lw))
            tot = cum[c - 1:c, :]
            en = jnp.exp(-cum)
            ed = jnp.exp(tot - cum)
            kk = kk_ref[s, tr, :]
            kb = kb_ref[s, tr, :]
            kx = k_ref[s, tr, :]
            prep[s, q] = dict(a=-kk * jnp.exp(cum - lw), r=r_ref[s, tr, :] * jnp.exp(cum), b=kb * en, k=kx * en,
                              bh=kb * ed, kh=kx * ed, v=v_ref[s, tr, :], etot=jnp.exp(tot))

    items = [(s, q, p) for s in range(nseq) for q in range(nchunks) for p in pairs]
    ps = lambda name, it: prep[it[0], it[1]][name][:, it[2] * pw:(it[2] + 1) * pw]
    each = lambda f: [f(i) for i in range(len(items))]
    pv = {name: each(lambda i: ps(name, items[i])) for name in ("a", "r", "b", "k", "bh", "kh", "v", "etot")}
    ar = each(lambda i: jnp.concatenate([pv["a"][i], pv["r"][i]], axis=0))
    g_b = each(lambda i: jnp.where(lower2, mm_nt(ar[i], pv["b"][i], "gram"), 0.0))
    g_k = each(lambda i: jnp.where(lower2, mm_nt(ar[i], pv["k"][i], "gram"), 0.0))
    a_ak = each(lambda i: g_k[i][:c])
    a_rb = each(lambda i: g_b[i][c:])
    a_rk = each(lambda i: g_k[i][c:])
    p_m = each(lambda i: g_b[i][:c])
    t_m = each(lambda i: eye + p_m[i])
    n = 2
    while n < c:
        p_m = each(lambda i: mm(p_m[i], p_m[i], "inverse"))
        t_m = each(lambda i: t_m[i] + mm(p_m[i], t_m[i], "inverse"))
        n *= 2
    av = each(lambda i: mm(a_ak[i], pv["v"][i], "apply"))
    w_m = each(lambda i: mm(t_m[i], pv["a"][i], "apply"))
    u_m = each(lambda i: mm(t_m[i], av[i], "apply"))
    r_p = each(lambda i: pv["r"][i] + mm(a_rb[i], w_m[i], "apply"))
    o_i = each(lambda i: mm(a_rb[i], u_m[i], "apply") + mm(a_rk[i], pv["v"][i], "apply"))
    m_m = each(lambda i: mm_tn(pv["bh"][i], w_m[i], "state") + eye_ch * pv["etot"][i])
    bk = each(lambda i: jnp.concatenate([pv["bh"][i], pv["kh"][i]], axis=0))
    uv = each(lambda i: jnp.concatenate([u_m[i], pv["v"][i]], axis=0))
    n_m = each(lambda i: mm_tn(bk[i], uv[i], "state"))
    rm = each(lambda i: jnp.concatenate([r_p[i], m_m[i]], axis=0))
    z = [z_ref[sp] for sp in range(nseq * npairs)]
    for q in range(nchunks):
        for s in range(nseq):
            for p in pairs:
                i = items.index((s, q, p))
                sp = s * npairs + p
                oz = mm(rm[i], z[sp], "carry")
                o_ref[s, q * c:(q + 1) * c, p * pw:(p + 1) * pw] = oz[:c] + o_i[i]
                z[sp] = oz[c:] + n_m[i]
    for sp in range(nseq * npairs):
        z_ref[sp] = z[sp]

    @pl.when(j == pl.num_programs(1) - 1)
    def _():
        for s in range(nseq):
            for p in pairs:
                zt = z_ref[s * npairs + p].T
                st_ref[s, 2 * p] = zt[:HEAD_DIM]
                st_ref[s, 2 * p + 1] = zt[HEAD_DIM:]


def _wkv(r, lw, k, v, kk, kb, s0, *, nseq, nchunks, rows):
    b, t, w = r.shape
    blk = nchunks * rows
    tok = pl.BlockSpec((nseq, blk, w), lambda bi, j: (bi, j, 0))
    st = pl.BlockSpec((nseq, RWKV_HEADS, HEAD_DIM, HEAD_DIM), lambda bi, j: (bi, 0, 0, 0))
    return pl.pallas_call(
        functools.partial(_wkv_kernel, nseq=nseq, nchunks=nchunks, rows=rows),
        grid=(b // nseq, t // blk),
        in_specs=[tok] * 6 + [st],
        out_specs=[tok, st],
        out_shape=[jax.ShapeDtypeStruct((b, t, w), F32),
                   jax.ShapeDtypeStruct((b, RWKV_HEADS, HEAD_DIM, HEAD_DIM), F32)],
        scratch_shapes=[pltpu.VMEM((nseq * RWKV_HEADS // 2, HEAD_DIM, 2 * HEAD_DIM), F32)],
        compiler_params=_params("arbitrary", "arbitrary"),
        name="wkv_scan",
    )(r, lw, k, v, kk, kb, s0)


def _post_kernel(o_ref, bo_ref, g_ref, att_ref, x_ref, gng_ref, gnb_ref, ones_ref, wo_ref, gf_ref,
                 wg_ref, wu_ref, wd_ref, gfin_ref, y_ref, *, final):
    tm = x_ref.shape[0]
    sub = min(tm, POST_SUB_ROWS)
    parts = [slice(i * sub, (i + 1) * sub) for i in range(tm // sub)]
    each = lambda f: [f(i) for i in range(len(parts))]
    ones_bd = ones_ref[...]
    o = each(lambda i: o_ref[parts[i], :])
    mean = each(lambda i: _seg_sum(o[i], ones_bd) * (1.0 / HEAD_DIM))
    oc = each(lambda i: o[i] - mean[i])
    var = each(lambda i: _seg_sum(oc[i] * oc[i], ones_bd) * (1.0 / HEAD_DIM))
    rw = each(lambda i: ((oc[i] * lax.rsqrt(var[i] + GN_EPS) * gng_ref[...] + gnb_ref[...] + bo_ref[parts[i], :])
                         * g_ref[parts[i], :]).astype(BF16))
    x1 = each(lambda i: x_ref[parts[i], :] + _dot(att_ref[parts[i], :].astype(BF16), wo_ref[:ATT_WIDTH, :])
              + _dot(rw[i], wo_ref[ATT_WIDTH:, :]))
    u = each(lambda i: _rms(x1[i], gf_ref[...]).astype(BF16))
    gate = each(lambda i: _dot(u[i], wg_ref[...]))
    up = each(lambda i: _dot(u[i], wu_ref[...]))
    hid = each(lambda i: (gate[i] * _sigmoid(gate[i]) * up[i]).astype(BF16))
    x2 = each(lambda i: x1[i] + _dot(hid[i], wd_ref[...]))
    for i in range(len(parts)):
        y_ref[parts[i], :] = _rms(x2[i], gfin_ref[...]) if final else x2[i]


def _post(o, bonus, g, att, x, p, *, tm, final):
    n = x.shape[0]
    d_ff = p["w_gate"].shape[1]
    row = lambda w: pl.BlockSpec((tm, w), lambda i: (i, 0))
    once = lambda shape: pl.BlockSpec(shape, lambda i: (0,) * len(shape), pipeline_mode=pl.Buffered(1))
    w = RWKV_WIDTH
    return pl.pallas_call(
        functools.partial(_post_kernel, final=final),
        grid=(n // tm,),
        in_specs=[row(w), row(w), row(w), row(ATT_WIDTH), row(D_MODEL),
                  once((1, w)), once((1, w)), once((w, w)), once((ATT_WIDTH + w, D_MODEL)), once((1, D_MODEL)),
                  once((D_MODEL, d_ff)), once((D_MODEL, d_ff)), once((d_ff, D_MODEL)), once((1, D_MODEL))],
        out_specs=row(D_MODEL),
        out_shape=jax.ShapeDtypeStruct((n, D_MODEL), F32),
        compiler_params=_params("arbitrary"),
        name="post_ffn",
    )(o, bonus, g, att, x, p["gn_g"], p["gn_b"], p["ones_bd"], p["w_out"], p["g_ffn"],
      p["w_gate"], p["w_up"], p["w_down"], p["g_final"])


def _layer_params(l, g_mix, w_in, attn_sinks, rwkv_mu, w0, w2, a0, a2, g2, k_k, k_a, r_k, gn_g, gn_b,
                  w_out, g_ffn, w_gate, w_up, w_down, g_final):
    vec = lambda a: a.reshape(1, -1).astype(F32)
    hd = jnp.arange(RWKV_WIDTH) // HEAD_DIM
    return dict(
        g_mix=g_mix[l], w_in=w_in[l].astype(BF16), sinks=attn_sinks[l].astype(F32),
        mu=vec(rwkv_mu[l]), w0=vec(w0[l]), w2=w2[l].astype(BF16), a0=vec(a0[l]), a2=a2[l].astype(BF16),
        g2=g2[l].astype(BF16), k_k=vec(k_k[l]), k_a=vec(k_a[l]), r_k=vec(r_k[l]),
        gn_g=vec(gn_g[l]), gn_b=vec(gn_b[l]), w_out=w_out[l].astype(BF16), g_ffn=vec(g_ffn[l]),
        w_gate=w_gate[l].astype(BF16), w_up=w_up[l].astype(BF16), w_down=w_down[l].astype(BF16),
        g_final=vec(g_final), ones_bd=(hd[:, None] == hd[None, :]).astype(BF16),
    )


def _pick(n, pref):
    t = pref
    while n % t:
        t //= 2
    return t


def _prompt_layer(x, p, final):
    b, t, d = x.shape
    n = b * t
    x2 = x.reshape(n, d)
    tm = _pick(t, 512)
    q, k, v, feat = _inproj(x2, p["g_mix"], p["w_in"], normalize=True, tm=tm)
    r, lw, k2, vv, kk, kb, g, bonus = _prep(feat, None, p, tm=tm, blocks_per_seq=t // tm)
    att = _attn_prompt(q.reshape(b, t, -1), k.reshape(b, t, -1), v.reshape(b, t, -1), p["sinks"])
    s0 = jnp.zeros((b, RWKV_HEADS, HEAD_DIM, HEAD_DIM), F32)
    sh = lambda a: a.reshape(b, t, RWKV_WIDTH)
    o, s_new = _wkv(sh(r), sh(lw), sh(k2), sh(vv), sh(kk), sh(kb), s0, nseq=_pick(b, WKV_PROMPT_SEQS),
                    nchunks=_pick(t // HEAD_DIM, WKV_PROMPT_CHUNKS), rows=HEAD_DIM)
    y = _post(o.reshape(n, -1), bonus, g, att.reshape(n, -1), x2, p, tm=_pick(n, POST_ROWS), final=final)
    wp = min(WINDOW, t)
    k_win = k.reshape(b, t, KV_WIDTH)[:, t - wp:].reshape(b, wp, KV_HEADS, HEAD_DIM)
    v_win = v.reshape(b, t, KV_WIDTH)[:, t - wp:].reshape(b, wp, KV_HEADS, HEAD_DIM)
    shift = _rms_rows(x[:, -1], p["g_mix"])
    return y.reshape(b, t, d), k_win, v_win, s_new, shift


def _sample_layer(x, h_prev, k_buf, v_buf, s0, p, final):
    b, t, d = x.shape
    n = b * t
    x2 = x.reshape(n, d)
    wb = k_buf.shape[1]
    q, k, v, feat = _inproj(x2, p["g_mix"], p["w_in"], normalize=True, tm=_pick(n, 512), q_dtype=F32)
    _, _, _, feat_prev = _inproj(h_prev, p["g_mix"], p["w_in"], normalize=False, tm=_pick(b, 512))
    f3 = feat.reshape(b, t, -1)
    prev = jnp.concatenate([feat_prev[:, None], f3[:, :-1]], axis=1).reshape(n, -1)
    r, lw, k2, vv, kk, kb, g, bonus = _prep(feat, prev, p, tm=_pick(n, 512), blocks_per_seq=1)
    att, k_win, v_win = _attn_sample(q, k, v, k_buf.reshape(b, wb, -1), v_buf.reshape(b, wb, -1),
                                     p["sinks"], t=t, seqs=_pick(b, 8))
    tp = -(-t // BF16_ROWS) * BF16_ROWS
    assert tp <= HEAD_DIM
    pad = lambda a: jnp.pad(a.reshape(b, t, RWKV_WIDTH), ((0, 0), (0, tp - t), (0, 0)))
    o, s_new = _wkv(pad(r), pad(lw), pad(k2), pad(vv), pad(kk), pad(kb), s0, nseq=_pick(b, WKV_SAMPLE_SEQS),
                    nchunks=1, rows=tp)
    y = _post(o[:, :t].reshape(n, -1), bonus, g, att, x2, p, tm=_pick(n, POST_ROWS), final=final)
    shift = _rms_rows(x[:, -1], p["g_mix"])
    return (y.reshape(b, t, d), k_win.reshape(b, wb, KV_HEADS, HEAD_DIM), v_win.reshape(b, wb, KV_HEADS, HEAD_DIM),
            s_new, shift)


def kernel(x_prompt, x_sample, cache_k, cache_v, state_wkv, state_shift, g_mix, w_in, attn_sinks, rwkv_mu, w0, w2,
           a0, a2, g2, k_k, k_a, r_k, gn_g, gn_b, w_out, g_ffn, w_gate, w_up, w_down, g_final):
    depth = w_in.shape[0]
    xp, xs = x_prompt, x_sample
    outs_p, outs_s = [], []
    for l in range(depth):
        p = _layer_params(l, g_mix, w_in, attn_sinks, rwkv_mu, w0, w2, a0, a2, g2, k_k, k_a, r_k, gn_g, gn_b,
                          w_out, g_ffn, w_gate, w_up, w_down, g_final)
        final = l == depth - 1
        xp, kp, vp, sp, hp = _prompt_layer(xp, p, final)
        xs, kn, vn, sn, hn = _sample_layer(xs, state_shift[l], cache_k[l], cache_v[l], state_wkv[l], p, final)
        outs_p.append((kp, vp, sp, hp))
        outs_s.append((kn, vn, sn, hn))
    stack = lambda outs, i: jnp.stack([o[i] for o in outs])
    return (xp, xs,
            stack(outs_p, 0), stack(outs_p, 1), stack(outs_p, 2), stack(outs_p, 3),
            stack(outs_s, 0), stack(outs_s, 1), stack(outs_s, 2), stack(outs_s, 3))
```

```python
import functools

import jax
import jax.numpy as jnp
from jax import lax
from jax.experimental import pallas as pl
from jax.experimental.pallas import tpu as pltpu

F32 = jnp.float32
BF16 = jnp.bfloat16

D_MODEL = 1024
HEAD_DIM = 64
ATT_HEADS = 8
KV_HEADS = 2
ATT_WIDTH = ATT_HEADS * HEAD_DIM
KV_WIDTH = KV_HEADS * HEAD_DIM
RWKV_HEADS = 8
RWKV_WIDTH = RWKV_HEADS * HEAD_DIM
WINDOW = 128
DECAY_LORA = 64
ICL_LORA = 64
GATE_LORA = 128
RWKV_BLOCK = 3 * RWKV_WIDTH + DECAY_LORA + ICL_LORA + GATE_LORA
ATT_PROJ = ATT_WIDTH + 2 * KV_WIDTH
PROJ_WIDTH = ATT_PROJ + RWKV_BLOCK
RMS_EPS = 1e-6
GN_EPS = 64e-5
L2_EPS = 1e-12
NEG_INF = -1e30

V7X_VMEM_BYTES = 64 * 1024 * 1024
VMEM_LIMIT_BYTES = V7X_VMEM_BYTES * 3 // 4

WKV_PROMPT_SEQS = 2
WKV_PROMPT_CHUNKS = 2
WKV_SAMPLE_SEQS = 8
WKV_PASSES = dict(gram=1, inverse=1, apply=1, state=3, carry=3)
BF16_ROWS = 16
ATTN_QUERY_BLOCKS = 4
POST_ROWS = 512
POST_SUB_ROWS = 256
INPROJ_SUB_ROWS = 256

NN = (((1,), (0,)), ((), ()))
NT = (((1,), (1,)), ((), ()))
TN = (((0,), (0,)), ((), ()))


def _dot(a, b, dn=NN):
    return lax.dot_general(a, b, dn, preferred_element_type=F32)


def _split2(x):
    hi = x.astype(BF16)
    lo = (x - hi.astype(F32)).astype(BF16)
    return hi, lo


def _split3(x):
    hi = x.astype(BF16)
    r1 = x - hi.astype(F32)
    mid = r1.astype(BF16)
    lo = (r1 - mid.astype(F32)).astype(BF16)
    return hi, mid, lo


def _dot3(xs, ys, dn=NN):
    xh, xl = xs
    yh, yl = ys
    return _dot(xh, yh, dn) + (_dot(xh, yl, dn) + _dot(xl, yh, dn))


def _cat3(x, axis, order):
    hi, lo = _split2(x)
    return jnp.concatenate([hi if ch == "h" else lo for ch in order], axis=axis)


def _seg_sum(x, ones_bd):
    hi, lo = _split2(x)
    pw = 2 * HEAD_DIM
    slabs = [slice(i, i + pw) for i in range(0, x.shape[1], pw)]
    return jnp.concatenate([_dot(hi[:, s], ones_bd) + _dot(lo[:, s], ones_bd) for s in slabs], axis=1)


def _rms(x, g):
    return x * lax.rsqrt(jnp.mean(x * x, axis=-1, keepdims=True) + RMS_EPS) * g


def _const_spec(shape):
    return pl.BlockSpec(shape, lambda *_: (0,) * len(shape))


def _params(*sem):
    return pltpu.CompilerParams(dimension_semantics=sem, vmem_limit_bytes=VMEM_LIMIT_BYTES)


def _rms_rows_kernel(x_ref, g_ref, o_ref):
    o_ref[...] = _rms(x_ref[...], g_ref[...])


def _rms_rows(x, g):
    n, d = x.shape
    return pl.pallas_call(
        _rms_rows_kernel,
        out_shape=jax.ShapeDtypeStruct((n, d), F32),
        name="rms_rows",
    )(x, g.reshape(1, d))


def _inproj_kernel(x_ref, g_ref, w_ref, q_ref, k_ref, v_ref, f_ref, *, normalize):
    x = x_ref[...]
    h = _rms(x, g_ref[...]) if normalize else x
    proj = _dot(h.astype(BF16), w_ref[...])
    q_ref[...] = proj[:, :ATT_WIDTH].astype(q_ref.dtype)
    k_ref[...] = proj[:, ATT_WIDTH:ATT_WIDTH + KV_WIDTH]
    v_ref[...] = proj[:, ATT_WIDTH + KV_WIDTH:ATT_PROJ]
    f_ref[...] = proj[:, ATT_PROJ:]


def _inproj(x, g_mix, w_in_bf, *, normalize, tm, q_dtype=BF16):
    n = x.shape[0]
    row = lambda w: pl.BlockSpec((tm, w), lambda i: (i, 0))
    return pl.pallas_call(
        functools.partial(_inproj_kernel, normalize=normalize),
        grid=(n // tm,),
        in_specs=[row(D_MODEL), _const_spec((1, D_MODEL)), _const_spec((D_MODEL, PROJ_WIDTH))],
        out_specs=[row(ATT_WIDTH), row(KV_WIDTH), row(KV_WIDTH), row(RWKV_BLOCK)],
        out_shape=[
            jax.ShapeDtypeStruct((n, ATT_WIDTH), q_dtype),
            jax.ShapeDtypeStruct((n, KV_WIDTH), F32),
            jax.ShapeDtypeStruct((n, KV_WIDTH), F32),
            jax.ShapeDtypeStruct((n, RWKV_BLOCK), F32),
        ],
        compiler_params=_params("arbitrary"),
        name="inproj",
    )(x, g_mix.reshape(1, D_MODEL), w_in_bf)


def _softplus(z):
    return jnp.maximum(z, 0.0) + jnp.log(1.0 + jnp.exp(-jnp.abs(z)))


def _sigmoid(z):
    return 1.0 / (1.0 + jnp.exp(-z))


N_PREP_PARAMS = 10
N_PREP_OUTS = 8


def _reset_carry(carry_ref, blocks_per_seq):
    @pl.when(pl.program_id(0) % blocks_per_seq == 0)
    def _():
        carry_ref[...] = jnp.zeros_like(carry_ref)


def _carried_prev(f, carry_ref):
    tm = f.shape[0]
    rolled = pltpu.roll(f, 1, 0)
    row = lax.broadcasted_iota(jnp.int32, f.shape, 0)
    prev = jnp.where(row == 0, carry_ref[0:1, :], rolled)
    carry_ref[0:1, :] = f[tm - 1:tm, :]
    return prev


def _prep_kernel(*refs, blocks_per_seq, has_prev):
    if has_prev:
        f_ref, p_ref = refs[:2]
        refs = refs[2:]
    else:
        f_ref = refs[0]
        refs = refs[1:]
    if not has_prev:
        _reset_carry(refs[N_PREP_PARAMS + N_PREP_OUTS], blocks_per_seq)
    f = f_ref[...]
    prev = p_ref[...] if has_prev else _carried_prev(f, refs[N_PREP_PARAMS + N_PREP_OUTS])
    _prep_math(f, prev, refs[:N_PREP_PARAMS], refs[N_PREP_PARAMS:N_PREP_PARAMS + N_PREP_OUTS])


def _inproj_prep_kernel(x_ref, g_ref, w_ref, *refs, blocks_per_seq):
    q_ref, k_ref, v_ref = refs[N_PREP_PARAMS:N_PREP_PARAMS + 3]
    outs = refs[N_PREP_PARAMS + 3:N_PREP_PARAMS + 3 + N_PREP_OUTS]
    carry_ref = refs[N_PREP_PARAMS + 3 + N_PREP_OUTS]
    _reset_carry(carry_ref, blocks_per_seq)
    tm = x_ref.shape[0]
    sub = min(tm, INPROJ_SUB_ROWS)
    parts = [slice(i * sub, (i + 1) * sub) for i in range(tm // sub)]
    proj = [_dot(_rms(x_ref[rows, :], g_ref[...]).astype(BF16), w_ref[...]) for rows in parts]
    for rows, pj in zip(parts, proj):
        q_ref[rows, :] = pj[:, :ATT_WIDTH].astype(q_ref.dtype)
        k_ref[rows, :] = pj[:, ATT_WIDTH:ATT_WIDTH + KV_WIDTH]
        v_ref[rows, :] = pj[:, ATT_WIDTH + KV_WIDTH:ATT_PROJ]
    for rows, pj in zip(parts, proj):
        f = pj[:, ATT_PROJ:]
        _prep_math(f, _carried_prev(f, carry_ref), refs[:N_PREP_PARAMS], [o.at[rows, :] for o in outs])


def _prep_math(f, prev, params, outs):
    (mu_ref, w0_ref, w2_ref, a0_ref, a2_ref, g2_ref, kk_ref, ka_ref, rk_ref, ones_ref) = params
    (r_out, l_out, k_out, v_out, kk_out, kb_out, g_out, bo_out) = outs
    xs = f + (prev - f) * mu_ref[...]
    w3 = 3 * RWKV_WIDTH
    r = xs[:, :RWKV_WIDTH]
    k = xs[:, RWKV_WIDTH:2 * RWKV_WIDTH]
    v = xs[:, 2 * RWKV_WIDTH:w3]
    wl = xs[:, w3:w3 + DECAY_LORA]
    al = xs[:, w3 + DECAY_LORA:w3 + DECAY_LORA + ICL_LORA]
    gl = xs[:, w3 + DECAY_LORA + ICL_LORA:]
    wr = -_softplus(-(w0_ref[...] + _dot(jnp.tanh(wl).astype(BF16), w2_ref[...]))) - 0.5
    a = _sigmoid(a0_ref[...] + _dot(al.astype(BF16), a2_ref[...]))
    g = _dot(_sigmoid(gl).astype(BF16), g2_ref[...])
    ones_bd = ones_ref[...]
    kk = k * kk_ref[...]
    kk = kk * lax.rsqrt(_seg_sum(kk * kk, ones_bd) + L2_EPS)
    k2 = k * (1.0 + (a - 1.0) * ka_ref[...])
    r_out[...] = r
    l_out[...] = -jnp.exp(wr)
    k_out[...] = k2
    v_out[...] = v
    kk_out[...] = kk
    kb_out[...] = kk * a
    g_out[...] = g
    bo_out[...] = _seg_sum(r * k2 * rk_ref[...], ones_bd) * v


def _prep_params(p):
    w = RWKV_WIDTH
    ins = [p["mu"], p["w0"], p["w2"], p["a0"], p["a2"], p["g2"], p["k_k"], p["k_a"], p["r_k"], p["ones_bd"]]
    specs = [_const_spec((1, RWKV_BLOCK)), _const_spec((1, w)), _const_spec((DECAY_LORA, w)),
             _const_spec((1, w)), _const_spec((ICL_LORA, w)), _const_spec((GATE_LORA, w)),
             _const_spec((1, w)), _const_spec((1, w)), _const_spec((1, w)),
             _const_spec((2 * HEAD_DIM, 2 * HEAD_DIM))]
    assert len(ins) == N_PREP_PARAMS
    return ins, specs


def _inproj_prep(x, p, *, tm, blocks_per_seq):
    n = x.shape[0]
    row = lambda w: pl.BlockSpec((tm, w), lambda i: (i, 0))
    w = RWKV_WIDTH
    prm, prm_specs = _prep_params(p)
    return pl.pallas_call(
        functools.partial(_inproj_prep_kernel, blocks_per_seq=blocks_per_seq),
        grid=(n // tm,),
        in_specs=[row(D_MODEL), _const_spec((1, D_MODEL)), _const_spec((D_MODEL, PROJ_WIDTH))] + prm_specs,
        out_specs=[row(ATT_WIDTH), row(KV_WIDTH), row(KV_WIDTH)] + [row(w)] * N_PREP_OUTS,
        out_shape=[jax.ShapeDtypeStruct((n, ATT_WIDTH), BF16), jax.ShapeDtypeStruct((n, KV_WIDTH), F32),
                   jax.ShapeDtypeStruct((n, KV_WIDTH), F32)] + [jax.ShapeDtypeStruct((n, w), F32)] * N_PREP_OUTS,
        scratch_shapes=[pltpu.VMEM((8, RWKV_BLOCK), F32)],
        compiler_params=_params("arbitrary"),
        name="inproj_prep",
    )(x, p["g_mix"].reshape(1, D_MODEL), p["w_in"], *prm)


def _prep(feat, prev, p, *, tm, blocks_per_seq):
    n = feat.shape[0]
    has_prev = prev is not None
    row = lambda w: pl.BlockSpec((tm, w), lambda i: (i, 0))
    w = RWKV_WIDTH
    ins = [feat] + ([prev] if has_prev else [])
    in_specs = [row(RWKV_BLOCK)] * len(ins)
    prm, prm_specs = _prep_params(p)
    ins += prm
    in_specs += prm_specs
    return pl.pallas_call(
        functools.partial(_prep_kernel, blocks_per_seq=blocks_per_seq, has_prev=has_prev),
        grid=(n // tm,),
        in_specs=in_specs,
        out_specs=[row(w)] * 8,
        out_shape=[jax.ShapeDtypeStruct((n, w), F32)] * 8,
        scratch_shapes=[] if has_prev else [pltpu.VMEM((8, RWKV_BLOCK), F32)],
        compiler_params=_params("arbitrary"),
        name="rwkv_prep",
    )(*ins)


def _kv_operands(kblk, vblk):
    lo_m = lax.broadcasted_iota(jnp.int32, (1, 2 * HEAD_DIM), 1) < HEAD_DIM
    k_sw = pltpu.roll(kblk, HEAD_DIM, 1)
    v_sw = pltpu.roll(vblk, HEAD_DIM, 1)
    ops = []
    for kv in range(KV_HEADS):
        own = lo_m if kv == 0 else ~lo_m
        kdup = jnp.where(own, kblk, k_sw).astype(BF16)
        v_own = jnp.where(own, vblk, 0.0).astype(BF16)
        v_oth = jnp.where(own, 0.0, v_sw).astype(BF16)
        ops.append((kdup,) + ((v_own, v_oth) if kv == 0 else (v_oth, v_own)))
    return ops


def _attend(q, kv_ops, mask, sink_ref, o_ref):
    nq = q.shape[0]
    lo_m = lax.broadcasted_iota(jnp.int32, (1, 2 * HEAD_DIM), 1) < HEAD_DIM
    mask2 = jnp.concatenate([mask, mask], axis=0)
    rowi = lax.broadcasted_iota(jnp.int32, (2 * nq, 1), 0)
    for kv in range(KV_HEADS):
        kdup, v_lo, v_hi = kv_ops[kv]
        for pr in range(2):
            c0 = (kv * 2 + pr) * 2 * HEAD_DIM
            qp = q[:, c0:c0 + 2 * HEAD_DIM].astype(F32) * (HEAD_DIM ** -0.5)
            qs = jnp.concatenate([jnp.where(lo_m, qp, 0.0), jnp.where(lo_m, 0.0, qp)], axis=0)
            s = jnp.where(mask2, _dot(qs.astype(BF16), kdup, NT), NEG_INF)
            h0 = (kv * 2 + pr) * 2
            sink = jnp.where(rowi < nq, sink_ref[h0], sink_ref[h0 + 1])
            m = jnp.maximum(jnp.max(s, axis=-1, keepdims=True), sink)
            e = jnp.exp(s - m)
            rden = 1.0 / (jnp.sum(e, axis=-1, keepdims=True) + jnp.exp(sink - m))
            out = _dot(e[:nq].astype(BF16), v_lo) + _dot(e[nq:].astype(BF16), v_hi)
            out = out * jnp.where(lo_m, rden[:nq], rden[nq:])
            o_ref[:, c0:c0 + 2 * HEAD_DIM] = out.astype(o_ref.dtype)


def _band_mask(nq, q0):
    qi = lax.broadcasted_iota(jnp.int32, (nq, 2 * WINDOW), 0) + q0
    kj = lax.broadcasted_iota(jnp.int32, (nq, 2 * WINDOW), 1)
    return (kj <= qi) & (qi - kj < WINDOW), kj


def _attn_prompt_kernel(sink_ref, q_ref, kp_ref, kc_ref, vp_ref, vc_ref, o_ref, *, qblocks):
    j = pl.program_id(1)
    band, kj = _band_mask(WINDOW, WINDOW)
    blk = lambda ref, i: ref[0, i * WINDOW:(i + 1) * WINDOW, :]
    ops = [_kv_operands(kp_ref[0], vp_ref[0])] + [_kv_operands(blk(kc_ref, i), blk(vc_ref, i))
                                                  for i in range(qblocks)]
    for i in range(qblocks):
        mask = band & ((j > 0) | (kj >= WINDOW)) if i == 0 else band
        kv_ops = [tuple(jnp.concatenate([ops[i][kv][n], ops[i + 1][kv][n]], axis=0) for n in range(3))
                  for kv in range(KV_HEADS)]
        rows = slice(i * WINDOW, (i + 1) * WINDOW)
        _attend(q_ref[0, rows, :], kv_ops, mask, sink_ref, o_ref.at[0, rows, :])


def _attn_prompt(q, k, v, sinks):
    b, t, _ = q.shape
    qblocks = _pick(t // WINDOW, ATTN_QUERY_BLOCKS)
    nb = t // (qblocks * WINDOW)
    cur = lambda w: pl.BlockSpec((1, qblocks * WINDOW, w), lambda bi, j: (bi, j, 0))
    prv = lambda w: pl.BlockSpec((1, WINDOW, w), lambda bi, j: (bi, jnp.maximum(j * qblocks - 1, 0), 0))
    return pl.pallas_call(
        functools.partial(_attn_prompt_kernel, qblocks=qblocks),
        grid=(b, nb),
        in_specs=[pl.BlockSpec(memory_space=pltpu.SMEM), cur(ATT_WIDTH), prv(KV_WIDTH), cur(KV_WIDTH),
                  prv(KV_WIDTH), cur(KV_WIDTH)],
        out_specs=cur(ATT_WIDTH),
        out_shape=jax.ShapeDtypeStruct((b, t, ATT_WIDTH), BF16),
        compiler_params=_params("arbitrary", "arbitrary"),
        name="attn_prompt",
    )(sinks, q, k, k, v, v)


def _attn_sample_kernel(sink_ref, q_ref, kn_ref, vn_ref, ck_ref, cv_ref, o_ref, kw_ref, vw_ref,
                        kall_ref, vall_ref, *, seqs, t):
    wb = ck_ref.shape[1]

    @pl.when(pl.program_id(0) == 0)
    def _():
        kall_ref[...] = jnp.zeros_like(kall_ref)
        vall_ref[...] = jnp.zeros_like(vall_ref)

    mask, _ = _band_mask(t, wb)
    for s in range(seqs):
        rows = slice(s * t, (s + 1) * t)
        kall_ref[0:wb, :] = ck_ref[s]
        vall_ref[0:wb, :] = cv_ref[s]
        kall_ref[wb:wb + t, :] = kn_ref[rows, :]
        vall_ref[wb:wb + t, :] = vn_ref[rows, :]
        _attend(q_ref[rows, :], _kv_operands(kall_ref[...], vall_ref[...]), mask, sink_ref, o_ref.at[rows, :])
        kw_ref[s] = kall_ref[t:t + wb, :]
        vw_ref[s] = vall_ref[t:t + wb, :]


def _attn_sample(q, k, v, cache_k, cache_v, sinks, *, t, seqs):
    n = q.shape[0]
    b, wb, _ = cache_k.shape
    row = lambda w: pl.BlockSpec((seqs * t, w), lambda i: (i, 0))
    win = pl.BlockSpec((seqs, wb, KV_WIDTH), lambda i: (i, 0, 0))
    return pl.pallas_call(
        functools.partial(_attn_sample_kernel, seqs=seqs, t=t),
        grid=(b // seqs,),
        in_specs=[pl.BlockSpec(memory_space=pltpu.SMEM), row(ATT_WIDTH), row(KV_WIDTH), row(KV_WIDTH), win, win],
        out_specs=[row(ATT_WIDTH), win, win],
        out_shape=[jax.ShapeDtypeStruct((n, ATT_WIDTH), F32),
                   jax.ShapeDtypeStruct((b, wb, KV_WIDTH), F32),
                   jax.ShapeDtypeStruct((b, wb, KV_WIDTH), F32)],
        scratch_shapes=[pltpu.VMEM((2 * WINDOW, KV_WIDTH), F32), pltpu.VMEM((2 * WINDOW, KV_WIDTH), F32)],
        compiler_params=_params("arbitrary"),
        name="attn_sample",
    )(sinks, q, k, v, cache_k, cache_v)


def _wkv_kernel(r_ref, l_ref, k_ref, v_ref, kk_ref, kb_ref, s0_ref, o_ref, st_ref, z_ref, *, nseq, nchunks, rows,
                real):
    c = rows
    pw = 2 * HEAD_DIM
    j = pl.program_id(1)
    npairs = RWKV_HEADS // 2
    pairs = range(npairs)

    @pl.when(j == 0)
    def _():
        for s in range(nseq):
            for p in pairs:
                z_ref[s * npairs + p] = jnp.concatenate([s0_ref[s, 2 * p], s0_ref[s, 2 * p + 1]], axis=0).T

    def pair_iota(nrow):
        return (lax.broadcasted_iota(jnp.int32, (nrow, pw), 0),
                lax.broadcasted_iota(jnp.int32, (nrow, pw), 1) & (HEAD_DIM - 1))

    row, col = pair_iota(c)
    lo = lax.broadcasted_iota(jnp.int32, (1, pw), 1) < HEAD_DIM
    lower2 = jnp.concatenate([row > col, row >= col], axis=0)
    eye = (row == col).astype(F32)
    krow, kcol = pair_iota(HEAD_DIM)
    eye_ch = (krow == kcol).astype(F32)
    ri = lax.broadcasted_iota(jnp.int32, (c, c), 0)
    tri = (ri >= lax.broadcasted_iota(jnp.int32, (c, c), 1)).astype(BF16)

    def bd(y):
        zero = jnp.zeros_like(y)
        parts = [jnp.where(lo, y, zero), jnp.where(lo, zero, y)]
        if y.shape[0] < HEAD_DIM:
            fill = jnp.zeros((HEAD_DIM - y.shape[0], pw), y.dtype)
            parts = [parts[0], fill, parts[1], fill]
        return jnp.concatenate(parts, axis=0)

    memo = {}

    def cached(x, tag, build):
        key = (id(x), tag)
        if key not in memo:
            memo[key] = (x, build())
        return memo[key][1]

    def split(x):
        return cached(x, "split", lambda: _split2(x))

    def lhs(x, passes, axis=1):
        hi, low = split(x)
        return cached(x, ("lhs", passes, axis), lambda: jnp.concatenate([hi, low, hi][:passes], axis=axis))

    def rhs(y, passes, axis=0):
        hi, low = split(y)

        def build():
            bh = bd(hi)
            return jnp.concatenate([bh, bh, bd(low)][:passes], axis=axis)
        return cached(y, ("rhs", passes, axis), build)

    def mm(x, y, group):
        return _dot(lhs(x, WKV_PASSES[group]), rhs(y, WKV_PASSES[group]))

    def mm_nt(x, y, group):
        return _dot(lhs(x, WKV_PASSES[group]), rhs(y, WKV_PASSES[group], 1), NT)

    def mm_tn(x, y, group):
        passes = WKV_PASSES[group]
        yh, yl = split(y)
        full = _dot(lhs(x, passes, 0), jnp.concatenate([yh, yh, yl][:passes], axis=0), TN)
        return jnp.where(lo, full[:HEAD_DIM], full[HEAD_DIM:])

    prep = {}
    for s in range(nseq):
        for q in range(nchunks):
            tr = slice(q * real, (q + 1) * real)

            def tok(ref):
                x = ref[s, tr, :]
                return x if real == c else jnp.concatenate([x, jnp.zeros((c - real, x.shape[1]), x.dtype)], axis=0)

            lw = tok(l_ref)
            cum = sum(_dot(tri, part) for part in _split3(lw))
            tot = cum[c - 1:c, :]
            en = jnp.exp(-cum)
            ed = jnp.exp(tot - cum)
            kk = tok(kk_ref)
            kb = tok(kb_ref)
            kx = tok(k_ref)
            prep[s, q] = dict(a=-kk * jnp.exp(cum - lw), r=tok(r_ref) * jnp.exp(cum), b=kb * en, k=kx * en,
                              bh=kb * ed, kh=kx * ed, v=tok(v_ref), etot=jnp.exp(tot))

    items = [(s, q, p) for s in range(nseq) for q in range(nchunks) for p in pairs]
    ps = lambda name, it: prep[it[0], it[1]][name][:, it[2] * pw:(it[2] + 1) * pw]
    each = lambda f: [f(i) for i in range(len(items))]
    pv = {name: each(lambda i: ps(name, items[i])) for name in ("a", "r", "b", "k", "bh", "kh", "v", "etot")}
    ar = each(lambda i: jnp.concatenate([pv["a"][i], pv["r"][i]], axis=0))
    g_b = each(lambda i: jnp.where(lower2, mm_nt(ar[i], pv["b"][i], "gram"), 0.0))
    g_k = each(lambda i: jnp.where(lower2, mm_nt(ar[i], pv["k"][i], "gram"), 0.0))
    a_ak = each(lambda i: g_k[i][:c])
    a_rb = each(lambda i: g_b[i][c:])
    a_rk = each(lambda i: g_k[i][c:])
    p_m = each(lambda i: g_b[i][:c])
    t_m = each(lambda i: eye + p_m[i])
    n = 2
    while n < c:
        p_m = each(lambda i: mm(p_m[i], p_m[i], "inverse"))
        t_m = each(lambda i: t_m[i] + mm(p_m[i], t_m[i], "inverse"))
        n *= 2
    av = each(lambda i: mm(a_ak[i], pv["v"][i], "apply"))
    w_m = each(lambda i: mm(t_m[i], pv["a"][i], "apply"))
    u_m = each(lambda i: mm(t_m[i], av[i], "apply"))
    r_p = each(lambda i: pv["r"][i] + mm(a_rb[i], w_m[i], "apply"))
    o_i = each(lambda i: mm(a_rb[i], u_m[i], "apply") + mm(a_rk[i], pv["v"][i], "apply"))
    m_m = each(lambda i: mm_tn(pv["bh"][i], w_m[i], "state") + eye_ch * pv["etot"][i])
    bk = each(lambda i: jnp.concatenate([pv["bh"][i], pv["kh"][i]], axis=0))
    uv = each(lambda i: jnp.concatenate([u_m[i], pv["v"][i]], axis=0))
    n_m = each(lambda i: mm_tn(bk[i], uv[i], "state"))
    rm = each(lambda i: jnp.concatenate([r_p[i], m_m[i]], axis=0))
    z = [z_ref[sp] for sp in range(nseq * npairs)]
    for q in range(nchunks):
        for s in range(nseq):
            for p in pairs:
                i = items.index((s, q, p))
                sp = s * npairs + p
                oz = mm(rm[i], z[sp], "carry")
                o_ref[s, q * real:(q + 1) * real, p * pw:(p + 1) * pw] = (oz[:c] + o_i[i])[:real]
                z[sp] = oz[c:] + n_m[i]
    for sp in range(nseq * npairs):
        z_ref[sp] = z[sp]

    @pl.when(j == pl.num_programs(1) - 1)
    def _():
        for s in range(nseq):
            for p in pairs:
                zt = z_ref[s * npairs + p].T
                st_ref[s, 2 * p] = zt[:HEAD_DIM]
                st_ref[s, 2 * p + 1] = zt[HEAD_DIM:]


def _wkv(r, lw, k, v, kk, kb, s0, *, nseq, nchunks, rows, real=None):
    real = rows if real is None else real
    assert real == rows or nchunks == 1
    b, t, w = r.shape
    blk = nchunks * real
    tok = pl.BlockSpec((nseq, blk, w), lambda bi, j: (bi, j, 0))
    st = pl.BlockSpec((nseq, RWKV_HEADS, HEAD_DIM, HEAD_DIM), lambda bi, j: (bi, 0, 0, 0))
    return pl.pallas_call(
        functools.partial(_wkv_kernel, nseq=nseq, nchunks=nchunks, rows=rows, real=real),
        grid=(b // nseq, t // blk),
        in_specs=[tok] * 6 + [st],
        out_specs=[tok, st],
        out_shape=[jax.ShapeDtypeStruct((b, t, w), F32),
                   jax.ShapeDtypeStruct((b, RWKV_HEADS, HEAD_DIM, HEAD_DIM), F32)],
        scratch_shapes=[pltpu.VMEM((nseq * RWKV_HEADS // 2, HEAD_DIM, 2 * HEAD_DIM), F32)],
        compiler_params=_params("arbitrary", "arbitrary"),
        name="wkv_scan",
    )(r, lw, k, v, kk, kb, s0)


def _post_kernel(o_ref, bo_ref, g_ref, att_ref, x_ref, gng_ref, gnb_ref, ones_ref, wo_ref, gf_ref,
                 wg_ref, wu_ref, wd_ref, gfin_ref, y_ref, *, final):
    tm = x_ref.shape[0]
    sub = min(tm, POST_SUB_ROWS)
    parts = [slice(i * sub, (i + 1) * sub) for i in range(tm // sub)]
    each = lambda f: [f(i) for i in range(len(parts))]
    ones_bd = ones_ref[...]
    o = each(lambda i: o_ref[parts[i], :])
    mean = each(lambda i: _seg_sum(o[i], ones_bd) * (1.0 / HEAD_DIM))
    oc = each(lambda i: o[i] - mean[i])
    var = each(lambda i: _seg_sum(oc[i] * oc[i], ones_bd) * (1.0 / HEAD_DIM))
    rw = each(lambda i: ((oc[i] * lax.rsqrt(var[i] + GN_EPS) * gng_ref[...] + gnb_ref[...] + bo_ref[parts[i], :])
                         * g_ref[parts[i], :]).astype(BF16))
    x1 = each(lambda i: x_ref[parts[i], :] + _dot(att_ref[parts[i], :].astype(BF16), wo_ref[:ATT_WIDTH, :])
              + _dot(rw[i], wo_ref[ATT_WIDTH:, :]))
    u = each(lambda i: _rms(x1[i], gf_ref[...]).astype(BF16))
    gate = each(lambda i: _dot(u[i], wg_ref[...]))
    up = each(lambda i: _dot(u[i], wu_ref[...]))
    hid = each(lambda i: (gate[i] * _sigmoid(gate[i]) * up[i]).astype(BF16))
    x2 = each(lambda i: x1[i] + _dot(hid[i], wd_ref[...]))
    for i in range(len(parts)):
        y_ref[parts[i], :] = _rms(x2[i], gfin_ref[...]) if final else x2[i]


def _post(o, bonus, g, att, x, p, *, tm, final):
    n = x.shape[0]
    d_ff = p["w_gate"].shape[1]
    row = lambda w: pl.BlockSpec((tm, w), lambda i: (i, 0))
    once = lambda shape: pl.BlockSpec(shape, lambda i: (0,) * len(shape), pipeline_mode=pl.Buffered(1))
    w = RWKV_WIDTH
    return pl.pallas_call(
        functools.partial(_post_kernel, final=final),
        grid=(n // tm,),
        in_specs=[row(w), row(w), row(w), row(ATT_WIDTH), row(D_MODEL),
                  once((1, w)), once((1, w)), once((2 * HEAD_DIM, 2 * HEAD_DIM)), once((ATT_WIDTH + w, D_MODEL)),
                  once((1, D_MODEL)),
                  once((D_MODEL, d_ff)), once((D_MODEL, d_ff)), once((d_ff, D_MODEL)), once((1, D_MODEL))],
        out_specs=row(D_MODEL),
        out_shape=jax.ShapeDtypeStruct((n, D_MODEL), F32),
        compiler_params=_params("arbitrary"),
        name="post_ffn",
    )(o, bonus, g, att, x, p["gn_g"], p["gn_b"], p["ones_bd"], p["w_out"], p["g_ffn"],
      p["w_gate"], p["w_up"], p["w_down"], p["g_final"])


def _layer_params(l, g_mix, w_in, attn_sinks, rwkv_mu, w0, w2, a0, a2, g2, k_k, k_a, r_k, gn_g, gn_b,
                  w_out, g_ffn, w_gate, w_up, w_down, g_final):
    vec = lambda a: a.reshape(1, -1).astype(F32)
    hd = jnp.arange(2 * HEAD_DIM) // HEAD_DIM
    return dict(
        g_mix=g_mix[l], w_in=w_in[l].astype(BF16), sinks=attn_sinks[l].astype(F32),
        mu=vec(rwkv_mu[l]), w0=vec(w0[l]), w2=w2[l].astype(BF16), a0=vec(a0[l]), a2=a2[l].astype(BF16),
        g2=g2[l].astype(BF16), k_k=vec(k_k[l]), k_a=vec(k_a[l]), r_k=vec(r_k[l]),
        gn_g=vec(gn_g[l]), gn_b=vec(gn_b[l]), w_out=w_out[l].astype(BF16), g_ffn=vec(g_ffn[l]),
        w_gate=w_gate[l].astype(BF16), w_up=w_up[l].astype(BF16), w_down=w_down[l].astype(BF16),
        g_final=vec(g_final), ones_bd=(hd[:, None] == hd[None, :]).astype(BF16),
    )


def _pick(n, pref):
    t = pref
    while n % t:
        t //= 2
    return t


def _prompt_layer(x, p, final):
    b, t, d = x.shape
    n = b * t
    x2 = x.reshape(n, d)
    tm = _pick(t, 512)
    q, k, v, r, lw, k2, vv, kk, kb, g, bonus = _inproj_prep(x2, p, tm=tm, blocks_per_seq=t // tm)
    att = _attn_prompt(q.reshape(b, t, -1), k.reshape(b, t, -1), v.reshape(b, t, -1), p["sinks"])
    s0 = jnp.zeros((b, RWKV_HEADS, HEAD_DIM, HEAD_DIM), F32)
    sh = lambda a: a.reshape(b, t, RWKV_WIDTH)
    o, s_new = _wkv(sh(r), sh(lw), sh(k2), sh(vv), sh(kk), sh(kb), s0, nseq=_pick(b, WKV_PROMPT_SEQS),
                    nchunks=_pick(t // HEAD_DIM, WKV_PROMPT_CHUNKS), rows=HEAD_DIM)
    y = _post(o.reshape(n, -1), bonus, g, att.reshape(n, -1), x2, p, tm=_pick(n, POST_ROWS), final=final)
    wp = min(WINDOW, t)
    k_win = k.reshape(b, t, KV_WIDTH)[:, t - wp:].reshape(b, wp, KV_HEADS, HEAD_DIM)
    v_win = v.reshape(b, t, KV_WIDTH)[:, t - wp:].reshape(b, wp, KV_HEADS, HEAD_DIM)
    shift = _rms_rows(x[:, -1], p["g_mix"])
    return y.reshape(b, t, d), k_win, v_win, s_new, shift


def _sample_layer(x, h_prev, k_buf, v_buf, s0, p, final):
    b, t, d = x.shape
    n = b * t
    x2 = x.reshape(n, d)
    wb = k_buf.shape[1]
    q, k, v, feat = _inproj(x2, p["g_mix"], p["w_in"], normalize=True, tm=_pick(n, 512), q_dtype=F32)
    _, _, _, feat_prev = _inproj(h_prev, p["g_mix"], p["w_in"], normalize=False, tm=_pick(b, 512))
    f3 = feat.reshape(b, t, -1)
    prev = jnp.concatenate([feat_prev[:, None], f3[:, :-1]], axis=1).reshape(n, -1)
    r, lw, k2, vv, kk, kb, g, bonus = _prep(feat, prev, p, tm=_pick(n, 512), blocks_per_seq=1)
    att, k_win, v_win = _attn_sample(q, k, v, k_buf.reshape(b, wb, -1), v_buf.reshape(b, wb, -1),
                                     p["sinks"], t=t, seqs=_pick(b, 8))
    tp = -(-t // BF16_ROWS) * BF16_ROWS
    assert tp <= HEAD_DIM and t % 8 == 0
    sh = lambda a: a.reshape(b, t, RWKV_WIDTH)
    o, s_new = _wkv(sh(r), sh(lw), sh(k2), sh(vv), sh(kk), sh(kb), s0, nseq=_pick(b, WKV_SAMPLE_SEQS),
                    nchunks=1, rows=tp, real=t)
    y = _post(o.reshape(n, -1), bonus, g, att, x2, p, tm=_pick(n, POST_ROWS), final=final)
    shift = _rms_rows(x[:, -1], p["g_mix"])
    return (y.reshape(b, t, d), k_win.reshape(b, wb, KV_HEADS, HEAD_DIM), v_win.reshape(b, wb, KV_HEADS, HEAD_DIM),
            s_new, shift)


def kernel(x_prompt, x_sample, cache_k, cache_v, state_wkv, state_shift, g_mix, w_in, attn_sinks, rwkv_mu, w0, w2,
           a0, a2, g2, k_k, k_a, r_k, gn_g, gn_b, w_out, g_ffn, w_gate, w_up, w_down, g_final):
    depth = w_in.shape[0]
    xp, xs = x_prompt, x_sample
    outs_p, outs_s = [], []
    for l in range(depth):
        p = _layer_params(l, g_mix, w_in, attn_sinks, rwkv_mu, w0, w2, a0, a2, g2, k_k, k_a, r_k, gn_g, gn_b,
                          w_out, g_ffn, w_gate, w_up, w_down, g_final)
        final = l == depth - 1
        xp, kp, vp, sp, hp = _prompt_layer(xp, p, final)
        xs, kn, vn, sn, hn = _sample_layer(xs, state_shift[l], cache_k[l], cache_v[l], state_wkv[l], p, final)
        outs_p.append((kp, vp, sp, hp))
        outs_s.append((kn, vn, sn, hn))
    stack = lambda outs, i: jnp.stack([o[i] for o in outs])
    return (xp, xs,
            stack(outs_p, 0), stack(outs_p, 1), stack(outs_p, 2), stack(outs_p, 3),
            stack(outs_s, 0), stack(outs_s, 1), stack(outs_s, 2), stack(outs_s, 3))
```

```python
import functools
import math

import jax
import jax.numpy as jnp
from jax import lax
from jax.experimental import pallas as pl
from jax.experimental.pallas import tpu as pltpu

F32 = jnp.float32
BF16 = jnp.bfloat16

D_MODEL = 1024
HEAD_DIM = 64
ATT_HEADS = 8
KV_HEADS = 2
ATT_WIDTH = ATT_HEADS * HEAD_DIM
KV_WIDTH = KV_HEADS * HEAD_DIM
RWKV_HEADS = 8
RWKV_WIDTH = RWKV_HEADS * HEAD_DIM
WINDOW = 128
DECAY_LORA = 64
ICL_LORA = 64
GATE_LORA = 128
RWKV_BLOCK = 3 * RWKV_WIDTH + DECAY_LORA + ICL_LORA + GATE_LORA
ATT_PROJ = ATT_WIDTH + 2 * KV_WIDTH
PROJ_WIDTH = ATT_PROJ + RWKV_BLOCK
RMS_EPS = 1e-6
GN_EPS = 64e-5
L2_EPS = 1e-12
NEG_INF = -1e30

V7X_VMEM_BYTES = 64 * 1024 * 1024
VMEM_LIMIT_BYTES = V7X_VMEM_BYTES * 3 // 4

WKV_PROMPT_SEQS = 2
WKV_PROMPT_CHUNKS = 2
WKV_SAMPLE_SEQS = 8
WKV_PASSES = dict(gram=1, inverse=1, apply=1, state=3, carry=3)
BF16_ROWS = 16
ATTN_QUERY_BLOCKS = 4
POST_ROWS = 512
POST_SUB_ROWS = 256
INPROJ_SUB_ROWS = 256

NN = (((1,), (0,)), ((), ()))
NT = (((1,), (1,)), ((), ()))
TN = (((0,), (0,)), ((), ()))


def _dot(a, b, dn=NN):
    return lax.dot_general(a, b, dn, preferred_element_type=F32)


def _split2(x):
    hi = x.astype(BF16)
    lo = (x - hi.astype(F32)).astype(BF16)
    return hi, lo


def _split3(x):
    hi = x.astype(BF16)
    r1 = x - hi.astype(F32)
    mid = r1.astype(BF16)
    lo = (r1 - mid.astype(F32)).astype(BF16)
    return hi, mid, lo


def _dot3(xs, ys, dn=NN):
    xh, xl = xs
    yh, yl = ys
    return _dot(xh, yh, dn) + (_dot(xh, yl, dn) + _dot(xl, yh, dn))


def _cat3(x, axis, order):
    hi, lo = _split2(x)
    return jnp.concatenate([hi if ch == "h" else lo for ch in order], axis=axis)


def _seg_sum(x, ones_bd):
    hi, lo = _split2(x)
    pw = 2 * HEAD_DIM
    slabs = [slice(i, i + pw) for i in range(0, x.shape[1], pw)]
    return jnp.concatenate([_dot(hi[:, s], ones_bd) + _dot(lo[:, s], ones_bd) for s in slabs], axis=1)


def _rms(x, g):
    return x * lax.rsqrt(jnp.mean(x * x, axis=-1, keepdims=True) + RMS_EPS) * g


def _const_spec(shape):
    return pl.BlockSpec(shape, lambda *_: (0,) * len(shape))


def _params(*sem):
    return pltpu.CompilerParams(dimension_semantics=sem, vmem_limit_bytes=VMEM_LIMIT_BYTES)


def _rms_rows_kernel(x_ref, g_ref, o_ref):
    o_ref[...] = _rms(x_ref[...], g_ref[...])


def _rms_rows(x, g):
    n, d = x.shape
    return pl.pallas_call(
        _rms_rows_kernel,
        out_shape=jax.ShapeDtypeStruct((n, d), F32),
        name="rms_rows",
    )(x, g.reshape(1, d))


def _inproj_kernel(x_ref, g_ref, w_ref, q_ref, k_ref, v_ref, f_ref, *, normalize):
    x = x_ref[...]
    h = _rms(x, g_ref[...]) if normalize else x
    proj = _dot(h.astype(BF16), w_ref[...])
    q_ref[...] = proj[:, :ATT_WIDTH].astype(q_ref.dtype)
    k_ref[...] = proj[:, ATT_WIDTH:ATT_WIDTH + KV_WIDTH]
    v_ref[...] = proj[:, ATT_WIDTH + KV_WIDTH:ATT_PROJ]
    f_ref[...] = proj[:, ATT_PROJ:]


def _inproj(x, g_mix, w_in_bf, *, normalize, tm, q_dtype=BF16):
    n = x.shape[0]
    row = lambda w: pl.BlockSpec((tm, w), lambda i: (i, 0))
    return pl.pallas_call(
        functools.partial(_inproj_kernel, normalize=normalize),
        grid=(n // tm,),
        in_specs=[row(D_MODEL), _const_spec((1, D_MODEL)), _const_spec((D_MODEL, PROJ_WIDTH))],
        out_specs=[row(ATT_WIDTH), row(KV_WIDTH), row(KV_WIDTH), row(RWKV_BLOCK)],
        out_shape=[
            jax.ShapeDtypeStruct((n, ATT_WIDTH), q_dtype),
            jax.ShapeDtypeStruct((n, KV_WIDTH), F32),
            jax.ShapeDtypeStruct((n, KV_WIDTH), F32),
            jax.ShapeDtypeStruct((n, RWKV_BLOCK), F32),
        ],
        compiler_params=_params("arbitrary"),
        name="inproj",
    )(x, g_mix.reshape(1, D_MODEL), w_in_bf)


DECAY_SCALE = math.exp(-0.5)


def _sigmoid(z):
    return 1.0 / (1.0 + jnp.exp(-z))


N_PREP_PARAMS = 10
N_PREP_OUTS = 8


def _reset_carry(carry_ref, blocks_per_seq):
    @pl.when(pl.program_id(0) % blocks_per_seq == 0)
    def _():
        carry_ref[...] = jnp.zeros_like(carry_ref)


def _carried_prev(f, carry_ref):
    tm = f.shape[0]
    rolled = pltpu.roll(f, 1, 0)
    row = lax.broadcasted_iota(jnp.int32, f.shape, 0)
    prev = jnp.where(row == 0, carry_ref[0:1, :], rolled)
    carry_ref[0:1, :] = f[tm - 1:tm, :]
    return prev


def _prep_kernel(f_ref, p_ref, *refs, t):
    f = f_ref[...]
    tm = f.shape[0]
    before = jnp.broadcast_to(p_ref[...], (tm // t, t, f.shape[1])).reshape(tm, f.shape[1])
    row = lax.broadcasted_iota(jnp.int32, f.shape, 0)
    prev = jnp.where(row % t == 0, before, pltpu.roll(f, 1, 0))
    _prep_math(f, prev, refs[:N_PREP_PARAMS], refs[N_PREP_PARAMS:N_PREP_PARAMS + N_PREP_OUTS])


def _inproj_prep_kernel(x_ref, g_ref, w_ref, *refs, blocks_per_seq):
    q_ref, k_ref, v_ref = refs[N_PREP_PARAMS:N_PREP_PARAMS + 3]
    outs = refs[N_PREP_PARAMS + 3:N_PREP_PARAMS + 3 + N_PREP_OUTS]
    carry_ref = refs[N_PREP_PARAMS + 3 + N_PREP_OUTS]
    _reset_carry(carry_ref, blocks_per_seq)
    tm = x_ref.shape[0]
    sub = min(tm, INPROJ_SUB_ROWS)
    parts = [slice(i * sub, (i + 1) * sub) for i in range(tm // sub)]
    proj = [_dot(_rms(x_ref[rows, :], g_ref[...]).astype(BF16), w_ref[...]) for rows in parts]
    for rows, pj in zip(parts, proj):
        q_ref[rows, :] = pj[:, :ATT_WIDTH].astype(q_ref.dtype)
        k_ref[rows, :] = pj[:, ATT_WIDTH:ATT_WIDTH + KV_WIDTH]
        v_ref[rows, :] = pj[:, ATT_WIDTH + KV_WIDTH:ATT_PROJ]
    for rows, pj in zip(parts, proj):
        f = pj[:, ATT_PROJ:]
        _prep_math(f, _carried_prev(f, carry_ref), refs[:N_PREP_PARAMS], [o.at[rows, :] for o in outs])


def _prep_math(f, prev, params, outs):
    (mu_ref, w0_ref, w2_ref, a0_ref, a2_ref, g2_ref, kk_ref, ka_ref, rk_ref, ones_ref) = params
    (r_out, l_out, k_out, v_out, kk_out, kb_out, g_out, bo_out) = outs
    xs = f + (prev - f) * mu_ref[...]
    w3 = 3 * RWKV_WIDTH
    r = xs[:, :RWKV_WIDTH]
    k = xs[:, RWKV_WIDTH:2 * RWKV_WIDTH]
    v = xs[:, 2 * RWKV_WIDTH:w3]
    wl = xs[:, w3:w3 + DECAY_LORA]
    al = xs[:, w3 + DECAY_LORA:w3 + DECAY_LORA + ICL_LORA]
    gl = xs[:, w3 + DECAY_LORA + ICL_LORA:]
    log_decay = -DECAY_SCALE * _sigmoid(w0_ref[...] + _dot(jnp.tanh(wl).astype(BF16), w2_ref[...]))
    a = _sigmoid(a0_ref[...] + _dot(al.astype(BF16), a2_ref[...]))
    g = _dot(_sigmoid(gl).astype(BF16), g2_ref[...])
    ones_bd = ones_ref[...]
    kk = k * kk_ref[...]
    kk = kk * lax.rsqrt(_seg_sum(kk * kk, ones_bd) + L2_EPS)
    k2 = k * (1.0 + (a - 1.0) * ka_ref[...])
    r_out[...] = r
    l_out[...] = log_decay
    k_out[...] = k2
    v_out[...] = v
    kk_out[...] = kk
    kb_out[...] = kk * a
    g_out[...] = g
    bo_out[...] = _seg_sum(r * k2 * rk_ref[...], ones_bd) * v


def _prep_params(p):
    w = RWKV_WIDTH
    ins = [p["mu"], p["w0"], p["w2"], p["a0"], p["a2"], p["g2"], p["k_k"], p["k_a"], p["r_k"], p["ones_bd"]]
    specs = [_const_spec((1, RWKV_BLOCK)), _const_spec((1, w)), _const_spec((DECAY_LORA, w)),
             _const_spec((1, w)), _const_spec((ICL_LORA, w)), _const_spec((GATE_LORA, w)),
             _const_spec((1, w)), _const_spec((1, w)), _const_spec((1, w)),
             _const_spec((2 * HEAD_DIM, 2 * HEAD_DIM))]
    assert len(ins) == N_PREP_PARAMS
    return ins, specs


def _inproj_prep(x, p, *, tm, blocks_per_seq):
    n = x.shape[0]
    row = lambda w: pl.BlockSpec((tm, w), lambda i: (i, 0))
    w = RWKV_WIDTH
    prm, prm_specs = _prep_params(p)
    return pl.pallas_call(
        functools.partial(_inproj_prep_kernel, blocks_per_seq=blocks_per_seq),
        grid=(n // tm,),
        in_specs=[row(D_MODEL), _const_spec((1, D_MODEL)), _const_spec((D_MODEL, PROJ_WIDTH))] + prm_specs,
        out_specs=[row(ATT_WIDTH), row(KV_WIDTH), row(KV_WIDTH)] + [row(w)] * N_PREP_OUTS,
        out_shape=[jax.ShapeDtypeStruct((n, ATT_WIDTH), BF16), jax.ShapeDtypeStruct((n, KV_WIDTH), F32),
                   jax.ShapeDtypeStruct((n, KV_WIDTH), F32)] + [jax.ShapeDtypeStruct((n, w), F32)] * N_PREP_OUTS,
        scratch_shapes=[pltpu.VMEM((8, RWKV_BLOCK), F32)],
        compiler_params=_params("arbitrary"),
        name="inproj_prep",
    )(x, p["g_mix"].reshape(1, D_MODEL), p["w_in"], *prm)


def _prep(feat, feat_before, p, *, t, tm):
    n = feat.shape[0]
    assert tm % t == 0
    row = lambda w: pl.BlockSpec((tm, w), lambda i: (i, 0))
    w = RWKV_WIDTH
    prm, prm_specs = _prep_params(p)
    return pl.pallas_call(
        functools.partial(_prep_kernel, t=t),
        grid=(n // tm,),
        in_specs=[row(RWKV_BLOCK), pl.BlockSpec((tm // t, 1, RWKV_BLOCK), lambda i: (i, 0, 0))] + prm_specs,
        out_specs=[row(w)] * N_PREP_OUTS,
        out_shape=[jax.ShapeDtypeStruct((n, w), F32)] * N_PREP_OUTS,
        compiler_params=_params("arbitrary"),
        name="rwkv_prep",
    )(feat, feat_before.reshape(n // t, 1, RWKV_BLOCK), *prm)


def _kv_operands(kblk, vblk):
    lo_m = lax.broadcasted_iota(jnp.int32, (1, 2 * HEAD_DIM), 1) < HEAD_DIM
    k_sw = pltpu.roll(kblk, HEAD_DIM, 1)
    v_sw = pltpu.roll(vblk, HEAD_DIM, 1)
    ops = []
    for kv in range(KV_HEADS):
        own = lo_m if kv == 0 else ~lo_m
        kdup = jnp.where(own, kblk, k_sw).astype(BF16)
        v_own = jnp.where(own, vblk, 0.0).astype(BF16)
        v_oth = jnp.where(own, 0.0, v_sw).astype(BF16)
        ops.append((kdup,) + ((v_own, v_oth) if kv == 0 else (v_oth, v_own)))
    return ops


def _attend(q, kv_ops, mask, sink_ref, o_ref):
    nq = q.shape[0]
    lo_m = lax.broadcasted_iota(jnp.int32, (1, 2 * HEAD_DIM), 1) < HEAD_DIM
    mask2 = jnp.concatenate([mask, mask], axis=0)
    rowi = lax.broadcasted_iota(jnp.int32, (2 * nq, 1), 0)
    for kv in range(KV_HEADS):
        kdup, v_lo, v_hi = kv_ops[kv]
        for pr in range(2):
            c0 = (kv * 2 + pr) * 2 * HEAD_DIM
            qp = q[:, c0:c0 + 2 * HEAD_DIM].astype(F32) * (HEAD_DIM ** -0.5)
            qs = jnp.concatenate([jnp.where(lo_m, qp, 0.0), jnp.where(lo_m, 0.0, qp)], axis=0)
            s = jnp.where(mask2, _dot(qs.astype(BF16), kdup, NT), NEG_INF)
            h0 = (kv * 2 + pr) * 2
            sink = jnp.where(rowi < nq, sink_ref[h0], sink_ref[h0 + 1])
            m = jnp.maximum(jnp.max(s, axis=-1, keepdims=True), sink)
            e = jnp.exp(s - m)
            rden = 1.0 / (jnp.sum(e, axis=-1, keepdims=True) + jnp.exp(sink - m))
            out = _dot(e[:nq].astype(BF16), v_lo) + _dot(e[nq:].astype(BF16), v_hi)
            out = out * jnp.where(lo_m, rden[:nq], rden[nq:])
            o_ref[:, c0:c0 + 2 * HEAD_DIM] = out.astype(o_ref.dtype)


def _band_mask(nq, q0):
    qi = lax.broadcasted_iota(jnp.int32, (nq, 2 * WINDOW), 0) + q0
    kj = lax.broadcasted_iota(jnp.int32, (nq, 2 * WINDOW), 1)
    return (kj <= qi) & (qi - kj < WINDOW), kj


def _attn_prompt_kernel(sink_ref, q_ref, kp_ref, kc_ref, vp_ref, vc_ref, o_ref, *, qblocks):
    j = pl.program_id(1)
    band, kj = _band_mask(WINDOW, WINDOW)
    blk = lambda ref, i: ref[0, i * WINDOW:(i + 1) * WINDOW, :]
    ops = [_kv_operands(kp_ref[0], vp_ref[0])] + [_kv_operands(blk(kc_ref, i), blk(vc_ref, i))
                                                  for i in range(qblocks)]
    for i in range(qblocks):
        mask = band & ((j > 0) | (kj >= WINDOW)) if i == 0 else band
        kv_ops = [tuple(jnp.concatenate([ops[i][kv][n], ops[i + 1][kv][n]], axis=0) for n in range(3))
                  for kv in range(KV_HEADS)]
        rows = slice(i * WINDOW, (i + 1) * WINDOW)
        _attend(q_ref[0, rows, :], kv_ops, mask, sink_ref, o_ref.at[0, rows, :])


def _attn_prompt(q, k, v, sinks):
    b, t, _ = q.shape
    qblocks = _pick(t // WINDOW, ATTN_QUERY_BLOCKS)
    nb = t // (qblocks * WINDOW)
    cur = lambda w: pl.BlockSpec((1, qblocks * WINDOW, w), lambda bi, j: (bi, j, 0))
    prv = lambda w: pl.BlockSpec((1, WINDOW, w), lambda bi, j: (bi, jnp.maximum(j * qblocks - 1, 0), 0))
    return pl.pallas_call(
        functools.partial(_attn_prompt_kernel, qblocks=qblocks),
        grid=(b, nb),
        in_specs=[pl.BlockSpec(memory_space=pltpu.SMEM), cur(ATT_WIDTH), prv(KV_WIDTH), cur(KV_WIDTH),
                  prv(KV_WIDTH), cur(KV_WIDTH)],
        out_specs=cur(ATT_WIDTH),
        out_shape=jax.ShapeDtypeStruct((b, t, ATT_WIDTH), BF16),
        compiler_params=_params("arbitrary", "arbitrary"),
        name="attn_prompt",
    )(sinks, q, k, k, v, v)


def _attn_sample_kernel(sink_ref, q_ref, kn_ref, vn_ref, ck_ref, cv_ref, o_ref, kw_ref, vw_ref,
                        kall_ref, vall_ref, *, seqs, t):
    wb = ck_ref.shape[1]

    @pl.when(pl.program_id(0) == 0)
    def _():
        kall_ref[...] = jnp.zeros_like(kall_ref)
        vall_ref[...] = jnp.zeros_like(vall_ref)

    mask, _ = _band_mask(t, wb)
    for s in range(seqs):
        rows = slice(s * t, (s + 1) * t)
        kall_ref[0:wb, :] = ck_ref[s]
        vall_ref[0:wb, :] = cv_ref[s]
        kall_ref[wb:wb + t, :] = kn_ref[rows, :]
        vall_ref[wb:wb + t, :] = vn_ref[rows, :]
        _attend(q_ref[rows, :], _kv_operands(kall_ref[...], vall_ref[...]), mask, sink_ref, o_ref.at[rows, :])
        kw_ref[s] = kall_ref[t:t + wb, :]
        vw_ref[s] = vall_ref[t:t + wb, :]


def _attn_sample(q, k, v, cache_k, cache_v, sinks, *, t, seqs):
    n = q.shape[0]
    b, wb, _ = cache_k.shape
    row = lambda w: pl.BlockSpec((seqs * t, w), lambda i: (i, 0))
    win = pl.BlockSpec((seqs, wb, KV_WIDTH), lambda i: (i, 0, 0))
    return pl.pallas_call(
        functools.partial(_attn_sample_kernel, seqs=seqs, t=t),
        grid=(b // seqs,),
        in_specs=[pl.BlockSpec(memory_space=pltpu.SMEM), row(ATT_WIDTH), row(KV_WIDTH), row(KV_WIDTH), win, win],
        out_specs=[row(ATT_WIDTH), win, win],
        out_shape=[jax.ShapeDtypeStruct((n, ATT_WIDTH), F32),
                   jax.ShapeDtypeStruct((b, wb, KV_WIDTH), F32),
                   jax.ShapeDtypeStruct((b, wb, KV_WIDTH), F32)],
        scratch_shapes=[pltpu.VMEM((2 * WINDOW, KV_WIDTH), F32), pltpu.VMEM((2 * WINDOW, KV_WIDTH), F32)],
        compiler_params=_params("arbitrary"),
        name="attn_sample",
    )(sinks, q, k, v, cache_k, cache_v)


def _wkv_kernel(r_ref, l_ref, k_ref, v_ref, kk_ref, kb_ref, s0_ref, o_ref, st_ref, z_ref, *, nseq, nchunks, rows,
                real):
    c = rows
    pw = 2 * HEAD_DIM
    j = pl.program_id(1)
    npairs = RWKV_HEADS // 2
    pairs = range(npairs)

    @pl.when(j == 0)
    def _():
        for s in range(nseq):
            for p in pairs:
                z_ref[s * npairs + p] = jnp.concatenate([s0_ref[s, 2 * p], s0_ref[s, 2 * p + 1]], axis=0).T

    def pair_iota(nrow):
        return (lax.broadcasted_iota(jnp.int32, (nrow, pw), 0),
                lax.broadcasted_iota(jnp.int32, (nrow, pw), 1) & (HEAD_DIM - 1))

    row, col = pair_iota(c)
    lo = lax.broadcasted_iota(jnp.int32, (1, pw), 1) < HEAD_DIM
    lower2 = jnp.concatenate([row > col, row >= col], axis=0)
    eye = (row == col).astype(F32)
    krow, kcol = pair_iota(HEAD_DIM)
    eye_ch = (krow == kcol).astype(F32)
    ri = lax.broadcasted_iota(jnp.int32, (c, c), 0)
    tri = (ri >= lax.broadcasted_iota(jnp.int32, (c, c), 1)).astype(BF16)

    def bd(y):
        zero = jnp.zeros_like(y)
        parts = [jnp.where(lo, y, zero), jnp.where(lo, zero, y)]
        if y.shape[0] < HEAD_DIM:
            fill = jnp.zeros((HEAD_DIM - y.shape[0], pw), y.dtype)
            parts = [parts[0], fill, parts[1], fill]
        return jnp.concatenate(parts, axis=0)

    memo = {}

    def cached(x, tag, build):
        key = (id(x), tag)
        if key not in memo:
            memo[key] = (x, build())
        return memo[key][1]

    def split(x):
        return cached(x, "split", lambda: _split2(x))

    def lhs(x, passes, axis=1):
        hi, low = split(x)
        return cached(x, ("lhs", passes, axis), lambda: jnp.concatenate([hi, low, hi][:passes], axis=axis))

    def rhs(y, passes, axis=0):
        hi, low = split(y)

        def build():
            bh = bd(hi)
            return jnp.concatenate([bh, bh, bd(low)][:passes], axis=axis)
        return cached(y, ("rhs", passes, axis), build)

    def mm(x, y, group):
        return _dot(lhs(x, WKV_PASSES[group]), rhs(y, WKV_PASSES[group]))

    def mm_nt(x, y, group):
        return _dot(lhs(x, WKV_PASSES[group]), rhs(y, WKV_PASSES[group], 1), NT)

    def mm_tn(x, y, group):
        passes = WKV_PASSES[group]
        yh, yl = split(y)
        full = _dot(lhs(x, passes, 0), jnp.concatenate([yh, yh, yl][:passes], axis=0), TN)
        return jnp.where(lo, full[:HEAD_DIM], full[HEAD_DIM:])

    prep = {}
    for s in range(nseq):
        for q in range(nchunks):
            tr = slice(q * real, (q + 1) * real)

            def tok(ref):
                x = ref[s, tr, :]
                return x if real == c else jnp.concatenate([x, jnp.zeros((c - real, x.shape[1]), x.dtype)], axis=0)

            lw = tok(l_ref)
            cum = sum(_dot(tri, part) for part in _split3(lw))
            tot = cum[c - 1:c, :]
            en = jnp.exp(-cum)
            ed = jnp.exp(tot - cum)
            kk = tok(kk_ref)
            kb = tok(kb_ref)
            kx = tok(k_ref)
            prep[s, q] = dict(a=-kk * jnp.exp(cum - lw), r=tok(r_ref) * jnp.exp(cum), b=kb * en, k=kx * en,
                              bh=kb * ed, kh=kx * ed, v=tok(v_ref), etot=jnp.exp(tot))

    items = [(s, q, p) for s in range(nseq) for q in range(nchunks) for p in pairs]
    ps = lambda name, it: prep[it[0], it[1]][name][:, it[2] * pw:(it[2] + 1) * pw]
    each = lambda f: [f(i) for i in range(len(items))]
    pv = {name: each(lambda i: ps(name, items[i])) for name in ("a", "r", "b", "k", "bh", "kh", "v", "etot")}
    ar = each(lambda i: jnp.concatenate([pv["a"][i], pv["r"][i]], axis=0))
    g_b = each(lambda i: jnp.where(lower2, mm_nt(ar[i], pv["b"][i], "gram"), 0.0))
    g_k = each(lambda i: jnp.where(lower2, mm_nt(ar[i], pv["k"][i], "gram"), 0.0))
    a_rb = each(lambda i: g_b[i][c:])
    p_m = each(lambda i: g_b[i][:c])
    t_m = each(lambda i: eye + p_m[i])
    n = 2
    while n < c:
        p_m = each(lambda i: mm(p_m[i], p_m[i], "inverse"))
        t_m = each(lambda i: t_m[i] + mm(p_m[i], t_m[i], "inverse"))
        n *= 2
    gv = each(lambda i: mm(g_k[i], pv["v"][i], "apply"))
    av = each(lambda i: gv[i][:c])
    w_m = each(lambda i: mm(t_m[i], pv["a"][i], "apply"))
    u_m = each(lambda i: mm(t_m[i], av[i], "apply"))
    r_p = each(lambda i: pv["r"][i] + mm(a_rb[i], w_m[i], "apply"))
    o_i = each(lambda i: mm(a_rb[i], u_m[i], "apply") + gv[i][c:])
    m_m = each(lambda i: mm_tn(pv["bh"][i], w_m[i], "state") + eye_ch * pv["etot"][i])
    bk = each(lambda i: jnp.concatenate([pv["bh"][i], pv["kh"][i]], axis=0))
    uv = each(lambda i: jnp.concatenate([u_m[i], pv["v"][i]], axis=0))
    n_m = each(lambda i: mm_tn(bk[i], uv[i], "state"))
    rm = each(lambda i: jnp.concatenate([r_p[i], m_m[i]], axis=0))
    z = [z_ref[sp] for sp in range(nseq * npairs)]
    for q in range(nchunks):
        for s in range(nseq):
            for p in pairs:
                i = items.index((s, q, p))
                sp = s * npairs + p
                oz = mm(rm[i], z[sp], "carry")
                o_ref[s, q * real:(q + 1) * real, p * pw:(p + 1) * pw] = (oz[:c] + o_i[i])[:real]
                z[sp] = oz[c:] + n_m[i]
    for sp in range(nseq * npairs):
        z_ref[sp] = z[sp]

    @pl.when(j == pl.num_programs(1) - 1)
    def _():
        for s in range(nseq):
            for p in pairs:
                zt = z_ref[s * npairs + p].T
                st_ref[s, 2 * p] = zt[:HEAD_DIM]
                st_ref[s, 2 * p + 1] = zt[HEAD_DIM:]


def _wkv(r, lw, k, v, kk, kb, s0, *, nseq, nchunks, rows, real=None):
    real = rows if real is None else real
    assert real == rows or nchunks == 1
    b, t, w = r.shape
    blk = nchunks * real
    tok = pl.BlockSpec((nseq, blk, w), lambda bi, j: (bi, j, 0))
    st = pl.BlockSpec((nseq, RWKV_HEADS, HEAD_DIM, HEAD_DIM), lambda bi, j: (bi, 0, 0, 0))
    return pl.pallas_call(
        functools.partial(_wkv_kernel, nseq=nseq, nchunks=nchunks, rows=rows, real=real),
        grid=(b // nseq, t // blk),
        in_specs=[tok] * 6 + [st],
        out_specs=[tok, st],
        out_shape=[jax.ShapeDtypeStruct((b, t, w), F32),
                   jax.ShapeDtypeStruct((b, RWKV_HEADS, HEAD_DIM, HEAD_DIM), F32)],
        scratch_shapes=[pltpu.VMEM((nseq * RWKV_HEADS // 2, HEAD_DIM, 2 * HEAD_DIM), F32)],
        compiler_params=_params("arbitrary", "arbitrary"),
        name="wkv_scan",
    )(r, lw, k, v, kk, kb, s0)


def _post_kernel(o_ref, bo_ref, g_ref, att_ref, x_ref, gng_ref, gnb_ref, ones_ref, wo_ref, gf_ref,
                 wg_ref, wu_ref, wd_ref, gfin_ref, y_ref, *, final):
    tm = x_ref.shape[0]
    sub = min(tm, POST_SUB_ROWS)
    parts = [slice(i * sub, (i + 1) * sub) for i in range(tm // sub)]
    each = lambda f: [f(i) for i in range(len(parts))]
    ones_bd = ones_ref[...]
    o = each(lambda i: o_ref[parts[i], :])
    mean = each(lambda i: _seg_sum(o[i], ones_bd) * (1.0 / HEAD_DIM))
    oc = each(lambda i: o[i] - mean[i])
    var = each(lambda i: _seg_sum(oc[i] * oc[i], ones_bd) * (1.0 / HEAD_DIM))
    rw = each(lambda i: ((oc[i] * lax.rsqrt(var[i] + GN_EPS) * gng_ref[...] + gnb_ref[...] + bo_ref[parts[i], :])
                         * g_ref[parts[i], :]).astype(BF16))
    x1 = each(lambda i: x_ref[parts[i], :] + _dot(att_ref[parts[i], :].astype(BF16), wo_ref[:ATT_WIDTH, :])
              + _dot(rw[i], wo_ref[ATT_WIDTH:, :]))
    u = each(lambda i: _rms(x1[i], gf_ref[...]).astype(BF16))
    gate = each(lambda i: _dot(u[i], wg_ref[...]))
    up = each(lambda i: _dot(u[i], wu_ref[...]))
    hid = each(lambda i: (gate[i] * _sigmoid(gate[i]) * up[i]).astype(BF16))
    x2 = each(lambda i: x1[i] + _dot(hid[i], wd_ref[...]))
    for i in range(len(parts)):
        y_ref[parts[i], :] = _rms(x2[i], gfin_ref[...]) if final else x2[i]


def _post(o, bonus, g, att, x, p, *, tm, final):
    n = x.shape[0]
    d_ff = p["w_gate"].shape[1]
    row = lambda w: pl.BlockSpec((tm, w), lambda i: (i, 0))
    once = lambda shape: pl.BlockSpec(shape, lambda i: (0,) * len(shape), pipeline_mode=pl.Buffered(1))
    w = RWKV_WIDTH
    return pl.pallas_call(
        functools.partial(_post_kernel, final=final),
        grid=(n // tm,),
        in_specs=[row(w), row(w), row(w), row(ATT_WIDTH), row(D_MODEL),
                  once((1, w)), once((1, w)), once((2 * HEAD_DIM, 2 * HEAD_DIM)), once((ATT_WIDTH + w, D_MODEL)),
                  once((1, D_MODEL)),
                  once((D_MODEL, d_ff)), once((D_MODEL, d_ff)), once((d_ff, D_MODEL)), once((1, D_MODEL))],
        out_specs=row(D_MODEL),
        out_shape=jax.ShapeDtypeStruct((n, D_MODEL), F32),
        compiler_params=_params("arbitrary"),
        name="post_ffn",
    )(o, bonus, g, att, x, p["gn_g"], p["gn_b"], p["ones_bd"], p["w_out"], p["g_ffn"],
      p["w_gate"], p["w_up"], p["w_down"], p["g_final"])


def _layer_params(l, g_mix, w_in, attn_sinks, rwkv_mu, w0, w2, a0, a2, g2, k_k, k_a, r_k, gn_g, gn_b,
                  w_out, g_ffn, w_gate, w_up, w_down, g_final):
    vec = lambda a: a.reshape(1, -1).astype(F32)
    hd = jnp.arange(2 * HEAD_DIM) // HEAD_DIM
    return dict(
        g_mix=g_mix[l], w_in=w_in[l].astype(BF16), sinks=attn_sinks[l].astype(F32),
        mu=vec(rwkv_mu[l]), w0=vec(w0[l]), w2=w2[l].astype(BF16), a0=vec(a0[l]), a2=a2[l].astype(BF16),
        g2=g2[l].astype(BF16), k_k=vec(k_k[l]), k_a=vec(k_a[l]), r_k=vec(r_k[l]),
        gn_g=vec(gn_g[l]), gn_b=vec(gn_b[l]), w_out=w_out[l].astype(BF16), g_ffn=vec(g_ffn[l]),
        w_gate=w_gate[l].astype(BF16), w_up=w_up[l].astype(BF16), w_down=w_down[l].astype(BF16),
        g_final=vec(g_final), ones_bd=(hd[:, None] == hd[None, :]).astype(BF16),
    )


def _pick(n, pref):
    t = pref
    while n % t:
        t //= 2
    return t


def _prompt_layer(x, p, final):
    b, t, d = x.shape
    n = b * t
    x2 = x.reshape(n, d)
    tm = _pick(t, 512)
    q, k, v, r, lw, k2, vv, kk, kb, g, bonus = _inproj_prep(x2, p, tm=tm, blocks_per_seq=t // tm)
    att = _attn_prompt(q.reshape(b, t, -1), k.reshape(b, t, -1), v.reshape(b, t, -1), p["sinks"])
    s0 = jnp.zeros((b, RWKV_HEADS, HEAD_DIM, HEAD_DIM), F32)
    sh = lambda a: a.reshape(b, t, RWKV_WIDTH)
    o, s_new = _wkv(sh(r), sh(lw), sh(k2), sh(vv), sh(kk), sh(kb), s0, nseq=_pick(b, WKV_PROMPT_SEQS),
                    nchunks=_pick(t // HEAD_DIM, WKV_PROMPT_CHUNKS), rows=HEAD_DIM)
    y = _post(o.reshape(n, -1), bonus, g, att.reshape(n, -1), x2, p, tm=_pick(n, POST_ROWS), final=final)
    wp = min(WINDOW, t)
    k_win = k.reshape(b, t, KV_WIDTH)[:, t - wp:].reshape(b, wp, KV_HEADS, HEAD_DIM)
    v_win = v.reshape(b, t, KV_WIDTH)[:, t - wp:].reshape(b, wp, KV_HEADS, HEAD_DIM)
    shift = _rms_rows(x[:, -1], p["g_mix"])
    return y.reshape(b, t, d), k_win, v_win, s_new, shift


def _sample_layer(x, h_prev, k_buf, v_buf, s0, p, final):
    b, t, d = x.shape
    n = b * t
    x2 = x.reshape(n, d)
    wb = k_buf.shape[1]
    q, k, v, feat = _inproj(x2, p["g_mix"], p["w_in"], normalize=True, tm=_pick(n, 512), q_dtype=F32)
    _, _, _, feat_prev = _inproj(h_prev, p["g_mix"], p["w_in"], normalize=False, tm=_pick(b, 512))
    r, lw, k2, vv, kk, kb, g, bonus = _prep(feat, feat_prev, p, t=t, tm=_pick(n, 512))
    att, k_win, v_win = _attn_sample(q, k, v, k_buf.reshape(b, wb, -1), v_buf.reshape(b, wb, -1),
                                     p["sinks"], t=t, seqs=_pick(b, 8))
    tp = -(-t // BF16_ROWS) * BF16_ROWS
    assert tp <= HEAD_DIM and t % 8 == 0
    sh = lambda a: a.reshape(b, t, RWKV_WIDTH)
    o, s_new = _wkv(sh(r), sh(lw), sh(k2), sh(vv), sh(kk), sh(kb), s0, nseq=_pick(b, WKV_SAMPLE_SEQS),
                    nchunks=1, rows=tp, real=t)
    y = _post(o.reshape(n, -1), bonus, g, att, x2, p, tm=_pick(n, POST_ROWS), final=final)
    shift = _rms_rows(x[:, -1], p["g_mix"])
    return (y.reshape(b, t, d), k_win.reshape(b, wb, KV_HEADS, HEAD_DIM), v_win.reshape(b, wb, KV_HEADS, HEAD_DIM),
            s_new, shift)


def kernel(x_prompt, x_sample, cache_k, cache_v, state_wkv, state_shift, g_mix, w_in, attn_sinks, rwkv_mu, w0, w2,
           a0, a2, g2, k_k, k_a, r_k, gn_g, gn_b, w_out, g_ffn, w_gate, w_up, w_down, g_final):
    depth = w_in.shape[0]
    xp, xs = x_prompt, x_sample
    outs_p, outs_s = [], []
    for l in range(depth):
        p = _layer_params(l, g_mix, w_in, attn_sinks, rwkv_mu, w0, w2, a0, a2, g2, k_k, k_a, r_k, gn_g, gn_b,
                          w_out, g_ffn, w_gate, w_up, w_down, g_final)
        final = l == depth - 1
        xp, kp, vp, sp, hp = _prompt_layer(xp, p, final)
        xs, kn, vn, sn, hn = _sample_layer(xs, state_shift[l], cache_k[l], cache_v[l], state_wkv[l], p, final)
        outs_p.append((kp, vp, sp, hp))
        outs_s.append((kn, vn, sn, hn))
    stack = lambda outs, i: jnp.stack([o[i] for o in outs])
    return (xp, xs,
            stack(outs_p, 0), stack(outs_p, 1), stack(outs_p, 2), stack(outs_p, 3),
            stack(outs_s, 0), stack(outs_s, 1), stack(outs_s, 2), stack(outs_s, 3))
```

```python
import functools
import math

import jax
import jax.numpy as jnp
from jax import lax
from jax.experimental import pallas as pl
from jax.experimental.pallas import tpu as pltpu

F32 = jnp.float32
BF16 = jnp.bfloat16

D_MODEL = 1024
HEAD_DIM = 64
ATT_HEADS = 8
KV_HEADS = 2
ATT_WIDTH = ATT_HEADS * HEAD_DIM
KV_WIDTH = KV_HEADS * HEAD_DIM
RWKV_HEADS = 8
RWKV_WIDTH = RWKV_HEADS * HEAD_DIM
WINDOW = 128
DECAY_LORA = 64
ICL_LORA = 64
GATE_LORA = 128
RWKV_BLOCK = 3 * RWKV_WIDTH + DECAY_LORA + ICL_LORA + GATE_LORA
ATT_PROJ = ATT_WIDTH + 2 * KV_WIDTH
PROJ_WIDTH = ATT_PROJ + RWKV_BLOCK
RMS_EPS = 1e-6
GN_EPS = 64e-5
L2_EPS = 1e-12
NEG_INF = -1e30

V7X_VMEM_BYTES = 64 * 1024 * 1024
VMEM_LIMIT_BYTES = V7X_VMEM_BYTES * 3 // 4

WKV_PROMPT_SEQS = 2
WKV_PROMPT_CHUNKS = 2
WKV_SAMPLE_SEQS = 8
WKV_PASSES = dict(gram=1, inverse=1, apply=1, state=3, carry=3)
BF16_ROWS = 16
POST_ROWS = 512
POST_SUB_ROWS = 256
INPROJ_SUB_ROWS = 256

NN = (((1,), (0,)), ((), ()))
NT = (((1,), (1,)), ((), ()))
TN = (((0,), (0,)), ((), ()))


def _dot(a, b, dn=NN):
    return lax.dot_general(a, b, dn, preferred_element_type=F32)


def _split2(x):
    hi = x.astype(BF16)
    lo = (x - hi.astype(F32)).astype(BF16)
    return hi, lo


def _split3(x):
    hi = x.astype(BF16)
    r1 = x - hi.astype(F32)
    mid = r1.astype(BF16)
    lo = (r1 - mid.astype(F32)).astype(BF16)
    return hi, mid, lo


def _dot3(xs, ys, dn=NN):
    xh, xl = xs
    yh, yl = ys
    return _dot(xh, yh, dn) + (_dot(xh, yl, dn) + _dot(xl, yh, dn))


def _cat3(x, axis, order):
    hi, lo = _split2(x)
    return jnp.concatenate([hi if ch == "h" else lo for ch in order], axis=axis)


def _seg_sum(x, ones_bd):
    hi, lo = _split2(x)
    pw = 2 * HEAD_DIM
    slabs = [slice(i, i + pw) for i in range(0, x.shape[1], pw)]
    return jnp.concatenate([_dot(hi[:, s], ones_bd) + _dot(lo[:, s], ones_bd) for s in slabs], axis=1)


def _rms(x, g):
    return x * lax.rsqrt(jnp.mean(x * x, axis=-1, keepdims=True) + RMS_EPS) * g


def _const_spec(shape):
    return pl.BlockSpec(shape, lambda *_: (0,) * len(shape))


def _params(*sem):
    return pltpu.CompilerParams(dimension_semantics=sem, vmem_limit_bytes=VMEM_LIMIT_BYTES)


def _rms_rows_kernel(x_ref, g_ref, o_ref):
    o_ref[...] = _rms(x_ref[...], g_ref[...])


def _rms_rows(x, g):
    n, d = x.shape
    return pl.pallas_call(
        _rms_rows_kernel,
        out_shape=jax.ShapeDtypeStruct((n, d), F32),
        name="rms_rows",
    )(x, g.reshape(1, d))


def _inproj_kernel(x_ref, g_ref, w_ref, q_ref, k_ref, v_ref, f_ref, *, normalize):
    x = x_ref[...]
    h = _rms(x, g_ref[...]) if normalize else x
    proj = _dot(h.astype(BF16), w_ref[...])
    q_ref[...] = proj[:, :ATT_WIDTH].astype(q_ref.dtype)
    k_ref[...] = proj[:, ATT_WIDTH:ATT_WIDTH + KV_WIDTH]
    v_ref[...] = proj[:, ATT_WIDTH + KV_WIDTH:ATT_PROJ]
    f_ref[...] = proj[:, ATT_PROJ:]


def _inproj(x, g_mix, w_in_bf, *, normalize, tm, q_dtype=BF16):
    n = x.shape[0]
    row = lambda w: pl.BlockSpec((tm, w), lambda i: (i, 0))
    return pl.pallas_call(
        functools.partial(_inproj_kernel, normalize=normalize),
        grid=(n // tm,),
        in_specs=[row(D_MODEL), _const_spec((1, D_MODEL)), _const_spec((D_MODEL, PROJ_WIDTH))],
        out_specs=[row(ATT_WIDTH), row(KV_WIDTH), row(KV_WIDTH), row(RWKV_BLOCK)],
        out_shape=[
            jax.ShapeDtypeStruct((n, ATT_WIDTH), q_dtype),
            jax.ShapeDtypeStruct((n, KV_WIDTH), F32),
            jax.ShapeDtypeStruct((n, KV_WIDTH), F32),
            jax.ShapeDtypeStruct((n, RWKV_BLOCK), F32),
        ],
        compiler_params=_params("arbitrary"),
        name="inproj",
    )(x, g_mix.reshape(1, D_MODEL), w_in_bf)


DECAY_SCALE = math.exp(-0.5)


def _sigmoid(z):
    return 1.0 / (1.0 + jnp.exp(-z))


N_PREP_PARAMS = 10
N_PREP_OUTS = 8


def _carried_prev(f, carry_ref):
    tm = f.shape[0]
    rolled = pltpu.roll(f, 1, 0)
    row = lax.broadcasted_iota(jnp.int32, f.shape, 0)
    prev = jnp.where(row == 0, carry_ref[0:1, :], rolled)
    carry_ref[0:1, :] = f[tm - 1:tm, :]
    return prev


def _prep_kernel(f_ref, p_ref, *refs, t):
    f = f_ref[...]
    tm = f.shape[0]
    before = jnp.broadcast_to(p_ref[...], (tm // t, t, f.shape[1])).reshape(tm, f.shape[1])
    row = lax.broadcasted_iota(jnp.int32, f.shape, 0)
    prev = jnp.where(row % t == 0, before, pltpu.roll(f, 1, 0))
    _prep_math(f, prev, refs[:N_PREP_PARAMS], refs[N_PREP_PARAMS:N_PREP_PARAMS + N_PREP_OUTS])


def _inproj_prep_kernel(sink_ref, x_ref, g_ref, w_ref, *refs, blocks_per_seq):
    att_ref, k_ref, v_ref = refs[N_PREP_PARAMS:N_PREP_PARAMS + 3]
    outs = refs[N_PREP_PARAMS + 3:N_PREP_PARAMS + 3 + N_PREP_OUTS]
    carry_ref, kprev_ref, vprev_ref = refs[N_PREP_PARAMS + 3 + N_PREP_OUTS:]
    first = pl.program_id(0) % blocks_per_seq == 0

    @pl.when(first)
    def _():
        carry_ref[...] = jnp.zeros_like(carry_ref)
        kprev_ref[...] = jnp.zeros_like(kprev_ref)
        vprev_ref[...] = jnp.zeros_like(vprev_ref)

    tm = x_ref.shape[0]
    sub = min(tm, INPROJ_SUB_ROWS)
    parts = [slice(i * sub, (i + 1) * sub) for i in range(tm // sub)]
    proj = [_dot(_rms(x_ref[rows, :], g_ref[...]).astype(BF16), w_ref[...]) for rows in parts]
    for rows, pj in zip(parts, proj):
        k_ref[rows, :] = pj[:, ATT_WIDTH:ATT_WIDTH + KV_WIDTH]
        v_ref[rows, :] = pj[:, ATT_WIDTH + KV_WIDTH:ATT_PROJ]
    for rows, pj in zip(parts, proj):
        f = pj[:, ATT_PROJ:]
        _prep_math(f, _carried_prev(f, carry_ref), refs[:N_PREP_PARAMS], [o.at[rows, :] for o in outs])

    def blk(i, c0, c1):
        r0 = i * WINDOW
        return proj[r0 // sub][r0 % sub:r0 % sub + WINDOW, c0:c1]

    nblk = tm // WINDOW
    kcol, vcol = (ATT_WIDTH, ATT_WIDTH + KV_WIDTH), (ATT_WIDTH + KV_WIDTH, ATT_PROJ)
    ops = [_kv_operands(kprev_ref[...], vprev_ref[...])] + [_kv_operands(blk(i, *kcol), blk(i, *vcol))
                                                           for i in range(nblk)]
    band, kj = _band_mask(WINDOW, WINDOW)
    for i in range(nblk):
        mask = band & (jnp.logical_not(first) | (kj >= WINDOW)) if i == 0 else band
        kv_ops = [tuple(jnp.concatenate([ops[i][kv][n], ops[i + 1][kv][n]], axis=0) for n in range(3))
                  for kv in range(KV_HEADS)]
        _attend(blk(i, 0, ATT_WIDTH).astype(BF16), kv_ops, mask, sink_ref,
                att_ref.at[i * WINDOW:(i + 1) * WINDOW, :])
    kprev_ref[...] = blk(nblk - 1, *kcol)
    vprev_ref[...] = blk(nblk - 1, *vcol)


def _prep_math(f, prev, params, outs):
    (mu_ref, w0_ref, w2_ref, a0_ref, a2_ref, g2_ref, kk_ref, ka_ref, rk_ref, ones_ref) = params
    (r_out, l_out, k_out, v_out, kk_out, kb_out, g_out, bo_out) = outs
    xs = f + (prev - f) * mu_ref[...]
    w3 = 3 * RWKV_WIDTH
    r = xs[:, :RWKV_WIDTH]
    k = xs[:, RWKV_WIDTH:2 * RWKV_WIDTH]
    v = xs[:, 2 * RWKV_WIDTH:w3]
    wl = xs[:, w3:w3 + DECAY_LORA]
    al = xs[:, w3 + DECAY_LORA:w3 + DECAY_LORA + ICL_LORA]
    gl = xs[:, w3 + DECAY_LORA + ICL_LORA:]
    log_decay = -DECAY_SCALE * _sigmoid(w0_ref[...] + _dot(jnp.tanh(wl).astype(BF16), w2_ref[...]))
    a = _sigmoid(a0_ref[...] + _dot(al.astype(BF16), a2_ref[...]))
    g = _dot(_sigmoid(gl).astype(BF16), g2_ref[...])
    ones_bd = ones_ref[...]
    kk = k * kk_ref[...]
    kk = kk * lax.rsqrt(_seg_sum(kk * kk, ones_bd) + L2_EPS)
    k2 = k * (1.0 + (a - 1.0) * ka_ref[...])
    r_out[...] = r
    l_out[...] = log_decay
    k_out[...] = k2
    v_out[...] = v
    kk_out[...] = kk
    kb_out[...] = kk * a
    g_out[...] = g
    bo_out[...] = _seg_sum(r * k2 * rk_ref[...], ones_bd) * v


def _prep_params(p):
    w = RWKV_WIDTH
    ins = [p["mu"], p["w0"], p["w2"], p["a0"], p["a2"], p["g2"], p["k_k"], p["k_a"], p["r_k"], p["ones_bd"]]
    specs = [_const_spec((1, RWKV_BLOCK)), _const_spec((1, w)), _const_spec((DECAY_LORA, w)),
             _const_spec((1, w)), _const_spec((ICL_LORA, w)), _const_spec((GATE_LORA, w)),
             _const_spec((1, w)), _const_spec((1, w)), _const_spec((1, w)),
             _const_spec((2 * HEAD_DIM, 2 * HEAD_DIM))]
    assert len(ins) == N_PREP_PARAMS
    return ins, specs


def _inproj_prep(x, p, *, tm, blocks_per_seq):
    n = x.shape[0]
    assert tm % WINDOW == 0
    row = lambda w: pl.BlockSpec((tm, w), lambda i: (i, 0))
    w = RWKV_WIDTH
    prm, prm_specs = _prep_params(p)
    return pl.pallas_call(
        functools.partial(_inproj_prep_kernel, blocks_per_seq=blocks_per_seq),
        grid=(n // tm,),
        in_specs=[pl.BlockSpec(memory_space=pltpu.SMEM), row(D_MODEL), _const_spec((1, D_MODEL)),
                  _const_spec((D_MODEL, PROJ_WIDTH))] + prm_specs,
        out_specs=[row(ATT_WIDTH), row(KV_WIDTH), row(KV_WIDTH)] + [row(w)] * N_PREP_OUTS,
        out_shape=[jax.ShapeDtypeStruct((n, ATT_WIDTH), BF16), jax.ShapeDtypeStruct((n, KV_WIDTH), F32),
                   jax.ShapeDtypeStruct((n, KV_WIDTH), F32)] + [jax.ShapeDtypeStruct((n, w), F32)] * N_PREP_OUTS,
        scratch_shapes=[pltpu.VMEM((8, RWKV_BLOCK), F32), pltpu.VMEM((WINDOW, KV_WIDTH), F32),
                        pltpu.VMEM((WINDOW, KV_WIDTH), F32)],
        compiler_params=_params("arbitrary"),
        name="inproj_prep_attn",
    )(p["sinks"], x, p["g_mix"].reshape(1, D_MODEL), p["w_in"], *prm)


def _prep(feat, feat_before, p, *, t, tm):
    n = feat.shape[0]
    assert tm % t == 0
    row = lambda w: pl.BlockSpec((tm, w), lambda i: (i, 0))
    w = RWKV_WIDTH
    prm, prm_specs = _prep_params(p)
    return pl.pallas_call(
        functools.partial(_prep_kernel, t=t),
        grid=(n // tm,),
        in_specs=[row(RWKV_BLOCK), pl.BlockSpec((tm // t, 1, RWKV_BLOCK), lambda i: (i, 0, 0))] + prm_specs,
        out_specs=[row(w)] * N_PREP_OUTS,
        out_shape=[jax.ShapeDtypeStruct((n, w), F32)] * N_PREP_OUTS,
        compiler_params=_params("arbitrary"),
        name="rwkv_prep",
    )(feat, feat_before.reshape(n // t, 1, RWKV_BLOCK), *prm)


def _kv_operands(kblk, vblk):
    lo_m = lax.broadcasted_iota(jnp.int32, (1, 2 * HEAD_DIM), 1) < HEAD_DIM
    k_sw = pltpu.roll(kblk, HEAD_DIM, 1)
    v_sw = pltpu.roll(vblk, HEAD_DIM, 1)
    ops = []
    for kv in range(KV_HEADS):
        own = lo_m if kv == 0 else ~lo_m
        kdup = jnp.where(own, kblk, k_sw).astype(BF16)
        v_own = jnp.where(own, vblk, 0.0).astype(BF16)
        v_oth = jnp.where(own, 0.0, v_sw).astype(BF16)
        ops.append((kdup,) + ((v_own, v_oth) if kv == 0 else (v_oth, v_own)))
    return ops


def _attend(q, kv_ops, mask, sink_ref, o_ref):
    nq = q.shape[0]
    lo_m = lax.broadcasted_iota(jnp.int32, (1, 2 * HEAD_DIM), 1) < HEAD_DIM
    mask2 = jnp.concatenate([mask, mask], axis=0)
    rowi = lax.broadcasted_iota(jnp.int32, (2 * nq, 1), 0)
    for kv in range(KV_HEADS):
        kdup, v_lo, v_hi = kv_ops[kv]
        for pr in range(2):
            c0 = (kv * 2 + pr) * 2 * HEAD_DIM
            qp = q[:, c0:c0 + 2 * HEAD_DIM].astype(F32) * (HEAD_DIM ** -0.5)
            qs = jnp.concatenate([jnp.where(lo_m, qp, 0.0), jnp.where(lo_m, 0.0, qp)], axis=0)
            s = jnp.where(mask2, _dot(qs.astype(BF16), kdup, NT), NEG_INF)
            h0 = (kv * 2 + pr) * 2
            sink = jnp.where(rowi < nq, sink_ref[h0], sink_ref[h0 + 1])
            m = jnp.maximum(jnp.max(s, axis=-1, keepdims=True), sink)
            e = jnp.exp(s - m)
            rden = 1.0 / (jnp.sum(e, axis=-1, keepdims=True) + jnp.exp(sink - m))
            out = _dot(e[:nq].astype(BF16), v_lo) + _dot(e[nq:].astype(BF16), v_hi)
            out = out * jnp.where(lo_m, rden[:nq], rden[nq:])
            o_ref[:, c0:c0 + 2 * HEAD_DIM] = out.astype(o_ref.dtype)


def _band_mask(nq, q0):
    qi = lax.broadcasted_iota(jnp.int32, (nq, 2 * WINDOW), 0) + q0
    kj = lax.broadcasted_iota(jnp.int32, (nq, 2 * WINDOW), 1)
    return (kj <= qi) & (qi - kj < WINDOW), kj


def _attn_sample_kernel(sink_ref, q_ref, kn_ref, vn_ref, ck_ref, cv_ref, o_ref, kw_ref, vw_ref,
                        kall_ref, vall_ref, *, seqs, t):
    wb = ck_ref.shape[1]

    @pl.when(pl.program_id(0) == 0)
    def _():
        kall_ref[...] = jnp.zeros_like(kall_ref)
        vall_ref[...] = jnp.zeros_like(vall_ref)

    mask, _ = _band_mask(t, wb)
    for s in range(seqs):
        rows = slice(s * t, (s + 1) * t)
        kall_ref[0:wb, :] = ck_ref[s]
        vall_ref[0:wb, :] = cv_ref[s]
        kall_ref[wb:wb + t, :] = kn_ref[rows, :]
        vall_ref[wb:wb + t, :] = vn_ref[rows, :]
        _attend(q_ref[rows, :], _kv_operands(kall_ref[...], vall_ref[...]), mask, sink_ref, o_ref.at[rows, :])
        kw_ref[s] = kall_ref[t:t + wb, :]
        vw_ref[s] = vall_ref[t:t + wb, :]


def _attn_sample(q, k, v, cache_k, cache_v, sinks, *, t, seqs):
    n = q.shape[0]
    b, wb, _ = cache_k.shape
    row = lambda w: pl.BlockSpec((seqs * t, w), lambda i: (i, 0))
    win = pl.BlockSpec((seqs, wb, KV_WIDTH), lambda i: (i, 0, 0))
    return pl.pallas_call(
        functools.partial(_attn_sample_kernel, seqs=seqs, t=t),
        grid=(b // seqs,),
        in_specs=[pl.BlockSpec(memory_space=pltpu.SMEM), row(ATT_WIDTH), row(KV_WIDTH), row(KV_WIDTH), win, win],
        out_specs=[row(ATT_WIDTH), win, win],
        out_shape=[jax.ShapeDtypeStruct((n, ATT_WIDTH), F32),
                   jax.ShapeDtypeStruct((b, wb, KV_WIDTH), F32),
                   jax.ShapeDtypeStruct((b, wb, KV_WIDTH), F32)],
        scratch_shapes=[pltpu.VMEM((2 * WINDOW, KV_WIDTH), F32), pltpu.VMEM((2 * WINDOW, KV_WIDTH), F32)],
        compiler_params=_params("arbitrary"),
        name="attn_sample",
    )(sinks, q, k, v, cache_k, cache_v)


def _wkv_kernel(r_ref, l_ref, k_ref, v_ref, kk_ref, kb_ref, s0_ref, o_ref, st_ref, z_ref, *, nseq, nchunks, rows,
                real):
    c = rows
    pw = 2 * HEAD_DIM
    j = pl.program_id(1)
    npairs = RWKV_HEADS // 2
    pairs = range(npairs)

    @pl.when(j == 0)
    def _():
        for s in range(nseq):
            for p in pairs:
                z_ref[s * npairs + p] = jnp.concatenate([s0_ref[s, 2 * p], s0_ref[s, 2 * p + 1]], axis=0).T

    def pair_iota(nrow):
        return (lax.broadcasted_iota(jnp.int32, (nrow, pw), 0),
                lax.broadcasted_iota(jnp.int32, (nrow, pw), 1) & (HEAD_DIM - 1))

    row, col = pair_iota(c)
    lo = lax.broadcasted_iota(jnp.int32, (1, pw), 1) < HEAD_DIM
    lower2 = jnp.concatenate([row > col, row >= col], axis=0)
    eye = (row == col).astype(F32)
    krow, kcol = pair_iota(HEAD_DIM)
    eye_ch = (krow == kcol).astype(F32)
    ri = lax.broadcasted_iota(jnp.int32, (c, c), 0)
    tri = (ri >= lax.broadcasted_iota(jnp.int32, (c, c), 1)).astype(BF16)

    def bd(y):
        zero = jnp.zeros_like(y)
        parts = [jnp.where(lo, y, zero), jnp.where(lo, zero, y)]
        if y.shape[0] < HEAD_DIM:
            fill = jnp.zeros((HEAD_DIM - y.shape[0], pw), y.dtype)
            parts = [parts[0], fill, parts[1], fill]
        return jnp.concatenate(parts, axis=0)

    memo = {}

    def cached(x, tag, build):
        key = (id(x), tag)
        if key not in memo:
            memo[key] = (x, build())
        return memo[key][1]

    def split(x):
        return cached(x, "split", lambda: _split2(x))

    def lhs(x, passes, axis=1):
        hi, low = split(x)
        return cached(x, ("lhs", passes, axis), lambda: jnp.concatenate([hi, low, hi][:passes], axis=axis))

    def rhs(y, passes, axis=0):
        hi, low = split(y)

        def build():
            bh = bd(hi)
            return jnp.concatenate([bh, bh, bd(low)][:passes], axis=axis)
        return cached(y, ("rhs", passes, axis), build)

    def mm(x, y, group):
        return _dot(lhs(x, WKV_PASSES[group]), rhs(y, WKV_PASSES[group]))

    def mm_nt(x, y, group):
        return _dot(lhs(x, WKV_PASSES[group]), rhs(y, WKV_PASSES[group], 1), NT)

    def mm_tn(x, y, group):
        passes = WKV_PASSES[group]
        yh, yl = split(y)
        full = _dot(lhs(x, passes, 0), jnp.concatenate([yh, yh, yl][:passes], axis=0), TN)
        return jnp.where(lo, full[:HEAD_DIM], full[HEAD_DIM:])

    prep = {}
    for s in range(nseq):
        for q in range(nchunks):
            tr = slice(q * real, (q + 1) * real)

            def tok(ref):
                x = ref[s, tr, :]
                return x if real == c else jnp.concatenate([x, jnp.zeros((c - real, x.shape[1]), x.dtype)], axis=0)

            lw = tok(l_ref)
            cum = sum(_dot(tri, part) for part in _split3(lw))
            tot = cum[c - 1:c, :]
            en = jnp.exp(-cum)
            ed = jnp.exp(tot - cum)
            kk = tok(kk_ref)
            kb = tok(kb_ref)
            kx = tok(k_ref)
            prep[s, q] = dict(a=-kk * jnp.exp(cum - lw), r=tok(r_ref) * jnp.exp(cum), b=kb * en, k=kx * en,
                              bh=kb * ed, kh=kx * ed, v=tok(v_ref), etot=jnp.exp(tot))

    items = [(s, q, p) for s in range(nseq) for q in range(nchunks) for p in pairs]
    ps = lambda name, it: prep[it[0], it[1]][name][:, it[2] * pw:(it[2] + 1) * pw]
    each = lambda f: [f(i) for i in range(len(items))]
    pv = {name: each(lambda i: ps(name, items[i])) for name in ("a", "r", "b", "k", "bh", "kh", "v", "etot")}
    ar = each(lambda i: jnp.concatenate([pv["a"][i], pv["r"][i]], axis=0))
    g_b = each(lambda i: jnp.where(lower2, mm_nt(ar[i], pv["b"][i], "gram"), 0.0))
    g_k = each(lambda i: jnp.where(lower2, mm_nt(ar[i], pv["k"][i], "gram"), 0.0))
    a_rb = each(lambda i: g_b[i][c:])
    p_m = each(lambda i: g_b[i][:c])
    t_m = each(lambda i: eye + p_m[i])
    n = 2
    while n < c:
        p_m = each(lambda i: mm(p_m[i], p_m[i], "inverse"))
        t_m = each(lambda i: t_m[i] + mm(p_m[i], t_m[i], "inverse"))
        n *= 2
    gv = each(lambda i: mm(g_k[i], pv["v"][i], "apply"))
    av = each(lambda i: gv[i][:c])
    w_m = each(lambda i: mm(t_m[i], pv["a"][i], "apply"))
    u_m = each(lambda i: mm(t_m[i], av[i], "apply"))
    r_p = each(lambda i: pv["r"][i] + mm(a_rb[i], w_m[i], "apply"))
    o_i = each(lambda i: mm(a_rb[i], u_m[i], "apply") + gv[i][c:])
    m_m = each(lambda i: mm_tn(pv["bh"][i], w_m[i], "state") + eye_ch * pv["etot"][i])
    bk = each(lambda i: jnp.concatenate([pv["bh"][i], pv["kh"][i]], axis=0))
    uv = each(lambda i: jnp.concatenate([u_m[i], pv["v"][i]], axis=0))
    n_m = each(lambda i: mm_tn(bk[i], uv[i], "state"))
    rm = each(lambda i: jnp.concatenate([r_p[i], m_m[i]], axis=0))
    z = [z_ref[sp] for sp in range(nseq * npairs)]
    for q in range(nchunks):
        for s in range(nseq):
            for p in pairs:
                i = items.index((s, q, p))
                sp = s * npairs + p
                oz = mm(rm[i], z[sp], "carry")
                o_ref[s, q * real:(q + 1) * real, p * pw:(p + 1) * pw] = (oz[:c] + o_i[i])[:real]
                z[sp] = oz[c:] + n_m[i]
    for sp in range(nseq * npairs):
        z_ref[sp] = z[sp]

    @pl.when(j == pl.num_programs(1) - 1)
    def _():
        for s in range(nseq):
            for p in pairs:
                zt = z_ref[s * npairs + p].T
                st_ref[s, 2 * p] = zt[:HEAD_DIM]
                st_ref[s, 2 * p + 1] = zt[HEAD_DIM:]


def _wkv(r, lw, k, v, kk, kb, s0, *, nseq, nchunks, rows, real=None):
    real = rows if real is None else real
    assert real == rows or nchunks == 1
    b, t, w = r.shape
    blk = nchunks * real
    tok = pl.BlockSpec((nseq, blk, w), lambda bi, j: (bi, j, 0))
    st = pl.BlockSpec((nseq, RWKV_HEADS, HEAD_DIM, HEAD_DIM), lambda bi, j: (bi, 0, 0, 0))
    return pl.pallas_call(
        functools.partial(_wkv_kernel, nseq=nseq, nchunks=nchunks, rows=rows, real=real),
        grid=(b // nseq, t // blk),
        in_specs=[tok] * 6 + [st],
        out_specs=[tok, st],
        out_shape=[jax.ShapeDtypeStruct((b, t, w), F32),
                   jax.ShapeDtypeStruct((b, RWKV_HEADS, HEAD_DIM, HEAD_DIM), F32)],
        scratch_shapes=[pltpu.VMEM((nseq * RWKV_HEADS // 2, HEAD_DIM, 2 * HEAD_DIM), F32)],
        compiler_params=_params("arbitrary", "arbitrary"),
        name="wkv_scan",
    )(r, lw, k, v, kk, kb, s0)


def _post_kernel(o_ref, bo_ref, g_ref, att_ref, x_ref, gng_ref, gnb_ref, ones_ref, wo_ref, gf_ref,
                 wg_ref, wu_ref, wd_ref, gfin_ref, y_ref, *, final):
    tm = x_ref.shape[0]
    sub = min(tm, POST_SUB_ROWS)
    parts = [slice(i * sub, (i + 1) * sub) for i in range(tm // sub)]
    each = lambda f: [f(i) for i in range(len(parts))]
    ones_bd = ones_ref[...]
    o = each(lambda i: o_ref[parts[i], :])
    mean = each(lambda i: _seg_sum(o[i], ones_bd) * (1.0 / HEAD_DIM))
    oc = each(lambda i: o[i] - mean[i])
    var = each(lambda i: _seg_sum(oc[i] * oc[i], ones_bd) * (1.0 / HEAD_DIM))
    rw = each(lambda i: ((oc[i] * lax.rsqrt(var[i] + GN_EPS) * gng_ref[...] + gnb_ref[...] + bo_ref[parts[i], :])
                         * g_ref[parts[i], :]).astype(BF16))
    x1 = each(lambda i: x_ref[parts[i], :] + _dot(att_ref[parts[i], :].astype(BF16), wo_ref[:ATT_WIDTH, :])
              + _dot(rw[i], wo_ref[ATT_WIDTH:, :]))
    u = each(lambda i: _rms(x1[i], gf_ref[...]).astype(BF16))
    gate = each(lambda i: _dot(u[i], wg_ref[...]))
    up = each(lambda i: _dot(u[i], wu_ref[...]))
    hid = each(lambda i: (gate[i] * _sigmoid(gate[i]) * up[i]).astype(BF16))
    x2 = each(lambda i: x1[i] + _dot(hid[i], wd_ref[...]))
    for i in range(len(parts)):
        y_ref[parts[i], :] = _rms(x2[i], gfin_ref[...]) if final else x2[i]


def _post(o, bonus, g, att, x, p, *, tm, final):
    n = x.shape[0]
    d_ff = p["w_gate"].shape[1]
    row = lambda w: pl.BlockSpec((tm, w), lambda i: (i, 0))
    once = lambda shape: pl.BlockSpec(shape, lambda i: (0,) * len(shape), pipeline_mode=pl.Buffered(1))
    w = RWKV_WIDTH
    return pl.pallas_call(
        functools.partial(_post_kernel, final=final),
        grid=(n // tm,),
        in_specs=[row(w), row(w), row(w), row(ATT_WIDTH), row(D_MODEL),
                  once((1, w)), once((1, w)), once((2 * HEAD_DIM, 2 * HEAD_DIM)), once((ATT_WIDTH + w, D_MODEL)),
                  once((1, D_MODEL)),
                  once((D_MODEL, d_ff)), once((D_MODEL, d_ff)), once((d_ff, D_MODEL)), once((1, D_MODEL))],
        out_specs=row(D_MODEL),
        out_shape=jax.ShapeDtypeStruct((n, D_MODEL), F32),
        compiler_params=_params("arbitrary"),
        name="post_ffn",
    )(o, bonus, g, att, x, p["gn_g"], p["gn_b"], p["ones_bd"], p["w_out"], p["g_ffn"],
      p["w_gate"], p["w_up"], p["w_down"], p["g_final"])


def _layer_params(l, g_mix, w_in, attn_sinks, rwkv_mu, w0, w2, a0, a2, g2, k_k, k_a, r_k, gn_g, gn_b,
                  w_out, g_ffn, w_gate, w_up, w_down, g_final):
    vec = lambda a: a.reshape(1, -1).astype(F32)
    hd = jnp.arange(2 * HEAD_DIM) // HEAD_DIM
    return dict(
        g_mix=g_mix[l], w_in=w_in[l].astype(BF16), sinks=attn_sinks[l].astype(F32),
        mu=vec(rwkv_mu[l]), w0=vec(w0[l]), w2=w2[l].astype(BF16), a0=vec(a0[l]), a2=a2[l].astype(BF16),
        g2=g2[l].astype(BF16), k_k=vec(k_k[l]), k_a=vec(k_a[l]), r_k=vec(r_k[l]),
        gn_g=vec(gn_g[l]), gn_b=vec(gn_b[l]), w_out=w_out[l].astype(BF16), g_ffn=vec(g_ffn[l]),
        w_gate=w_gate[l].astype(BF16), w_up=w_up[l].astype(BF16), w_down=w_down[l].astype(BF16),
        g_final=vec(g_final), ones_bd=(hd[:, None] == hd[None, :]).astype(BF16),
    )


def _pick(n, pref):
    t = pref
    while n % t:
        t //= 2
    return t


def _prompt_layer(x, p, final):
    b, t, d = x.shape
    n = b * t
    x2 = x.reshape(n, d)
    tm = _pick(t, 512)
    att, k, v, r, lw, k2, vv, kk, kb, g, bonus = _inproj_prep(x2, p, tm=tm, blocks_per_seq=t // tm)
    s0 = jnp.zeros((b, RWKV_HEADS, HEAD_DIM, HEAD_DIM), F32)
    sh = lambda a: a.reshape(b, t, RWKV_WIDTH)
    o, s_new = _wkv(sh(r), sh(lw), sh(k2), sh(vv), sh(kk), sh(kb), s0, nseq=_pick(b, WKV_PROMPT_SEQS),
                    nchunks=_pick(t // HEAD_DIM, WKV_PROMPT_CHUNKS), rows=HEAD_DIM)
    y = _post(o.reshape(n, -1), bonus, g, att, x2, p, tm=_pick(n, POST_ROWS), final=final)
    wp = min(WINDOW, t)
    k_win = k.reshape(b, t, KV_WIDTH)[:, t - wp:].reshape(b, wp, KV_HEADS, HEAD_DIM)
    v_win = v.reshape(b, t, KV_WIDTH)[:, t - wp:].reshape(b, wp, KV_HEADS, HEAD_DIM)
    shift = _rms_rows(x[:, -1], p["g_mix"])
    return y.reshape(b, t, d), k_win, v_win, s_new, shift


def _sample_layer(x, h_prev, k_buf, v_buf, s0, p, final):
    b, t, d = x.shape
    n = b * t
    x2 = x.reshape(n, d)
    wb = k_buf.shape[1]
    q, k, v, feat = _inproj(x2, p["g_mix"], p["w_in"], normalize=True, tm=_pick(n, 512), q_dtype=F32)
    _, _, _, feat_prev = _inproj(h_prev, p["g_mix"], p["w_in"], normalize=False, tm=_pick(b, 512))
    r, lw, k2, vv, kk, kb, g, bonus = _prep(feat, feat_prev, p, t=t, tm=_pick(n, 512))
    att, k_win, v_win = _attn_sample(q, k, v, k_buf.reshape(b, wb, -1), v_buf.reshape(b, wb, -1),
                                     p["sinks"], t=t, seqs=_pick(b, 8))
    tp = -(-t // BF16_ROWS) * BF16_ROWS
    assert tp <= HEAD_DIM and t % 8 == 0
    sh = lambda a: a.reshape(b, t, RWKV_WIDTH)
    o, s_new = _wkv(sh(r), sh(lw), sh(k2), sh(vv), sh(kk), sh(kb), s0, nseq=_pick(b, WKV_SAMPLE_SEQS),
                    nchunks=1, rows=tp, real=t)
    y = _post(o.reshape(n, -1), bonus, g, att, x2, p, tm=_pick(n, POST_ROWS), final=final)
    shift = _rms_rows(x[:, -1], p["g_mix"])
    return (y.reshape(b, t, d), k_win.reshape(b, wb, KV_HEADS, HEAD_DIM), v_win.reshape(b, wb, KV_HEADS, HEAD_DIM),
            s_new, shift)


def kernel(x_prompt, x_sample, cache_k, cache_v, state_wkv, state_shift, g_mix, w_in, attn_sinks, rwkv_mu, w0, w2,
           a0, a2, g2, k_k, k_a, r_k, gn_g, gn_b, w_out, g_ffn, w_gate, w_up, w_down, g_final):
    depth = w_in.shape[0]
    xp, xs = x_prompt, x_sample
    outs_p, outs_s = [], []
    for l in range(depth):
        p = _layer_params(l, g_mix, w_in, attn_sinks, rwkv_mu, w0, w2, a0, a2, g2, k_k, k_a, r_k, gn_g, gn_b,
                          w_out, g_ffn, w_gate, w_up, w_down, g_final)
        final = l == depth - 1
        xp, kp, vp, sp, hp = _prompt_layer(xp, p, final)
        xs, kn, vn, sn, hn = _sample_layer(xs, state_shift[l], cache_k[l], cache_v[l], state_wkv[l], p, final)
        outs_p.append((kp, vp, sp, hp))
        outs_s.append((kn, vn, sn, hn))
    stack = lambda outs, i: jnp.stack([o[i] for o in outs])
    return (xp, xs,
            stack(outs_p, 0), stack(outs_p, 1), stack(outs_p, 2), stack(outs_p, 3),
            stack(outs_s, 0), stack(outs_s, 1), stack(outs_s, 2), stack(outs_s, 3))
```

```python
import functools
import math

import jax
import jax.numpy as jnp
from jax import lax
from jax.experimental import pallas as pl
from jax.experimental.pallas import tpu as pltpu

F32 = jnp.float32
BF16 = jnp.bfloat16

D_MODEL = 1024
HEAD_DIM = 64
ATT_HEADS = 8
KV_HEADS = 2
ATT_WIDTH = ATT_HEADS * HEAD_DIM
KV_WIDTH = KV_HEADS * HEAD_DIM
RWKV_HEADS = 8
RWKV_WIDTH = RWKV_HEADS * HEAD_DIM
WINDOW = 128
DECAY_LORA = 64
ICL_LORA = 64
GATE_LORA = 128
RWKV_BLOCK = 3 * RWKV_WIDTH + DECAY_LORA + ICL_LORA + GATE_LORA
ATT_PROJ = ATT_WIDTH + 2 * KV_WIDTH
PROJ_WIDTH = ATT_PROJ + RWKV_BLOCK
RMS_EPS = 1e-6
GN_EPS = 64e-5
L2_EPS = 1e-12
NEG_INF = -1e30
DECAY_SCALE = math.exp(-0.5)

V7X_VMEM_BYTES = 64 * 1024 * 1024
VMEM_LIMIT_BYTES = V7X_VMEM_BYTES * 3 // 4
LANES = 128

WKV_PROMPT_SEQS = 2
WKV_PROMPT_CHUNKS = 2
WKV_SAMPLE_SEQS = 8
WKV_PASSES = dict(gram=1, inverse=1, apply=1, state=3, carry=3)
BF16_ROWS = 16
POST_ROWS = 512
POST_SUB_ROWS = 256
INPROJ_ROWS = 512
INPROJ_SUB_ROWS = 256

NN = (((1,), (0,)), ((), ()))
NT = (((1,), (1,)), ((), ()))
TN = (((0,), (0,)), ((), ()))


def _dot(a, b, dn=NN):
    return lax.dot_general(a, b, dn, preferred_element_type=F32)


def _split2(x):
    hi = x.astype(BF16)
    lo = (x - hi.astype(F32)).astype(BF16)
    return hi, lo


def _split3(x):
    hi = x.astype(BF16)
    r1 = x - hi.astype(F32)
    mid = r1.astype(BF16)
    lo = (r1 - mid.astype(F32)).astype(BF16)
    return hi, mid, lo


def _seg_sum(x, ones_bd):
    hi, lo = _split2(x)
    slabs = [slice(i, i + LANES) for i in range(0, x.shape[1], LANES)]
    return jnp.concatenate([_dot(hi[:, s], ones_bd) + _dot(lo[:, s], ones_bd) for s in slabs], axis=1)


def _rms(x, g):
    return x * lax.rsqrt(jnp.mean(x * x, axis=-1, keepdims=True) + RMS_EPS) * g


def _sigmoid(z):
    return 1.0 / (1.0 + jnp.exp(-z))


def _const_spec(shape):
    return pl.BlockSpec(shape, lambda *_: (0,) * len(shape))


def _params(*sem):
    return pltpu.CompilerParams(dimension_semantics=sem, vmem_limit_bytes=VMEM_LIMIT_BYTES)


def _rms_rows_kernel(x_ref, g_ref, o_ref):
    o_ref[...] = _rms(x_ref[...], g_ref[...])


def _rms_rows(x, g):
    n, d = x.shape
    return pl.pallas_call(
        _rms_rows_kernel,
        out_shape=jax.ShapeDtypeStruct((n, d), F32),
        name="rms_rows",
    )(x, g.reshape(1, d))


def _inproj_kernel(x_ref, g_ref, w_ref, q_ref, k_ref, v_ref, f_ref, *, norm_rows):
    x = x_ref[...]
    row = lax.broadcasted_iota(jnp.int32, (x.shape[0], 1), 0) + pl.program_id(0) * x.shape[0]
    h = jnp.where(row < norm_rows, _rms(x, g_ref[...]), x)
    proj = _dot(h.astype(BF16), w_ref[...])
    q_ref[...] = proj[:, :ATT_WIDTH]
    k_ref[...] = proj[:, ATT_WIDTH:ATT_WIDTH + KV_WIDTH]
    v_ref[...] = proj[:, ATT_WIDTH + KV_WIDTH:ATT_PROJ]
    f_ref[...] = proj[:, ATT_PROJ:]


def _inproj(x, g_mix, w_in_bf, *, norm_rows, tm):
    n = x.shape[0]
    row = lambda w: pl.BlockSpec((tm, w), lambda i: (i, 0))
    return pl.pallas_call(
        functools.partial(_inproj_kernel, norm_rows=norm_rows),
        grid=(n // tm,),
        in_specs=[row(D_MODEL), _const_spec((1, D_MODEL)), _const_spec((D_MODEL, PROJ_WIDTH))],
        out_specs=[row(ATT_WIDTH), row(KV_WIDTH), row(KV_WIDTH), row(RWKV_BLOCK)],
        out_shape=[
            jax.ShapeDtypeStruct((n, ATT_WIDTH), F32),
            jax.ShapeDtypeStruct((n, KV_WIDTH), F32),
            jax.ShapeDtypeStruct((n, KV_WIDTH), F32),
            jax.ShapeDtypeStruct((n, RWKV_BLOCK), F32),
        ],
        compiler_params=_params("arbitrary"),
        name="inproj",
    )(x, g_mix.reshape(1, D_MODEL), w_in_bf)


N_PREP_PARAMS = 10
N_PREP_OUTS = 8


def _carried_prev(f, carry_ref):
    tm = f.shape[0]
    rolled = pltpu.roll(f, 1, 0)
    row = lax.broadcasted_iota(jnp.int32, f.shape, 0)
    prev = jnp.where(row == 0, carry_ref[0:1, :], rolled)
    carry_ref[0:1, :] = f[tm - 1:tm, :]
    return prev


def _prep_kernel(f_ref, p_ref, *refs, t):
    f = f_ref[...]
    tm = f.shape[0]
    before = jnp.broadcast_to(p_ref[...], (tm // t, t, f.shape[1])).reshape(tm, f.shape[1])
    row = lax.broadcasted_iota(jnp.int32, f.shape, 0)
    prev = jnp.where(row % t == 0, before, pltpu.roll(f, 1, 0))
    _prep_math(f, prev, refs[:N_PREP_PARAMS], refs[N_PREP_PARAMS:N_PREP_PARAMS + N_PREP_OUTS])


def _inproj_prep_kernel(sink_ref, x_ref, g_ref, w_ref, *refs, blocks_per_seq):
    att_ref, k_ref, v_ref = refs[N_PREP_PARAMS:N_PREP_PARAMS + 3]
    outs = refs[N_PREP_PARAMS + 3:N_PREP_PARAMS + 3 + N_PREP_OUTS]
    carry_ref, kprev_ref, vprev_ref = refs[N_PREP_PARAMS + 3 + N_PREP_OUTS:]
    first = pl.program_id(0) % blocks_per_seq == 0

    @pl.when(first)
    def _():
        carry_ref[...] = jnp.zeros_like(carry_ref)
        kprev_ref[...] = jnp.zeros_like(kprev_ref)
        vprev_ref[...] = jnp.zeros_like(vprev_ref)

    tm = x_ref.shape[0]
    sub = min(tm, INPROJ_SUB_ROWS)
    parts = [slice(i * sub, (i + 1) * sub) for i in range(tm // sub)]
    proj = [_dot(_rms(x_ref[rows, :], g_ref[...]).astype(BF16), w_ref[...]) for rows in parts]
    for rows, pj in zip(parts, proj):
        k_ref[rows, :] = pj[:, ATT_WIDTH:ATT_WIDTH + KV_WIDTH]
        v_ref[rows, :] = pj[:, ATT_WIDTH + KV_WIDTH:ATT_PROJ]
    for rows, pj in zip(parts, proj):
        f = pj[:, ATT_PROJ:]
        _prep_math(f, _carried_prev(f, carry_ref), refs[:N_PREP_PARAMS], [o.at[rows, :] for o in outs])

    def blk(i, c0, c1):
        r0 = i * WINDOW
        return proj[r0 // sub][r0 % sub:r0 % sub + WINDOW, c0:c1]

    nblk = tm // WINDOW
    kcol, vcol = (ATT_WIDTH, ATT_WIDTH + KV_WIDTH), (ATT_WIDTH + KV_WIDTH, ATT_PROJ)
    ops = [_kv_operands(kprev_ref[...], vprev_ref[...])] + [_kv_operands(blk(i, *kcol), blk(i, *vcol))
                                                           for i in range(nblk)]
    band, kj = _band_mask(WINDOW, WINDOW)
    for i in range(nblk):
        mask = band & (jnp.logical_not(first) | (kj >= WINDOW)) if i == 0 else band
        kv_ops = [tuple(jnp.concatenate([ops[i][kv][n], ops[i + 1][kv][n]], axis=0) for n in range(3))
                  for kv in range(KV_HEADS)]
        _attend(blk(i, 0, ATT_WIDTH).astype(BF16), kv_ops, mask, sink_ref,
                att_ref.at[i * WINDOW:(i + 1) * WINDOW, :])
    kprev_ref[...] = blk(nblk - 1, *kcol)
    vprev_ref[...] = blk(nblk - 1, *vcol)


def _prep_math(f, prev, params, outs):
    (mu_ref, w0_ref, w2_ref, a0_ref, a2_ref, g2_ref, kk_ref, ka_ref, rk_ref, ones_ref) = params
    (r_out, l_out, k_out, v_out, kk_out, kb_out, g_out, bo_out) = outs
    xs = f + (prev - f) * mu_ref[...]
    w3 = 3 * RWKV_WIDTH
    r = xs[:, :RWKV_WIDTH]
    k = xs[:, RWKV_WIDTH:2 * RWKV_WIDTH]
    v = xs[:, 2 * RWKV_WIDTH:w3]
    wl = xs[:, w3:w3 + DECAY_LORA]
    al = xs[:, w3 + DECAY_LORA:w3 + DECAY_LORA + ICL_LORA]
    gl = xs[:, w3 + DECAY_LORA + ICL_LORA:]
    log_decay = -DECAY_SCALE * _sigmoid(w0_ref[...] + _dot(jnp.tanh(wl).astype(BF16), w2_ref[...]))
    a = _sigmoid(a0_ref[...] + _dot(al.astype(BF16), a2_ref[...]))
    g = _dot(_sigmoid(gl).astype(BF16), g2_ref[...])
    ones_bd = ones_ref[...]
    kk = k * kk_ref[...]
    kk = kk * lax.rsqrt(_seg_sum(kk * kk, ones_bd) + L2_EPS)
    k2 = k * (1.0 + (a - 1.0) * ka_ref[...])
    r_out[...] = r
    l_out[...] = log_decay
    k_out[...] = k2
    v_out[...] = v
    kk_out[...] = kk
    kb_out[...] = kk * a
    g_out[...] = g
    bo_out[...] = _seg_sum(r * k2 * rk_ref[...], ones_bd) * v


def _prep_params(p):
    w = RWKV_WIDTH
    ins = [p["mu"], p["w0"], p["w2"], p["a0"], p["a2"], p["g2"], p["k_k"], p["k_a"], p["r_k"], p["ones_bd"]]
    specs = [_const_spec((1, RWKV_BLOCK)), _const_spec((1, w)), _const_spec((DECAY_LORA, w)),
             _const_spec((1, w)), _const_spec((ICL_LORA, w)), _const_spec((GATE_LORA, w)),
             _const_spec((1, w)), _const_spec((1, w)), _const_spec((1, w)),
             _const_spec((LANES, LANES))]
    assert len(ins) == N_PREP_PARAMS
    return ins, specs


def _inproj_prep(x, p, *, tm, blocks_per_seq):
    n = x.shape[0]
    assert tm % WINDOW == 0
    row = lambda w: pl.BlockSpec((tm, w), lambda i: (i, 0))
    w = RWKV_WIDTH
    prm, prm_specs = _prep_params(p)
    return pl.pallas_call(
        functools.partial(_inproj_prep_kernel, blocks_per_seq=blocks_per_seq),
        grid=(n // tm,),
        in_specs=[pl.BlockSpec(memory_space=pltpu.SMEM), row(D_MODEL), _const_spec((1, D_MODEL)),
                  _const_spec((D_MODEL, PROJ_WIDTH))] + prm_specs,
        out_specs=[row(ATT_WIDTH), row(KV_WIDTH), row(KV_WIDTH)] + [row(w)] * N_PREP_OUTS,
        out_shape=[jax.ShapeDtypeStruct((n, ATT_WIDTH), BF16), jax.ShapeDtypeStruct((n, KV_WIDTH), F32),
                   jax.ShapeDtypeStruct((n, KV_WIDTH), F32)] + [jax.ShapeDtypeStruct((n, w), F32)] * N_PREP_OUTS,
        scratch_shapes=[pltpu.VMEM((8, RWKV_BLOCK), F32), pltpu.VMEM((WINDOW, KV_WIDTH), F32),
                        pltpu.VMEM((WINDOW, KV_WIDTH), F32)],
        compiler_params=_params("arbitrary"),
        name="inproj_prep_attn",
    )(p["sinks"], x, p["g_mix"].reshape(1, D_MODEL), p["w_in"], *prm)


def _prep(feat, feat_before, p, *, t, tm):
    n = feat.shape[0]
    assert tm % t == 0
    row = lambda w: pl.BlockSpec((tm, w), lambda i: (i, 0))
    w = RWKV_WIDTH
    prm, prm_specs = _prep_params(p)
    return pl.pallas_call(
        functools.partial(_prep_kernel, t=t),
        grid=(n // tm,),
        in_specs=[row(RWKV_BLOCK), pl.BlockSpec((tm // t, 1, RWKV_BLOCK), lambda i: (i, 0, 0))] + prm_specs,
        out_specs=[row(w)] * N_PREP_OUTS,
        out_shape=[jax.ShapeDtypeStruct((n, w), F32)] * N_PREP_OUTS,
        compiler_params=_params("arbitrary"),
        name="rwkv_prep",
    )(feat, feat_before.reshape(n // t, 1, RWKV_BLOCK), *prm)


def _kv_operands(kblk, vblk):
    lo_m = lax.broadcasted_iota(jnp.int32, (1, LANES), 1) < HEAD_DIM
    k_sw = pltpu.roll(kblk, HEAD_DIM, 1)
    v_sw = pltpu.roll(vblk, HEAD_DIM, 1)
    ops = []
    for kv in range(KV_HEADS):
        own = lo_m if kv == 0 else ~lo_m
        kdup = jnp.where(own, kblk, k_sw).astype(BF16)
        v_own = jnp.where(own, vblk, 0.0).astype(BF16)
        v_oth = jnp.where(own, 0.0, v_sw).astype(BF16)
        ops.append((kdup,) + ((v_own, v_oth) if kv == 0 else (v_oth, v_own)))
    return ops


def _attend(q, kv_ops, mask, sink_ref, o_ref):
    nq = q.shape[0]
    lo_m = lax.broadcasted_iota(jnp.int32, (1, LANES), 1) < HEAD_DIM
    mask2 = jnp.concatenate([mask, mask], axis=0)
    rowi = lax.broadcasted_iota(jnp.int32, (2 * nq, 1), 0)
    for kv in range(KV_HEADS):
        kdup, v_lo, v_hi = kv_ops[kv]
        for pr in range(2):
            c0 = (kv * 2 + pr) * LANES
            qp = q[:, c0:c0 + LANES].astype(F32) * (HEAD_DIM ** -0.5)
            qs = jnp.concatenate([jnp.where(lo_m, qp, 0.0), jnp.where(lo_m, 0.0, qp)], axis=0)
            s = jnp.where(mask2, _dot(qs.astype(BF16), kdup, NT), NEG_INF)
            h0 = (kv * 2 + pr) * 2
            sink = jnp.where(rowi < nq, sink_ref[h0], sink_ref[h0 + 1])
            m = jnp.maximum(jnp.max(s, axis=-1, keepdims=True), sink)
            e = jnp.exp(s - m)
            rden = 1.0 / (jnp.sum(e, axis=-1, keepdims=True) + jnp.exp(sink - m))
            out = _dot(e[:nq].astype(BF16), v_lo) + _dot(e[nq:].astype(BF16), v_hi)
            out = out * jnp.where(lo_m, rden[:nq], rden[nq:])
            o_ref[:, c0:c0 + LANES] = out.astype(o_ref.dtype)


def _band_mask(nq, q0):
    qi = lax.broadcasted_iota(jnp.int32, (nq, 2 * WINDOW), 0) + q0
    kj = lax.broadcasted_iota(jnp.int32, (nq, 2 * WINDOW), 1)
    return (kj <= qi) & (qi - kj < WINDOW), kj


def _attn_sample_kernel(sink_ref, q_ref, kn_ref, vn_ref, ck_ref, cv_ref, o_ref, kw_ref, vw_ref,
                        kall_ref, vall_ref, *, seqs, t):
    wb = ck_ref.shape[1]

    @pl.when(pl.program_id(0) == 0)
    def _():
        kall_ref[...] = jnp.zeros_like(kall_ref)
        vall_ref[...] = jnp.zeros_like(vall_ref)

    mask, _ = _band_mask(t, wb)
    for s in range(seqs):
        rows = slice(s * t, (s + 1) * t)
        kall_ref[0:wb, :] = ck_ref[s]
        vall_ref[0:wb, :] = cv_ref[s]
        kall_ref[wb:wb + t, :] = kn_ref[rows, :]
        vall_ref[wb:wb + t, :] = vn_ref[rows, :]
        _attend(q_ref[rows, :], _kv_operands(kall_ref[...], vall_ref[...]), mask, sink_ref, o_ref.at[rows, :])
        kw_ref[s] = kall_ref[t:t + wb, :]
        vw_ref[s] = vall_ref[t:t + wb, :]


def _attn_sample(q, k, v, cache_k, cache_v, sinks, *, t, seqs):
    b, wb, _ = cache_k.shape
    n = b * t
    row = lambda w: pl.BlockSpec((seqs * t, w), lambda i: (i, 0))
    win = pl.BlockSpec((seqs, wb, KV_WIDTH), lambda i: (i, 0, 0))
    return pl.pallas_call(
        functools.partial(_attn_sample_kernel, seqs=seqs, t=t),
        grid=(b // seqs,),
        in_specs=[pl.BlockSpec(memory_space=pltpu.SMEM), row(ATT_WIDTH), row(KV_WIDTH), row(KV_WIDTH), win, win],
        out_specs=[row(ATT_WIDTH), win, win],
        out_shape=[jax.ShapeDtypeStruct((n, ATT_WIDTH), F32),
                   jax.ShapeDtypeStruct((b, wb, KV_WIDTH), F32),
                   jax.ShapeDtypeStruct((b, wb, KV_WIDTH), F32)],
        scratch_shapes=[pltpu.VMEM((2 * WINDOW, KV_WIDTH), F32), pltpu.VMEM((2 * WINDOW, KV_WIDTH), F32)],
        compiler_params=_params("arbitrary"),
        name="attn_sample",
    )(sinks, q, k, v, cache_k, cache_v)


def _wkv_kernel(r_ref, l_ref, k_ref, v_ref, kk_ref, kb_ref, s0_ref, o_ref, st_ref, z_ref, *, nseq, nchunks, rows,
                real):
    c = rows
    pw = LANES
    j = pl.program_id(1)
    npairs = RWKV_HEADS // 2
    pairs = range(npairs)

    @pl.when(j == 0)
    def _():
        for s in range(nseq):
            for p in pairs:
                z_ref[s * npairs + p] = jnp.concatenate([s0_ref[s, 2 * p], s0_ref[s, 2 * p + 1]], axis=0).T

    def pair_iota(nrow):
        return (lax.broadcasted_iota(jnp.int32, (nrow, pw), 0),
                lax.broadcasted_iota(jnp.int32, (nrow, pw), 1) & (HEAD_DIM - 1))

    row, col = pair_iota(c)
    lo = lax.broadcasted_iota(jnp.int32, (1, pw), 1) < HEAD_DIM
    lower2 = jnp.concatenate([row > col, row >= col], axis=0)
    eye = (row == col).astype(F32)
    krow, kcol = pair_iota(HEAD_DIM)
    eye_ch = (krow == kcol).astype(F32)
    ri = lax.broadcasted_iota(jnp.int32, (c, c), 0)
    tri = (ri >= lax.broadcasted_iota(jnp.int32, (c, c), 1)).astype(BF16)

    def bd(y):
        zero = jnp.zeros_like(y)
        parts = [jnp.where(lo, y, zero), jnp.where(lo, zero, y)]
        if y.shape[0] < HEAD_DIM:
            fill = jnp.zeros((HEAD_DIM - y.shape[0], pw), y.dtype)
            parts = [parts[0], fill, parts[1], fill]
        return jnp.concatenate(parts, axis=0)

    memo = {}

    def cached(x, tag, build):
        key = (id(x), tag)
        if key not in memo:
            memo[key] = (x, build())
        return memo[key][1]

    def split(x):
        return cached(x, "split", lambda: _split2(x))

    def lhs(x, passes, axis=1):
        hi, low = split(x)
        return cached(x, ("lhs", passes, axis), lambda: jnp.concatenate([hi, low, hi][:passes], axis=axis))

    def rhs(y, passes, axis=0):
        hi, low = split(y)

        def build():
            bh = bd(hi)
            return jnp.concatenate([bh, bh, bd(low)][:passes], axis=axis)
        return cached(y, ("rhs", passes, axis), build)

    def mm(x, y, group):
        return _dot(lhs(x, WKV_PASSES[group]), rhs(y, WKV_PASSES[group]))

    def mm_nt(x, y, group):
        return _dot(lhs(x, WKV_PASSES[group]), rhs(y, WKV_PASSES[group], 1), NT)

    def mm_tn(x, y, group):
        passes = WKV_PASSES[group]
        yh, yl = split(y)
        full = _dot(lhs(x, passes, 0), jnp.concatenate([yh, yh, yl][:passes], axis=0), TN)
        return jnp.where(lo, full[:HEAD_DIM], full[HEAD_DIM:])

    prep = {}
    for s in range(nseq):
        for q in range(nchunks):
            tr = slice(q * real, (q + 1) * real)

            def tok(ref):
                x = ref[s, tr, :]
                return x if real == c else jnp.concatenate([x, jnp.zeros((c - real, x.shape[1]), x.dtype)], axis=0)

            lw = tok(l_ref)
            cum = sum(_dot(tri, part) for part in _split3(lw))
            tot = cum[c - 1:c, :]
            en = jnp.exp(-cum)
            ed = jnp.exp(tot - cum)
            kk = tok(kk_ref)
            kb = tok(kb_ref)
            kx = tok(k_ref)
            prep[s, q] = dict(a=-kk * jnp.exp(cum - lw), r=tok(r_ref) * jnp.exp(cum), b=kb * en, k=kx * en,
                              bh=kb * ed, kh=kx * ed, v=tok(v_ref), etot=jnp.exp(tot))

    items = [(s, q, p) for s in range(nseq) for q in range(nchunks) for p in pairs]
    ps = lambda name, it: prep[it[0], it[1]][name][:, it[2] * pw:(it[2] + 1) * pw]
    each = lambda f: [f(i) for i in range(len(items))]
    pv = {name: each(lambda i: ps(name, items[i])) for name in ("a", "r", "b", "k", "bh", "kh", "v", "etot")}
    ar = each(lambda i: jnp.concatenate([pv["a"][i], pv["r"][i]], axis=0))
    g_b = each(lambda i: jnp.where(lower2, mm_nt(ar[i], pv["b"][i], "gram"), 0.0))
    g_k = each(lambda i: jnp.where(lower2, mm_nt(ar[i], pv["k"][i], "gram"), 0.0))
    a_rb = each(lambda i: g_b[i][c:])
    p_m = each(lambda i: g_b[i][:c])
    t_m = each(lambda i: eye + p_m[i])
    n = 2
    while n < c:
        p_m = each(lambda i: mm(p_m[i], p_m[i], "inverse"))
        t_m = each(lambda i: t_m[i] + mm(p_m[i], t_m[i], "inverse"))
        n *= 2
    gv = each(lambda i: mm(g_k[i], pv["v"][i], "apply"))
    av = each(lambda i: gv[i][:c])
    w_m = each(lambda i: mm(t_m[i], pv["a"][i], "apply"))
    u_m = each(lambda i: mm(t_m[i], av[i], "apply"))
    r_p = each(lambda i: pv["r"][i] + mm(a_rb[i], w_m[i], "apply"))
    o_i = each(lambda i: mm(a_rb[i], u_m[i], "apply") + gv[i][c:])
    m_m = each(lambda i: mm_tn(pv["bh"][i], w_m[i], "state") + eye_ch * pv["etot"][i])
    bk = each(lambda i: jnp.concatenate([pv["bh"][i], pv["kh"][i]], axis=0))
    uv = each(lambda i: jnp.concatenate([u_m[i], pv["v"][i]], axis=0))
    n_m = each(lambda i: mm_tn(bk[i], uv[i], "state"))
    rm = each(lambda i: jnp.concatenate([r_p[i], m_m[i]], axis=0))
    z = [z_ref[sp] for sp in range(nseq * npairs)]
    for q in range(nchunks):
        for s in range(nseq):
            for p in pairs:
                i = items.index((s, q, p))
                sp = s * npairs + p
                oz = mm(rm[i], z[sp], "carry")
                o_ref[s, q * real:(q + 1) * real, p * pw:(p + 1) * pw] = (oz[:c] + o_i[i])[:real]
                z[sp] = oz[c:] + n_m[i]
    for sp in range(nseq * npairs):
        z_ref[sp] = z[sp]

    @pl.when(j == pl.num_programs(1) - 1)
    def _():
        for s in range(nseq):
            for p in pairs:
                zt = z_ref[s * npairs + p].T
                st_ref[s, 2 * p] = zt[:HEAD_DIM]
                st_ref[s, 2 * p + 1] = zt[HEAD_DIM:]


def _wkv(r, lw, k, v, kk, kb, s0, *, nseq, nchunks, rows, real=None):
    real = rows if real is None else real
    assert real == rows or nchunks == 1
    b, t, w = r.shape
    blk = nchunks * real
    tok = pl.BlockSpec((nseq, blk, w), lambda bi, j: (bi, j, 0))
    st = pl.BlockSpec((nseq, RWKV_HEADS, HEAD_DIM, HEAD_DIM), lambda bi, j: (bi, 0, 0, 0))
    return pl.pallas_call(
        functools.partial(_wkv_kernel, nseq=nseq, nchunks=nchunks, rows=rows, real=real),
        grid=(b // nseq, t // blk),
        in_specs=[tok] * 6 + [st],
        out_specs=[tok, st],
        out_shape=[jax.ShapeDtypeStruct((b, t, w), F32),
                   jax.ShapeDtypeStruct((b, RWKV_HEADS, HEAD_DIM, HEAD_DIM), F32)],
        scratch_shapes=[pltpu.VMEM((nseq * RWKV_HEADS // 2, HEAD_DIM, LANES), F32)],
        compiler_params=_params("arbitrary", "arbitrary"),
        name="wkv_scan",
    )(r, lw, k, v, kk, kb, s0)


def _post_kernel(o_ref, bo_ref, g_ref, att_ref, x_ref, gng_ref, gnb_ref, ones_ref, wo_ref, gf_ref,
                 wg_ref, wu_ref, wd_ref, gfin_ref, y_ref, *, final):
    tm = x_ref.shape[0]
    sub = min(tm, POST_SUB_ROWS)
    parts = [slice(i * sub, (i + 1) * sub) for i in range(tm // sub)]
    each = lambda f: [f(i) for i in range(len(parts))]
    ones_bd = ones_ref[...]
    o = each(lambda i: o_ref[parts[i], :])
    mean = each(lambda i: _seg_sum(o[i], ones_bd) * (1.0 / HEAD_DIM))
    oc = each(lambda i: o[i] - mean[i])
    var = each(lambda i: _seg_sum(oc[i] * oc[i], ones_bd) * (1.0 / HEAD_DIM))
    rw = each(lambda i: ((oc[i] * lax.rsqrt(var[i] + GN_EPS) * gng_ref[...] + gnb_ref[...] + bo_ref[parts[i], :])
                         * g_ref[parts[i], :]).astype(BF16))
    x1 = each(lambda i: x_ref[parts[i], :] + _dot(att_ref[parts[i], :].astype(BF16), wo_ref[:ATT_WIDTH, :])
              + _dot(rw[i], wo_ref[ATT_WIDTH:, :]))
    u = each(lambda i: _rms(x1[i], gf_ref[...]).astype(BF16))
    gate = each(lambda i: _dot(u[i], wg_ref[...]))
    up = each(lambda i: _dot(u[i], wu_ref[...]))
    hid = each(lambda i: (gate[i] * _sigmoid(gate[i]) * up[i]).astype(BF16))
    x2 = each(lambda i: x1[i] + _dot(hid[i], wd_ref[...]))
    for i in range(len(parts)):
        y_ref[parts[i], :] = _rms(x2[i], gfin_ref[...]) if final else x2[i]


def _post(o, bonus, g, att, x, p, *, tm, final):
    n = x.shape[0]
    d_ff = p["w_gate"].shape[1]
    row = lambda w: pl.BlockSpec((tm, w), lambda i: (i, 0))
    once = lambda shape: pl.BlockSpec(shape, lambda i: (0,) * len(shape), pipeline_mode=pl.Buffered(1))
    w = RWKV_WIDTH
    return pl.pallas_call(
        functools.partial(_post_kernel, final=final),
        grid=(n // tm,),
        in_specs=[row(w), row(w), row(w), row(ATT_WIDTH), row(D_MODEL),
                  once((1, w)), once((1, w)), once((LANES, LANES)), once((ATT_WIDTH + w, D_MODEL)),
                  once((1, D_MODEL)),
                  once((D_MODEL, d_ff)), once((D_MODEL, d_ff)), once((d_ff, D_MODEL)), once((1, D_MODEL))],
        out_specs=row(D_MODEL),
        out_shape=jax.ShapeDtypeStruct((n, D_MODEL), F32),
        compiler_params=_params("arbitrary"),
        name="post_ffn",
    )(o, bonus, g, att, x, p["gn_g"], p["gn_b"], p["ones_bd"], p["w_out"], p["g_ffn"],
      p["w_gate"], p["w_up"], p["w_down"], p["g_final"])


def _layer_params(l, g_mix, w_in, attn_sinks, rwkv_mu, w0, w2, a0, a2, g2, k_k, k_a, r_k, gn_g, gn_b,
                  w_out, g_ffn, w_gate, w_up, w_down, g_final):
    vec = lambda a: a.reshape(1, -1).astype(F32)
    hd = jnp.arange(LANES) // HEAD_DIM
    return dict(
        g_mix=g_mix[l], w_in=w_in[l].astype(BF16), sinks=attn_sinks[l].astype(F32),
        mu=vec(rwkv_mu[l]), w0=vec(w0[l]), w2=w2[l].astype(BF16), a0=vec(a0[l]), a2=a2[l].astype(BF16),
        g2=g2[l].astype(BF16), k_k=vec(k_k[l]), k_a=vec(k_a[l]), r_k=vec(r_k[l]),
        gn_g=vec(gn_g[l]), gn_b=vec(gn_b[l]), w_out=w_out[l].astype(BF16), g_ffn=vec(g_ffn[l]),
        w_gate=w_gate[l].astype(BF16), w_up=w_up[l].astype(BF16), w_down=w_down[l].astype(BF16),
        g_final=vec(g_final), ones_bd=(hd[:, None] == hd[None, :]).astype(BF16),
    )


def _pick(n, pref):
    t = pref
    while n % t:
        t //= 2
    return t


def _pick_rows(n, cap, unit):
    assert n % unit == 0
    return unit * max(d for d in range(1, cap // unit + 1) if (n // unit) % d == 0)


def _prompt_layer(x, p, final):
    b, t, d = x.shape
    n = b * t
    x2 = x.reshape(n, d)
    tm = _pick(t, INPROJ_ROWS)
    att, k, v, r, lw, k2, vv, kk, kb, g, bonus = _inproj_prep(x2, p, tm=tm, blocks_per_seq=t // tm)
    s0 = jnp.zeros((b, RWKV_HEADS, HEAD_DIM, HEAD_DIM), F32)
    sh = lambda a: a.reshape(b, t, RWKV_WIDTH)
    o, s_new = _wkv(sh(r), sh(lw), sh(k2), sh(vv), sh(kk), sh(kb), s0, nseq=_pick(b, WKV_PROMPT_SEQS),
                    nchunks=_pick(t // HEAD_DIM, WKV_PROMPT_CHUNKS), rows=HEAD_DIM)
    y = _post(o.reshape(n, -1), bonus, g, att, x2, p, tm=_pick(n, POST_ROWS), final=final)
    wp = min(WINDOW, t)
    k_win = k.reshape(b, t, KV_WIDTH)[:, t - wp:].reshape(b, wp, KV_HEADS, HEAD_DIM)
    v_win = v.reshape(b, t, KV_WIDTH)[:, t - wp:].reshape(b, wp, KV_HEADS, HEAD_DIM)
    shift = _rms_rows(x[:, -1], p["g_mix"])
    return y.reshape(b, t, d), k_win, v_win, s_new, shift


def _sample_layer(x, h_prev, k_buf, v_buf, s0, p, final):
    b, t, d = x.shape
    n = b * t
    x2 = x.reshape(n, d)
    wb = k_buf.shape[1]
    rows_all = jnp.concatenate([x2, h_prev.astype(x2.dtype)], axis=0)
    q, k, v, feat_all = _inproj(rows_all, p["g_mix"], p["w_in"], norm_rows=n, tm=_pick_rows(n + b, 512, 8))
    feat, feat_prev = feat_all[:n], feat_all[n:]
    r, lw, k2, vv, kk, kb, g, bonus = _prep(feat, feat_prev, p, t=t, tm=_pick(n, 512))
    att, k_win, v_win = _attn_sample(q, k, v, k_buf.reshape(b, wb, -1), v_buf.reshape(b, wb, -1),
                                     p["sinks"], t=t, seqs=_pick(b, 8))
    tp = -(-t // BF16_ROWS) * BF16_ROWS
    assert tp <= HEAD_DIM and t % 8 == 0
    sh = lambda a: a.reshape(b, t, RWKV_WIDTH)
    o, s_new = _wkv(sh(r), sh(lw), sh(k2), sh(vv), sh(kk), sh(kb), s0, nseq=_pick(b, WKV_SAMPLE_SEQS),
                    nchunks=1, rows=tp, real=t)
    y = _post(o.reshape(n, -1), bonus, g, att, x2, p, tm=_pick(n, POST_ROWS), final=final)
    shift = _rms_rows(x[:, -1], p["g_mix"])
    return (y.reshape(b, t, d), k_win.reshape(b, wb, KV_HEADS, HEAD_DIM), v_win.reshape(b, wb, KV_HEADS, HEAD_DIM),
            s_new, shift)


def kernel(x_prompt, x_sample, cache_k, cache_v, state_wkv, state_shift, g_mix, w_in, attn_sinks, rwkv_mu, w0, w2,
           a0, a2, g2, k_k, k_a, r_k, gn_g, gn_b, w_out, g_ffn, w_gate, w_up, w_down, g_final):
    depth = w_in.shape[0]
    xp, xs = x_prompt, x_sample
    outs_p, outs_s = [], []
    for l in range(depth):
        p = _layer_params(l, g_mix, w_in, attn_sinks, rwkv_mu, w0, w2, a0, a2, g2, k_k, k_a, r_k, gn_g, gn_b,
                          w_out, g_ffn, w_gate, w_up, w_down, g_final)
        final = l == depth - 1
        xp, kp, vp, sp, hp = _prompt_layer(xp, p, final)
        xs, kn, vn, sn, hn = _sample_layer(xs, state_shift[l], cache_k[l], cache_v[l], state_wkv[l], p, final)
        outs_p.append((kp, vp, sp, hp))
        outs_s.append((kn, vn, sn, hn))
    stack = lambda outs, i: jnp.stack([o[i] for o in outs])
    return (xp, xs,
            stack(outs_p, 0), stack(outs_p, 1), stack(outs_p, 2), stack(outs_p, 3),
            stack(outs_s, 0), stack(outs_s, 1), stack(outs_s, 2), stack(outs_s, 3))
```

```python
import functools
import math

import jax
import jax.numpy as jnp
from jax import lax
from jax.experimental import pallas as pl
from jax.experimental.pallas import tpu as pltpu

F32 = jnp.float32
BF16 = jnp.bfloat16

D_MODEL = 1024
HEAD_DIM = 64
ATT_HEADS = 8
KV_HEADS = 2
ATT_WIDTH = ATT_HEADS * HEAD_DIM
KV_WIDTH = KV_HEADS * HEAD_DIM
RWKV_HEADS = 8
RWKV_WIDTH = RWKV_HEADS * HEAD_DIM
WINDOW = 128
DECAY_LORA = 64
ICL_LORA = 64
GATE_LORA = 128
RWKV_BLOCK = 3 * RWKV_WIDTH + DECAY_LORA + ICL_LORA + GATE_LORA
ATT_PROJ = ATT_WIDTH + 2 * KV_WIDTH
PROJ_WIDTH = ATT_PROJ + RWKV_BLOCK
RMS_EPS = 1e-6
GN_EPS = 64e-5
L2_EPS = 1e-12
NEG_INF = -1e30
DECAY_SCALE = math.exp(-0.5)

V7X_VMEM_BYTES = 64 * 1024 * 1024
VMEM_LIMIT_BYTES = V7X_VMEM_BYTES * 3 // 4
LANES = 128

WKV_PROMPT_SEQS = 2
WKV_PROMPT_CHUNKS = 2
WKV_SAMPLE_SEQS = 8
WKV_PASSES = dict(gram=1, inverse=1, apply=1, state=3, carry=3)
BF16_ROWS = 16
POST_ROWS = 512
POST_SUB_ROWS = 256
INPROJ_ROWS = 512
INPROJ_SUB_ROWS = 256

NN = (((1,), (0,)), ((), ()))
NT = (((1,), (1,)), ((), ()))
TN = (((0,), (0,)), ((), ()))


def _dot(a, b, dn=NN):
    return lax.dot_general(a, b, dn, preferred_element_type=F32)


def _split2(x):
    hi = x.astype(BF16)
    lo = (x - hi.astype(F32)).astype(BF16)
    return hi, lo


def _split3(x):
    hi = x.astype(BF16)
    r1 = x - hi.astype(F32)
    mid = r1.astype(BF16)
    lo = (r1 - mid.astype(F32)).astype(BF16)
    return hi, mid, lo


def _seg_sum(x, ones_bd):
    hi, lo = _split2(x)
    slabs = [slice(i, i + LANES) for i in range(0, x.shape[1], LANES)]
    return jnp.concatenate([_dot(hi[:, s], ones_bd) + _dot(lo[:, s], ones_bd) for s in slabs], axis=1)


def _rms(x, g):
    return x * lax.rsqrt(jnp.mean(x * x, axis=-1, keepdims=True) + RMS_EPS) * g


def _sigmoid(z):
    return 1.0 / (1.0 + jnp.exp(-z))


def _const_spec(shape):
    return pl.BlockSpec(shape, lambda *_: (0,) * len(shape))


def _params(*sem):
    return pltpu.CompilerParams(dimension_semantics=sem, vmem_limit_bytes=VMEM_LIMIT_BYTES)


def _rms_rows_kernel(x_ref, g_ref, o_ref):
    o_ref[...] = _rms(x_ref[...], g_ref[...])


def _rms_rows(x, g):
    n, d = x.shape
    return pl.pallas_call(
        _rms_rows_kernel,
        out_shape=jax.ShapeDtypeStruct((n, d), F32),
        name="rms_rows",
    )(x, g.reshape(1, d))


def _inproj_kernel(x_ref, g_ref, w_ref, q_ref, k_ref, v_ref, f_ref, *, norm_rows):
    x = x_ref[...]
    row = lax.broadcasted_iota(jnp.int32, (x.shape[0], 1), 0) + pl.program_id(0) * x.shape[0]
    h = jnp.where(row < norm_rows, _rms(x, g_ref[...]), x)
    proj = _dot(h.astype(BF16), w_ref[...])
    q_ref[...] = proj[:, :ATT_WIDTH]
    k_ref[...] = proj[:, ATT_WIDTH:ATT_WIDTH + KV_WIDTH]
    v_ref[...] = proj[:, ATT_WIDTH + KV_WIDTH:ATT_PROJ]
    f_ref[...] = proj[:, ATT_PROJ:]


def _inproj(x, g_mix, w_in_bf, *, norm_rows, tm):
    n = x.shape[0]
    row = lambda w: pl.BlockSpec((tm, w), lambda i: (i, 0))
    return pl.pallas_call(
        functools.partial(_inproj_kernel, norm_rows=norm_rows),
        grid=(n // tm,),
        in_specs=[row(D_MODEL), _const_spec((1, D_MODEL)), _const_spec((D_MODEL, PROJ_WIDTH))],
        out_specs=[row(ATT_WIDTH), row(KV_WIDTH), row(KV_WIDTH), row(RWKV_BLOCK)],
        out_shape=[
            jax.ShapeDtypeStruct((n, ATT_WIDTH), F32),
            jax.ShapeDtypeStruct((n, KV_WIDTH), F32),
            jax.ShapeDtypeStruct((n, KV_WIDTH), F32),
            jax.ShapeDtypeStruct((n, RWKV_BLOCK), F32),
        ],
        compiler_params=_params("arbitrary"),
        name="inproj",
    )(x, g_mix.reshape(1, D_MODEL), w_in_bf)


N_PREP_PARAMS = 10
N_SCAN_COLS = 6
N_GATE_COLS = 2


def _prep_out_views(scan_ref, gate_ref, rows):
    w = RWKV_WIDTH
    return ([scan_ref.at[rows, c * w:(c + 1) * w] for c in range(N_SCAN_COLS)]
            + [gate_ref.at[rows, c * w:(c + 1) * w] for c in range(N_GATE_COLS)])


def _carried_prev(f, carry_ref):
    tm = f.shape[0]
    rolled = pltpu.roll(f, 1, 0)
    row = lax.broadcasted_iota(jnp.int32, f.shape, 0)
    prev = jnp.where(row == 0, carry_ref[0:1, :], rolled)
    carry_ref[0:1, :] = f[tm - 1:tm, :]
    return prev


def _prep_kernel(f_ref, p_ref, *refs, t):
    f = f_ref[...]
    tm = f.shape[0]
    before = jnp.broadcast_to(p_ref[...], (tm // t, t, f.shape[1])).reshape(tm, f.shape[1])
    row = lax.broadcasted_iota(jnp.int32, f.shape, 0)
    prev = jnp.where(row % t == 0, before, pltpu.roll(f, 1, 0))
    scan_ref, gate_ref = refs[N_PREP_PARAMS:N_PREP_PARAMS + 2]
    _prep_math(f, prev, refs[:N_PREP_PARAMS], _prep_out_views(scan_ref, gate_ref, slice(None)))


def _inproj_prep_kernel(sink_ref, x_ref, g_ref, w_ref, *refs, blocks_per_seq):
    att_ref, k_ref, v_ref = refs[N_PREP_PARAMS:N_PREP_PARAMS + 3]
    scan_ref, gate_ref = refs[N_PREP_PARAMS + 3:N_PREP_PARAMS + 5]
    carry_ref, kprev_ref, vprev_ref = refs[N_PREP_PARAMS + 5:]
    first = pl.program_id(0) % blocks_per_seq == 0

    @pl.when(first)
    def _():
        carry_ref[...] = jnp.zeros_like(carry_ref)
        kprev_ref[...] = jnp.zeros_like(kprev_ref)
        vprev_ref[...] = jnp.zeros_like(vprev_ref)

    tm = x_ref.shape[0]
    sub = min(tm, INPROJ_SUB_ROWS)
    parts = [slice(i * sub, (i + 1) * sub) for i in range(tm // sub)]
    proj = [_dot(_rms(x_ref[rows, :], g_ref[...]).astype(BF16), w_ref[...]) for rows in parts]
    for rows, pj in zip(parts, proj):
        k_ref[rows, :] = pj[:, ATT_WIDTH:ATT_WIDTH + KV_WIDTH]
        v_ref[rows, :] = pj[:, ATT_WIDTH + KV_WIDTH:ATT_PROJ]
    for rows, pj in zip(parts, proj):
        f = pj[:, ATT_PROJ:]
        _prep_math(f, _carried_prev(f, carry_ref), refs[:N_PREP_PARAMS], _prep_out_views(scan_ref, gate_ref, rows))

    def blk(i, c0, c1):
        r0 = i * WINDOW
        return proj[r0 // sub][r0 % sub:r0 % sub + WINDOW, c0:c1]

    nblk = tm // WINDOW
    kcol, vcol = (ATT_WIDTH, ATT_WIDTH + KV_WIDTH), (ATT_WIDTH + KV_WIDTH, ATT_PROJ)
    ops = [_kv_operands(kprev_ref[...], vprev_ref[...])] + [_kv_operands(blk(i, *kcol), blk(i, *vcol))
                                                           for i in range(nblk)]
    band, kj = _band_mask(WINDOW, WINDOW)
    for i in range(nblk):
        mask = band & (jnp.logical_not(first) | (kj >= WINDOW)) if i == 0 else band
        kv_ops = [tuple(jnp.concatenate([ops[i][kv][n], ops[i + 1][kv][n]], axis=0) for n in range(3))
                  for kv in range(KV_HEADS)]
        _attend(blk(i, 0, ATT_WIDTH).astype(BF16), kv_ops, mask, sink_ref,
                att_ref.at[i * WINDOW:(i + 1) * WINDOW, :])
    kprev_ref[...] = blk(nblk - 1, *kcol)
    vprev_ref[...] = blk(nblk - 1, *vcol)


def _prep_math(f, prev, params, outs):
    (mu_ref, w0_ref, w2_ref, a0_ref, a2_ref, g2_ref, kk_ref, ka_ref, rk_ref, ones_ref) = params
    (r_out, l_out, k_out, v_out, kk_out, kb_out, g_out, bo_out) = outs
    xs = f + (prev - f) * mu_ref[...]
    w3 = 3 * RWKV_WIDTH
    r = xs[:, :RWKV_WIDTH]
    k = xs[:, RWKV_WIDTH:2 * RWKV_WIDTH]
    v = xs[:, 2 * RWKV_WIDTH:w3]
    wl = xs[:, w3:w3 + DECAY_LORA]
    al = xs[:, w3 + DECAY_LORA:w3 + DECAY_LORA + ICL_LORA]
    gl = xs[:, w3 + DECAY_LORA + ICL_LORA:]
    log_decay = -DECAY_SCALE * _sigmoid(w0_ref[...] + _dot(jnp.tanh(wl).astype(BF16), w2_ref[...]))
    a = _sigmoid(a0_ref[...] + _dot(al.astype(BF16), a2_ref[...]))
    g = _dot(_sigmoid(gl).astype(BF16), g2_ref[...])
    ones_bd = ones_ref[...]
    kk = k * kk_ref[...]
    kk = kk * lax.rsqrt(_seg_sum(kk * kk, ones_bd) + L2_EPS)
    k2 = k * (1.0 + (a - 1.0) * ka_ref[...])
    r_out[...] = r
    l_out[...] = log_decay
    k_out[...] = k2
    v_out[...] = v
    kk_out[...] = kk
    kb_out[...] = kk * a
    g_out[...] = g
    bo_out[...] = _seg_sum(r * k2 * rk_ref[...], ones_bd) * v


def _prep_params(p):
    w = RWKV_WIDTH
    ins = [p["mu"], p["w0"], p["w2"], p["a0"], p["a2"], p["g2"], p["k_k"], p["k_a"], p["r_k"], p["ones_bd"]]
    specs = [_const_spec((1, RWKV_BLOCK)), _const_spec((1, w)), _const_spec((DECAY_LORA, w)),
             _const_spec((1, w)), _const_spec((ICL_LORA, w)), _const_spec((GATE_LORA, w)),
             _const_spec((1, w)), _const_spec((1, w)), _const_spec((1, w)),
             _const_spec((LANES, LANES))]
    assert len(ins) == N_PREP_PARAMS
    return ins, specs


def _inproj_prep(x, p, *, tm, blocks_per_seq):
    n = x.shape[0]
    assert tm % WINDOW == 0
    row = lambda w: pl.BlockSpec((tm, w), lambda i: (i, 0))
    w = RWKV_WIDTH
    prm, prm_specs = _prep_params(p)
    return pl.pallas_call(
        functools.partial(_inproj_prep_kernel, blocks_per_seq=blocks_per_seq),
        grid=(n // tm,),
        in_specs=[pl.BlockSpec(memory_space=pltpu.SMEM), row(D_MODEL), _const_spec((1, D_MODEL)),
                  _const_spec((D_MODEL, PROJ_WIDTH))] + prm_specs,
        out_specs=[row(ATT_WIDTH), row(KV_WIDTH), row(KV_WIDTH), row(N_SCAN_COLS * w), row(N_GATE_COLS * w)],
        out_shape=[jax.ShapeDtypeStruct((n, ATT_WIDTH), BF16), jax.ShapeDtypeStruct((n, KV_WIDTH), F32),
                   jax.ShapeDtypeStruct((n, KV_WIDTH), F32), jax.ShapeDtypeStruct((n, N_SCAN_COLS * w), F32),
                   jax.ShapeDtypeStruct((n, N_GATE_COLS * w), F32)],
        scratch_shapes=[pltpu.VMEM((8, RWKV_BLOCK), F32), pltpu.VMEM((WINDOW, KV_WIDTH), F32),
                        pltpu.VMEM((WINDOW, KV_WIDTH), F32)],
        compiler_params=_params("arbitrary"),
        name="inproj_prep_attn",
    )(p["sinks"], x, p["g_mix"].reshape(1, D_MODEL), p["w_in"], *prm)


def _prep(feat, feat_before, p, *, t, tm):
    n = feat.shape[0]
    assert tm % t == 0
    row = lambda w: pl.BlockSpec((tm, w), lambda i: (i, 0))
    w = RWKV_WIDTH
    prm, prm_specs = _prep_params(p)
    return pl.pallas_call(
        functools.partial(_prep_kernel, t=t),
        grid=(n // tm,),
        in_specs=[row(RWKV_BLOCK), pl.BlockSpec((tm // t, 1, RWKV_BLOCK), lambda i: (i, 0, 0))] + prm_specs,
        out_specs=[row(N_SCAN_COLS * w), row(N_GATE_COLS * w)],
        out_shape=[jax.ShapeDtypeStruct((n, N_SCAN_COLS * w), F32), jax.ShapeDtypeStruct((n, N_GATE_COLS * w), F32)],
        compiler_params=_params("arbitrary"),
        name="rwkv_prep",
    )(feat, feat_before.reshape(n // t, 1, RWKV_BLOCK), *prm)


def _kv_operands(kblk, vblk):
    lo_m = lax.broadcasted_iota(jnp.int32, (1, LANES), 1) < HEAD_DIM
    k_sw = pltpu.roll(kblk, HEAD_DIM, 1)
    v_sw = pltpu.roll(vblk, HEAD_DIM, 1)
    ops = []
    for kv in range(KV_HEADS):
        own = lo_m if kv == 0 else ~lo_m
        kdup = jnp.where(own, kblk, k_sw).astype(BF16)
        v_own = jnp.where(own, vblk, 0.0).astype(BF16)
        v_oth = jnp.where(own, 0.0, v_sw).astype(BF16)
        ops.append((kdup,) + ((v_own, v_oth) if kv == 0 else (v_oth, v_own)))
    return ops


def _attend(q, kv_ops, mask, sink_ref, o_ref):
    nq = q.shape[0]
    lo_m = lax.broadcasted_iota(jnp.int32, (1, LANES), 1) < HEAD_DIM
    mask2 = jnp.concatenate([mask, mask], axis=0)
    rowi = lax.broadcasted_iota(jnp.int32, (2 * nq, 1), 0)
    for kv in range(KV_HEADS):
        kdup, v_lo, v_hi = kv_ops[kv]
        for pr in range(2):
            c0 = (kv * 2 + pr) * LANES
            qp = q[:, c0:c0 + LANES].astype(F32) * (HEAD_DIM ** -0.5)
            qs = jnp.concatenate([jnp.where(lo_m, qp, 0.0), jnp.where(lo_m, 0.0, qp)], axis=0)
            s = jnp.where(mask2, _dot(qs.astype(BF16), kdup, NT), NEG_INF)
            h0 = (kv * 2 + pr) * 2
            sink = jnp.where(rowi < nq, sink_ref[h0], sink_ref[h0 + 1])
            m = jnp.maximum(jnp.max(s, axis=-1, keepdims=True), sink)
            e = jnp.exp(s - m)
            rden = 1.0 / (jnp.sum(e, axis=-1, keepdims=True) + jnp.exp(sink - m))
            out = _dot(e[:nq].astype(BF16), v_lo) + _dot(e[nq:].astype(BF16), v_hi)
            out = out * jnp.where(lo_m, rden[:nq], rden[nq:])
            o_ref[:, c0:c0 + LANES] = out.astype(o_ref.dtype)


def _band_mask(nq, q0):
    qi = lax.broadcasted_iota(jnp.int32, (nq, 2 * WINDOW), 0) + q0
    kj = lax.broadcasted_iota(jnp.int32, (nq, 2 * WINDOW), 1)
    return (kj <= qi) & (qi - kj < WINDOW), kj


def _attn_sample_kernel(sink_ref, q_ref, kn_ref, vn_ref, ck_ref, cv_ref, o_ref, kw_ref, vw_ref,
                        kall_ref, vall_ref, *, seqs, t):
    wb = ck_ref.shape[1]

    @pl.when(pl.program_id(0) == 0)
    def _():
        kall_ref[...] = jnp.zeros_like(kall_ref)
        vall_ref[...] = jnp.zeros_like(vall_ref)

    mask, _ = _band_mask(t, wb)
    for s in range(seqs):
        rows = slice(s * t, (s + 1) * t)
        kall_ref[0:wb, :] = ck_ref[s]
        vall_ref[0:wb, :] = cv_ref[s]
        kall_ref[wb:wb + t, :] = kn_ref[rows, :]
        vall_ref[wb:wb + t, :] = vn_ref[rows, :]
        _attend(q_ref[rows, :], _kv_operands(kall_ref[...], vall_ref[...]), mask, sink_ref, o_ref.at[rows, :])
        kw_ref[s] = kall_ref[t:t + wb, :]
        vw_ref[s] = vall_ref[t:t + wb, :]


def _attn_sample(q, k, v, cache_k, cache_v, sinks, *, t, seqs):
    b, wb, _ = cache_k.shape
    n = b * t
    row = lambda w: pl.BlockSpec((seqs * t, w), lambda i: (i, 0))
    win = pl.BlockSpec((seqs, wb, KV_WIDTH), lambda i: (i, 0, 0))
    return pl.pallas_call(
        functools.partial(_attn_sample_kernel, seqs=seqs, t=t),
        grid=(b // seqs,),
        in_specs=[pl.BlockSpec(memory_space=pltpu.SMEM), row(ATT_WIDTH), row(KV_WIDTH), row(KV_WIDTH), win, win],
        out_specs=[row(ATT_WIDTH), win, win],
        out_shape=[jax.ShapeDtypeStruct((n, ATT_WIDTH), F32),
                   jax.ShapeDtypeStruct((b, wb, KV_WIDTH), F32),
                   jax.ShapeDtypeStruct((b, wb, KV_WIDTH), F32)],
        scratch_shapes=[pltpu.VMEM((2 * WINDOW, KV_WIDTH), F32), pltpu.VMEM((2 * WINDOW, KV_WIDTH), F32)],
        compiler_params=_params("arbitrary"),
        name="attn_sample",
    )(sinks, q, k, v, cache_k, cache_v)


def _wkv_kernel(in_ref, s0_ref, o_ref, st_ref, z_ref, *, nseq, nchunks, rows, real):
    c = rows
    pw = LANES
    j = pl.program_id(1)
    npairs = RWKV_HEADS // 2
    pairs = range(npairs)

    @pl.when(j == 0)
    def _():
        for s in range(nseq):
            for p in pairs:
                z_ref[s * npairs + p] = jnp.concatenate([s0_ref[s, 2 * p], s0_ref[s, 2 * p + 1]], axis=0).T

    def pair_iota(nrow):
        return (lax.broadcasted_iota(jnp.int32, (nrow, pw), 0),
                lax.broadcasted_iota(jnp.int32, (nrow, pw), 1) & (HEAD_DIM - 1))

    row, col = pair_iota(c)
    lo = lax.broadcasted_iota(jnp.int32, (1, pw), 1) < HEAD_DIM
    lower2 = jnp.concatenate([row > col, row >= col], axis=0)
    eye = (row == col).astype(F32)
    krow, kcol = pair_iota(HEAD_DIM)
    eye_ch = (krow == kcol).astype(F32)
    ri = lax.broadcasted_iota(jnp.int32, (c, c), 0)
    tri = (ri >= lax.broadcasted_iota(jnp.int32, (c, c), 1)).astype(BF16)

    def bd(y):
        zero = jnp.zeros_like(y)
        parts = [jnp.where(lo, y, zero), jnp.where(lo, zero, y)]
        if y.shape[0] < HEAD_DIM:
            fill = jnp.zeros((HEAD_DIM - y.shape[0], pw), y.dtype)
            parts = [parts[0], fill, parts[1], fill]
        return jnp.concatenate(parts, axis=0)

    memo = {}

    def cached(x, tag, build):
        key = (id(x), tag)
        if key not in memo:
            memo[key] = (x, build())
        return memo[key][1]

    def split(x):
        return cached(x, "split", lambda: _split2(x))

    def lhs(x, passes, axis=1):
        hi, low = split(x)
        return cached(x, ("lhs", passes, axis), lambda: jnp.concatenate([hi, low, hi][:passes], axis=axis))

    def rhs(y, passes, axis=0):
        hi, low = split(y)

        def build():
            bh = bd(hi)
            return jnp.concatenate([bh, bh, bd(low)][:passes], axis=axis)
        return cached(y, ("rhs", passes, axis), build)

    def mm(x, y, group):
        return _dot(lhs(x, WKV_PASSES[group]), rhs(y, WKV_PASSES[group]))

    def mm_nt(x, y, group):
        return _dot(lhs(x, WKV_PASSES[group]), rhs(y, WKV_PASSES[group], 1), NT)

    def mm_tn(x, y, group):
        passes = WKV_PASSES[group]
        yh, yl = split(y)
        full = _dot(lhs(x, passes, 0), jnp.concatenate([yh, yh, yl][:passes], axis=0), TN)
        return jnp.where(lo, full[:HEAD_DIM], full[HEAD_DIM:])

    prep = {}
    for s in range(nseq):
        for q in range(nchunks):
            tr = slice(q * real, (q + 1) * real)

            def tok(col):
                x = in_ref[s, tr, col * RWKV_WIDTH:(col + 1) * RWKV_WIDTH]
                return x if real == c else jnp.concatenate([x, jnp.zeros((c - real, x.shape[1]), x.dtype)], axis=0)

            lw = tok(1)
            cum = sum(_dot(tri, part) for part in _split3(lw))
            tot = cum[c - 1:c, :]
            en = jnp.exp(-cum)
            ed = jnp.exp(tot - cum)
            kk = tok(4)
            kb = tok(5)
            kx = tok(2)
            prep[s, q] = dict(a=-kk * jnp.exp(cum - lw), r=tok(0) * jnp.exp(cum), b=kb * en, k=kx * en,
                              bh=kb * ed, kh=kx * ed, v=tok(3), etot=jnp.exp(tot))

    items = [(s, q, p) for s in range(nseq) for q in range(nchunks) for p in pairs]
    ps = lambda name, it: prep[it[0], it[1]][name][:, it[2] * pw:(it[2] + 1) * pw]
    each = lambda f: [f(i) for i in range(len(items))]
    pv = {name: each(lambda i: ps(name, items[i])) for name in ("a", "r", "b", "k", "bh", "kh", "v", "etot")}
    ar = each(lambda i: jnp.concatenate([pv["a"][i], pv["r"][i]], axis=0))
    g_b = each(lambda i: jnp.where(lower2, mm_nt(ar[i], pv["b"][i], "gram"), 0.0))
    g_k = each(lambda i: jnp.where(lower2, mm_nt(ar[i], pv["k"][i], "gram"), 0.0))
    a_rb = each(lambda i: g_b[i][c:])
    p_m = each(lambda i: g_b[i][:c])
    t_m = each(lambda i: eye + p_m[i])
    n = 2
    while n < c:
        p_m = each(lambda i: mm(p_m[i], p_m[i], "inverse"))
        t_m = each(lambda i: t_m[i] + mm(p_m[i], t_m[i], "inverse"))
        n *= 2
    gv = each(lambda i: mm(g_k[i], pv["v"][i], "apply"))
    av = each(lambda i: gv[i][:c])
    w_m = each(lambda i: mm(t_m[i], pv["a"][i], "apply"))
    u_m = each(lambda i: mm(t_m[i], av[i], "apply"))
    r_p = each(lambda i: pv["r"][i] + mm(a_rb[i], w_m[i], "apply"))
    o_i = each(lambda i: mm(a_rb[i], u_m[i], "apply") + gv[i][c:])
    m_m = each(lambda i: mm_tn(pv["bh"][i], w_m[i], "state") + eye_ch * pv["etot"][i])
    bk = each(lambda i: jnp.concatenate([pv["bh"][i], pv["kh"][i]], axis=0))
    uv = each(lambda i: jnp.concatenate([u_m[i], pv["v"][i]], axis=0))
    n_m = each(lambda i: mm_tn(bk[i], uv[i], "state"))
    rm = each(lambda i: jnp.concatenate([r_p[i], m_m[i]], axis=0))
    z = [z_ref[sp] for sp in range(nseq * npairs)]
    for q in range(nchunks):
        for s in range(nseq):
            for p in pairs:
                i = items.index((s, q, p))
                sp = s * npairs + p
                oz = mm(rm[i], z[sp], "carry")
                o_ref[s, q * real:(q + 1) * real, p * pw:(p + 1) * pw] = (oz[:c] + o_i[i])[:real]
                z[sp] = oz[c:] + n_m[i]
    for sp in range(nseq * npairs):
        z_ref[sp] = z[sp]

    @pl.when(j == pl.num_programs(1) - 1)
    def _():
        for s in range(nseq):
            for p in pairs:
                zt = z_ref[s * npairs + p].T
                st_ref[s, 2 * p] = zt[:HEAD_DIM]
                st_ref[s, 2 * p + 1] = zt[HEAD_DIM:]


def _wkv(scan_in, s0, *, nseq, nchunks, rows, real=None):
    real = rows if real is None else real
    assert real == rows or nchunks == 1
    b, t, wide = scan_in.shape
    w = RWKV_WIDTH
    assert wide == N_SCAN_COLS * w
    blk = nchunks * real
    tok = lambda width: pl.BlockSpec((nseq, blk, width), lambda bi, j: (bi, j, 0))
    st = pl.BlockSpec((nseq, RWKV_HEADS, HEAD_DIM, HEAD_DIM), lambda bi, j: (bi, 0, 0, 0))
    return pl.pallas_call(
        functools.partial(_wkv_kernel, nseq=nseq, nchunks=nchunks, rows=rows, real=real),
        grid=(b // nseq, t // blk),
        in_specs=[tok(wide), st],
        out_specs=[tok(w), st],
        out_shape=[jax.ShapeDtypeStruct((b, t, w), F32),
                   jax.ShapeDtypeStruct((b, RWKV_HEADS, HEAD_DIM, HEAD_DIM), F32)],
        scratch_shapes=[pltpu.VMEM((nseq * RWKV_HEADS // 2, HEAD_DIM, LANES), F32)],
        compiler_params=_params("arbitrary", "arbitrary"),
        name="wkv_scan",
    )(scan_in, s0)


def _post_kernel(o_ref, gate_ref, att_ref, x_ref, gng_ref, gnb_ref, ones_ref, wo_ref, gf_ref,
                 wg_ref, wu_ref, wd_ref, gfin_ref, y_ref, *, final):
    tm = x_ref.shape[0]
    sub = min(tm, POST_SUB_ROWS)
    parts = [slice(i * sub, (i + 1) * sub) for i in range(tm // sub)]
    each = lambda f: [f(i) for i in range(len(parts))]
    ones_bd = ones_ref[...]
    o = each(lambda i: o_ref[parts[i], :])
    mean = each(lambda i: _seg_sum(o[i], ones_bd) * (1.0 / HEAD_DIM))
    oc = each(lambda i: o[i] - mean[i])
    var = each(lambda i: _seg_sum(oc[i] * oc[i], ones_bd) * (1.0 / HEAD_DIM))
    rw = each(lambda i: ((oc[i] * lax.rsqrt(var[i] + GN_EPS) * gng_ref[...] + gnb_ref[...]
                          + gate_ref[parts[i], RWKV_WIDTH:]) * gate_ref[parts[i], :RWKV_WIDTH]).astype(BF16))
    x1 = each(lambda i: x_ref[parts[i], :] + _dot(att_ref[parts[i], :].astype(BF16), wo_ref[:ATT_WIDTH, :])
              + _dot(rw[i], wo_ref[ATT_WIDTH:, :]))
    u = each(lambda i: _rms(x1[i], gf_ref[...]).astype(BF16))
    gate = each(lambda i: _dot(u[i], wg_ref[...]))
    up = each(lambda i: _dot(u[i], wu_ref[...]))
    hid = each(lambda i: (gate[i] * _sigmoid(gate[i]) * up[i]).astype(BF16))
    x2 = each(lambda i: x1[i] + _dot(hid[i], wd_ref[...]))
    for i in range(len(parts)):
        y_ref[parts[i], :] = _rms(x2[i], gfin_ref[...]) if final else x2[i]


def _post(o, gate, att, x, p, *, tm, final):
    n = x.shape[0]
    d_ff = p["w_gate"].shape[1]
    row = lambda w: pl.BlockSpec((tm, w), lambda i: (i, 0))
    once = lambda shape: pl.BlockSpec(shape, lambda i: (0,) * len(shape), pipeline_mode=pl.Buffered(1))
    w = RWKV_WIDTH
    return pl.pallas_call(
        functools.partial(_post_kernel, final=final),
        grid=(n // tm,),
        in_specs=[row(w), row(N_GATE_COLS * w), row(ATT_WIDTH), row(D_MODEL),
                  once((1, w)), once((1, w)), once((LANES, LANES)), once((ATT_WIDTH + w, D_MODEL)),
                  once((1, D_MODEL)),
                  once((D_MODEL, d_ff)), once((D_MODEL, d_ff)), once((d_ff, D_MODEL)), once((1, D_MODEL))],
        out_specs=row(D_MODEL),
        out_shape=jax.ShapeDtypeStruct((n, D_MODEL), F32),
        compiler_params=_params("arbitrary"),
        name="post_ffn",
    )(o, gate, att, x, p["gn_g"], p["gn_b"], p["ones_bd"], p["w_out"], p["g_ffn"],
      p["w_gate"], p["w_up"], p["w_down"], p["g_final"])


def _layer_params(l, g_mix, w_in, attn_sinks, rwkv_mu, w0, w2, a0, a2, g2, k_k, k_a, r_k, gn_g, gn_b,
                  w_out, g_ffn, w_gate, w_up, w_down, g_final):
    vec = lambda a: a.reshape(1, -1).astype(F32)
    hd = jnp.arange(LANES) // HEAD_DIM
    return dict(
        g_mix=g_mix[l], w_in=w_in[l].astype(BF16), sinks=attn_sinks[l].astype(F32),
        mu=vec(rwkv_mu[l]), w0=vec(w0[l]), w2=w2[l].astype(BF16), a0=vec(a0[l]), a2=a2[l].astype(BF16),
        g2=g2[l].astype(BF16), k_k=vec(k_k[l]), k_a=vec(k_a[l]), r_k=vec(r_k[l]),
        gn_g=vec(gn_g[l]), gn_b=vec(gn_b[l]), w_out=w_out[l].astype(BF16), g_ffn=vec(g_ffn[l]),
        w_gate=w_gate[l].astype(BF16), w_up=w_up[l].astype(BF16), w_down=w_down[l].astype(BF16),
        g_final=vec(g_final), ones_bd=(hd[:, None] == hd[None, :]).astype(BF16),
    )


def _pick(n, pref):
    t = pref
    while n % t:
        t //= 2
    return t


def _pick_rows(n, cap, unit):
    assert n % unit == 0
    return unit * max(d for d in range(1, cap // unit + 1) if (n // unit) % d == 0)


def _prompt_layer(x, p, final):
    b, t, d = x.shape
    n = b * t
    x2 = x.reshape(n, d)
    tm = _pick(t, INPROJ_ROWS)
    att, k, v, scan_in, gate = _inproj_prep(x2, p, tm=tm, blocks_per_seq=t // tm)
    s0 = jnp.zeros((b, RWKV_HEADS, HEAD_DIM, HEAD_DIM), F32)
    o, s_new = _wkv(scan_in.reshape(b, t, -1), s0, nseq=_pick(b, WKV_PROMPT_SEQS),
                    nchunks=_pick(t // HEAD_DIM, WKV_PROMPT_CHUNKS), rows=HEAD_DIM)
    y = _post(o.reshape(n, -1), gate, att, x2, p, tm=_pick(n, POST_ROWS), final=final)
    wp = min(WINDOW, t)
    k_win = k.reshape(b, t, KV_WIDTH)[:, t - wp:].reshape(b, wp, KV_HEADS, HEAD_DIM)
    v_win = v.reshape(b, t, KV_WIDTH)[:, t - wp:].reshape(b, wp, KV_HEADS, HEAD_DIM)
    shift = _rms_rows(x[:, -1], p["g_mix"])
    return y.reshape(b, t, d), k_win, v_win, s_new, shift


def _sample_layer(x, h_prev, k_buf, v_buf, s0, p, final):
    b, t, d = x.shape
    n = b * t
    x2 = x.reshape(n, d)
    wb = k_buf.shape[1]
    rows_all = jnp.concatenate([x2, h_prev.astype(x2.dtype)], axis=0)
    q, k, v, feat_all = _inproj(rows_all, p["g_mix"], p["w_in"], norm_rows=n, tm=_pick_rows(n + b, 512, 8))
    feat, feat_prev = feat_all[:n], feat_all[n:]
    scan_in, gate = _prep(feat, feat_prev, p, t=t, tm=_pick(n, 512))
    att, k_win, v_win = _attn_sample(q, k, v, k_buf.reshape(b, wb, -1), v_buf.reshape(b, wb, -1),
                                     p["sinks"], t=t, seqs=_pick(b, 8))
    tp = -(-t // BF16_ROWS) * BF16_ROWS
    assert tp <= HEAD_DIM and t % 8 == 0
    o, s_new = _wkv(scan_in.reshape(b, t, -1), s0, nseq=_pick(b, WKV_SAMPLE_SEQS), nchunks=1, rows=tp, real=t)
    y = _post(o.reshape(n, -1), gate, att, x2, p, tm=_pick(n, POST_ROWS), final=final)
    shift = _rms_rows(x[:, -1], p["g_mix"])
    return (y.reshape(b, t, d), k_win.reshape(b, wb, KV_HEADS, HEAD_DIM), v_win.reshape(b, wb, KV_HEADS, HEAD_DIM),
            s_new, shift)


def kernel(x_prompt, x_sample, cache_k, cache_v, state_wkv, state_shift, g_mix, w_in, attn_sinks, rwkv_mu, w0, w2,
           a0, a2, g2, k_k, k_a, r_k, gn_g, gn_b, w_out, g_ffn, w_gate, w_up, w_down, g_final):
    depth = w_in.shape[0]
    xp, xs = x_prompt, x_sample
    outs_p, outs_s = [], []
    for l in range(depth):
        p = _layer_params(l, g_mix, w_in, attn_sinks, rwkv_mu, w0, w2, a0, a2, g2, k_k, k_a, r_k, gn_g, gn_b,
                          w_out, g_ffn, w_gate, w_up, w_down, g_final)
        final = l == depth - 1
        xp, kp, vp, sp, hp = _prompt_layer(xp, p, final)
        xs, kn, vn, sn, hn = _sample_layer(xs, state_shift[l], cache_k[l], cache_v[l], state_wkv[l], p, final)
        outs_p.append((kp, vp, sp, hp))
        outs_s.append((kn, vn, sn, hn))
    stack = lambda outs, i: jnp.stack([o[i] for o in outs])
    return (xp, xs,
            stack(outs_p, 0), stack(outs_p, 1), stack(outs_p, 2), stack(outs_p, 3),
            stack(outs_s, 0), stack(outs_s, 1), stack(outs_s, 2), stack(outs_s, 3))
```

```python
import functools
import math

import jax
import jax.numpy as jnp
from jax import lax
from jax.experimental import pallas as pl
from jax.experimental.pallas import tpu as pltpu

F32 = jnp.float32
BF16 = jnp.bfloat16

D_MODEL = 1024
HEAD_DIM = 64
ATT_HEADS = 8
KV_HEADS = 2
ATT_WIDTH = ATT_HEADS * HEAD_DIM
KV_WIDTH = KV_HEADS * HEAD_DIM
RWKV_HEADS = 8
RWKV_WIDTH = RWKV_HEADS * HEAD_DIM
WINDOW = 128
DECAY_LORA = 64
ICL_LORA = 64
GATE_LORA = 128
RWKV_BLOCK = 3 * RWKV_WIDTH + DECAY_LORA + ICL_LORA + GATE_LORA
ATT_PROJ = ATT_WIDTH + 2 * KV_WIDTH
PROJ_WIDTH = ATT_PROJ + RWKV_BLOCK
RMS_EPS = 1e-6
GN_EPS = 64e-5
L2_EPS = 1e-12
NEG_INF = -1e30
DECAY_SCALE = math.exp(-0.5)

V7X_VMEM_BYTES = 64 * 1024 * 1024
VMEM_LIMIT_BYTES = V7X_VMEM_BYTES * 3 // 4
LANES = 128

WKV_PROMPT_SEQS = 2
WKV_PROMPT_CHUNKS = 4
WKV_SAMPLE_SEQS = 8
WKV_PASSES = dict(gram=1, inverse=1, apply=1, state=3, carry=3)
BF16_ROWS = 16
POST_ROWS = 512
POST_SUB_ROWS = 256
INPROJ_ROWS = 512
INPROJ_SUB_ROWS = 256

NN = (((1,), (0,)), ((), ()))
NT = (((1,), (1,)), ((), ()))
TN = (((0,), (0,)), ((), ()))


def _dot(a, b, dn=NN):
    return lax.dot_general(a, b, dn, preferred_element_type=F32)


def _split2(x):
    hi = x.astype(BF16)
    lo = (x - hi.astype(F32)).astype(BF16)
    return hi, lo


def _split3(x):
    hi = x.astype(BF16)
    r1 = x - hi.astype(F32)
    mid = r1.astype(BF16)
    lo = (r1 - mid.astype(F32)).astype(BF16)
    return hi, mid, lo


def _seg_sum(x, ones_bd):
    hi, lo = _split2(x)
    slabs = [slice(i, i + LANES) for i in range(0, x.shape[1], LANES)]
    return jnp.concatenate([_dot(hi[:, s], ones_bd) + _dot(lo[:, s], ones_bd) for s in slabs], axis=1)


def _rms(x, g):
    return x * lax.rsqrt(jnp.mean(x * x, axis=-1, keepdims=True) + RMS_EPS) * g


def _sigmoid(z):
    return 1.0 / (1.0 + jnp.exp(-z))


def _const_spec(shape):
    return pl.BlockSpec(shape, lambda *_: (0,) * len(shape))


def _params(*sem):
    return pltpu.CompilerParams(dimension_semantics=sem, vmem_limit_bytes=VMEM_LIMIT_BYTES)


def _rms_rows_kernel(x_ref, g_ref, o_ref):
    o_ref[...] = _rms(x_ref[...], g_ref[...])


def _rms_rows(x, g):
    n, d = x.shape
    return pl.pallas_call(
        _rms_rows_kernel,
        out_shape=jax.ShapeDtypeStruct((n, d), F32),
        name="rms_rows",
    )(x, g.reshape(1, d))


def _inproj_kernel(x_ref, g_ref, w_ref, q_ref, k_ref, v_ref, f_ref, *, norm_rows):
    x = x_ref[...]
    row = lax.broadcasted_iota(jnp.int32, (x.shape[0], 1), 0) + pl.program_id(0) * x.shape[0]
    h = jnp.where(row < norm_rows, _rms(x, g_ref[...]), x)
    proj = _dot(h.astype(BF16), w_ref[...])
    q_ref[...] = proj[:, :ATT_WIDTH]
    k_ref[...] = proj[:, ATT_WIDTH:ATT_WIDTH + KV_WIDTH]
    v_ref[...] = proj[:, ATT_WIDTH + KV_WIDTH:ATT_PROJ]
    f_ref[...] = proj[:, ATT_PROJ:]


def _inproj(x, g_mix, w_in_bf, *, norm_rows, tm):
    n = x.shape[0]
    row = lambda w: pl.BlockSpec((tm, w), lambda i: (i, 0))
    return pl.pallas_call(
        functools.partial(_inproj_kernel, norm_rows=norm_rows),
        grid=(n // tm,),
        in_specs=[row(D_MODEL), _const_spec((1, D_MODEL)), _const_spec((D_MODEL, PROJ_WIDTH))],
        out_specs=[row(ATT_WIDTH), row(KV_WIDTH), row(KV_WIDTH), row(RWKV_BLOCK)],
        out_shape=[
            jax.ShapeDtypeStruct((n, ATT_WIDTH), F32),
            jax.ShapeDtypeStruct((n, KV_WIDTH), F32),
            jax.ShapeDtypeStruct((n, KV_WIDTH), F32),
            jax.ShapeDtypeStruct((n, RWKV_BLOCK), F32),
        ],
        compiler_params=_params("arbitrary"),
        name="inproj",
    )(x, g_mix.reshape(1, D_MODEL), w_in_bf)


N_PREP_PARAMS = 10
N_SCAN_COLS = 6
N_GATE_COLS = 2


def _prep_out_views(scan_ref, gate_ref, rows):
    w = RWKV_WIDTH
    return ([scan_ref.at[rows, c * w:(c + 1) * w] for c in range(N_SCAN_COLS)]
            + [gate_ref.at[rows, c * w:(c + 1) * w] for c in range(N_GATE_COLS)])


def _carried_prev(f, carry_ref):
    tm = f.shape[0]
    rolled = pltpu.roll(f, 1, 0)
    row = lax.broadcasted_iota(jnp.int32, f.shape, 0)
    prev = jnp.where(row == 0, carry_ref[0:1, :], rolled)
    carry_ref[0:1, :] = f[tm - 1:tm, :]
    return prev


def _prep_kernel(f_ref, p_ref, *refs, t):
    f = f_ref[...]
    tm = f.shape[0]
    before = jnp.broadcast_to(p_ref[...], (tm // t, t, f.shape[1])).reshape(tm, f.shape[1])
    row = lax.broadcasted_iota(jnp.int32, f.shape, 0)
    prev = jnp.where(row % t == 0, before, pltpu.roll(f, 1, 0))
    scan_ref, gate_ref = refs[N_PREP_PARAMS:N_PREP_PARAMS + 2]
    _prep_math(f, prev, refs[:N_PREP_PARAMS], _prep_out_views(scan_ref, gate_ref, slice(None)))


def _inproj_prep_kernel(sink_ref, x_ref, g_ref, w_ref, *refs, blocks_per_seq):
    att_ref, k_ref, v_ref = refs[N_PREP_PARAMS:N_PREP_PARAMS + 3]
    scan_ref, gate_ref = refs[N_PREP_PARAMS + 3:N_PREP_PARAMS + 5]
    carry_ref, kprev_ref, vprev_ref = refs[N_PREP_PARAMS + 5:]
    first = pl.program_id(0) % blocks_per_seq == 0

    @pl.when(first)
    def _():
        carry_ref[...] = jnp.zeros_like(carry_ref)
        kprev_ref[...] = jnp.zeros_like(kprev_ref)
        vprev_ref[...] = jnp.zeros_like(vprev_ref)

    tm = x_ref.shape[0]
    sub = min(tm, INPROJ_SUB_ROWS)
    parts = [slice(i * sub, (i + 1) * sub) for i in range(tm // sub)]
    proj = [_dot(_rms(x_ref[rows, :], g_ref[...]).astype(BF16), w_ref[...]) for rows in parts]
    for rows, pj in zip(parts, proj):
        k_ref[rows, :] = pj[:, ATT_WIDTH:ATT_WIDTH + KV_WIDTH]
        v_ref[rows, :] = pj[:, ATT_WIDTH + KV_WIDTH:ATT_PROJ]
    for rows, pj in zip(parts, proj):
        f = pj[:, ATT_PROJ:]
        _prep_math(f, _carried_prev(f, carry_ref), refs[:N_PREP_PARAMS], _prep_out_views(scan_ref, gate_ref, rows))

    def blk(i, c0, c1):
        r0 = i * WINDOW
        return proj[r0 // sub][r0 % sub:r0 % sub + WINDOW, c0:c1]

    nblk = tm // WINDOW
    kcol, vcol = (ATT_WIDTH, ATT_WIDTH + KV_WIDTH), (ATT_WIDTH + KV_WIDTH, ATT_PROJ)
    ops = [_kv_operands(kprev_ref[...], vprev_ref[...])] + [_kv_operands(blk(i, *kcol), blk(i, *vcol))
                                                           for i in range(nblk)]
    band, kj = _band_mask(WINDOW, WINDOW)
    for i in range(nblk):
        mask = band & (jnp.logical_not(first) | (kj >= WINDOW)) if i == 0 else band
        kv_ops = [tuple(jnp.concatenate([ops[i][kv][n], ops[i + 1][kv][n]], axis=0) for n in range(3))
                  for kv in range(KV_HEADS)]
        _attend(blk(i, 0, ATT_WIDTH).astype(BF16), kv_ops, mask, sink_ref,
                att_ref.at[i * WINDOW:(i + 1) * WINDOW, :])
    kprev_ref[...] = blk(nblk - 1, *kcol)
    vprev_ref[...] = blk(nblk - 1, *vcol)


def _prep_math(f, prev, params, outs):
    (mu_ref, w0_ref, w2_ref, a0_ref, a2_ref, g2_ref, kk_ref, ka_ref, rk_ref, ones_ref) = params
    (r_out, l_out, k_out, v_out, kk_out, kb_out, g_out, bo_out) = outs
    xs = f + (prev - f) * mu_ref[...]
    w3 = 3 * RWKV_WIDTH
    r = xs[:, :RWKV_WIDTH]
    k = xs[:, RWKV_WIDTH:2 * RWKV_WIDTH]
    v = xs[:, 2 * RWKV_WIDTH:w3]
    wl = xs[:, w3:w3 + DECAY_LORA]
    al = xs[:, w3 + DECAY_LORA:w3 + DECAY_LORA + ICL_LORA]
    gl = xs[:, w3 + DECAY_LORA + ICL_LORA:]
    log_decay = -DECAY_SCALE * _sigmoid(w0_ref[...] + _dot(jnp.tanh(wl).astype(BF16), w2_ref[...]))
    a = _sigmoid(a0_ref[...] + _dot(al.astype(BF16), a2_ref[...]))
    g = _dot(_sigmoid(gl).astype(BF16), g2_ref[...])
    ones_bd = ones_ref[...]
    kk = k * kk_ref[...]
    kk = kk * lax.rsqrt(_seg_sum(kk * kk, ones_bd) + L2_EPS)
    k2 = k * (1.0 + (a - 1.0) * ka_ref[...])
    r_out[...] = r
    l_out[...] = log_decay
    k_out[...] = k2
    v_out[...] = v
    kk_out[...] = kk
    kb_out[...] = kk * a
    g_out[...] = g
    bo_out[...] = _seg_sum(r * k2 * rk_ref[...], ones_bd) * v


def _prep_params(p):
    w = RWKV_WIDTH
    ins = [p["mu"], p["w0"], p["w2"], p["a0"], p["a2"], p["g2"], p["k_k"], p["k_a"], p["r_k"], p["ones_bd"]]
    specs = [_const_spec((1, RWKV_BLOCK)), _const_spec((1, w)), _const_spec((DECAY_LORA, w)),
             _const_spec((1, w)), _const_spec((ICL_LORA, w)), _const_spec((GATE_LORA, w)),
             _const_spec((1, w)), _const_spec((1, w)), _const_spec((1, w)),
             _const_spec((LANES, LANES))]
    assert len(ins) == N_PREP_PARAMS
    return ins, specs


def _inproj_prep(x, p, *, tm, blocks_per_seq):
    n = x.shape[0]
    assert tm % WINDOW == 0
    row = lambda w: pl.BlockSpec((tm, w), lambda i: (i, 0))
    w = RWKV_WIDTH
    prm, prm_specs = _prep_params(p)
    return pl.pallas_call(
        functools.partial(_inproj_prep_kernel, blocks_per_seq=blocks_per_seq),
        grid=(n // tm,),
        in_specs=[pl.BlockSpec(memory_space=pltpu.SMEM), row(D_MODEL), _const_spec((1, D_MODEL)),
                  _const_spec((D_MODEL, PROJ_WIDTH))] + prm_specs,
        out_specs=[row(ATT_WIDTH), row(KV_WIDTH), row(KV_WIDTH), row(N_SCAN_COLS * w), row(N_GATE_COLS * w)],
        out_shape=[jax.ShapeDtypeStruct((n, ATT_WIDTH), BF16), jax.ShapeDtypeStruct((n, KV_WIDTH), F32),
                   jax.ShapeDtypeStruct((n, KV_WIDTH), F32), jax.ShapeDtypeStruct((n, N_SCAN_COLS * w), F32),
                   jax.ShapeDtypeStruct((n, N_GATE_COLS * w), F32)],
        scratch_shapes=[pltpu.VMEM((8, RWKV_BLOCK), F32), pltpu.VMEM((WINDOW, KV_WIDTH), F32),
                        pltpu.VMEM((WINDOW, KV_WIDTH), F32)],
        compiler_params=_params("arbitrary"),
        name="inproj_prep_attn",
    )(p["sinks"], x, p["g_mix"].reshape(1, D_MODEL), p["w_in"], *prm)


def _prep(feat, feat_before, p, *, t, tm):
    n = feat.shape[0]
    assert tm % t == 0
    row = lambda w: pl.BlockSpec((tm, w), lambda i: (i, 0))
    w = RWKV_WIDTH
    prm, prm_specs = _prep_params(p)
    return pl.pallas_call(
        functools.partial(_prep_kernel, t=t),
        grid=(n // tm,),
        in_specs=[row(RWKV_BLOCK), pl.BlockSpec((tm // t, 1, RWKV_BLOCK), lambda i: (i, 0, 0))] + prm_specs,
        out_specs=[row(N_SCAN_COLS * w), row(N_GATE_COLS * w)],
        out_shape=[jax.ShapeDtypeStruct((n, N_SCAN_COLS * w), F32), jax.ShapeDtypeStruct((n, N_GATE_COLS * w), F32)],
        compiler_params=_params("arbitrary"),
        name="rwkv_prep",
    )(feat, feat_before.reshape(n // t, 1, RWKV_BLOCK), *prm)


def _kv_operands(kblk, vblk):
    lo_m = lax.broadcasted_iota(jnp.int32, (1, LANES), 1) < HEAD_DIM
    k_sw = pltpu.roll(kblk, HEAD_DIM, 1)
    v_sw = pltpu.roll(vblk, HEAD_DIM, 1)
    ops = []
    for kv in range(KV_HEADS):
        own = lo_m if kv == 0 else ~lo_m
        kdup = jnp.where(own, kblk, k_sw).astype(BF16)
        v_own = jnp.where(own, vblk, 0.0).astype(BF16)
        v_oth = jnp.where(own, 0.0, v_sw).astype(BF16)
        ops.append((kdup,) + ((v_own, v_oth) if kv == 0 else (v_oth, v_own)))
    return ops


def _attend(q, kv_ops, mask, sink_ref, o_ref):
    nq = q.shape[0]
    lo_m = lax.broadcasted_iota(jnp.int32, (1, LANES), 1) < HEAD_DIM
    mask2 = jnp.concatenate([mask, mask], axis=0)
    rowi = lax.broadcasted_iota(jnp.int32, (2 * nq, 1), 0)
    for kv in range(KV_HEADS):
        kdup, v_lo, v_hi = kv_ops[kv]
        for pr in range(2):
            c0 = (kv * 2 + pr) * LANES
            qp = q[:, c0:c0 + LANES].astype(F32) * (HEAD_DIM ** -0.5)
            qs = jnp.concatenate([jnp.where(lo_m, qp, 0.0), jnp.where(lo_m, 0.0, qp)], axis=0)
            s = jnp.where(mask2, _dot(qs.astype(BF16), kdup, NT), NEG_INF)
            h0 = (kv * 2 + pr) * 2
            sink = jnp.where(rowi < nq, sink_ref[h0], sink_ref[h0 + 1])
            m = jnp.maximum(jnp.max(s, axis=-1, keepdims=True), sink)
            e = jnp.exp(s - m)
            rden = 1.0 / (jnp.sum(e, axis=-1, keepdims=True) + jnp.exp(sink - m))
            out = _dot(e[:nq].astype(BF16), v_lo) + _dot(e[nq:].astype(BF16), v_hi)
            out = out * jnp.where(lo_m, rden[:nq], rden[nq:])
            o_ref[:, c0:c0 + LANES] = out.astype(o_ref.dtype)


def _band_mask(nq, q0):
    qi = lax.broadcasted_iota(jnp.int32, (nq, 2 * WINDOW), 0) + q0
    kj = lax.broadcasted_iota(jnp.int32, (nq, 2 * WINDOW), 1)
    return (kj <= qi) & (qi - kj < WINDOW), kj


def _attn_sample_kernel(sink_ref, q_ref, kn_ref, vn_ref, ck_ref, cv_ref, o_ref, kw_ref, vw_ref,
                        kall_ref, vall_ref, *, seqs, t):
    wb = ck_ref.shape[1]

    @pl.when(pl.program_id(0) == 0)
    def _():
        kall_ref[...] = jnp.zeros_like(kall_ref)
        vall_ref[...] = jnp.zeros_like(vall_ref)

    mask, _ = _band_mask(t, wb)
    for s in range(seqs):
        rows = slice(s * t, (s + 1) * t)
        kall_ref[0:wb, :] = ck_ref[s]
        vall_ref[0:wb, :] = cv_ref[s]
        kall_ref[wb:wb + t, :] = kn_ref[rows, :]
        vall_ref[wb:wb + t, :] = vn_ref[rows, :]
        _attend(q_ref[rows, :], _kv_operands(kall_ref[...], vall_ref[...]), mask, sink_ref, o_ref.at[rows, :])
        kw_ref[s] = kall_ref[t:t + wb, :]
        vw_ref[s] = vall_ref[t:t + wb, :]


def _attn_sample(q, k, v, cache_k, cache_v, sinks, *, t, seqs):
    b, wb, _ = cache_k.shape
    n = b * t
    row = lambda w: pl.BlockSpec((seqs * t, w), lambda i: (i, 0))
    win = pl.BlockSpec((seqs, wb, KV_WIDTH), lambda i: (i, 0, 0))
    return pl.pallas_call(
        functools.partial(_attn_sample_kernel, seqs=seqs, t=t),
        grid=(b // seqs,),
        in_specs=[pl.BlockSpec(memory_space=pltpu.SMEM), row(ATT_WIDTH), row(KV_WIDTH), row(KV_WIDTH), win, win],
        out_specs=[row(ATT_WIDTH), win, win],
        out_shape=[jax.ShapeDtypeStruct((n, ATT_WIDTH), F32),
                   jax.ShapeDtypeStruct((b, wb, KV_WIDTH), F32),
                   jax.ShapeDtypeStruct((b, wb, KV_WIDTH), F32)],
        scratch_shapes=[pltpu.VMEM((2 * WINDOW, KV_WIDTH), F32), pltpu.VMEM((2 * WINDOW, KV_WIDTH), F32)],
        compiler_params=_params("arbitrary"),
        name="attn_sample",
    )(sinks, q, k, v, cache_k, cache_v)


def _wkv_kernel(in_ref, s0_ref, o_ref, st_ref, z_ref, *, nseq, nchunks, rows, real):
    c = rows
    pw = LANES
    j = pl.program_id(1)
    npairs = RWKV_HEADS // 2
    pairs = range(npairs)

    @pl.when(j == 0)
    def _():
        for s in range(nseq):
            for p in pairs:
                z_ref[s * npairs + p] = jnp.concatenate([s0_ref[s, 2 * p], s0_ref[s, 2 * p + 1]], axis=0).T

    def pair_iota(nrow):
        return (lax.broadcasted_iota(jnp.int32, (nrow, pw), 0),
                lax.broadcasted_iota(jnp.int32, (nrow, pw), 1) & (HEAD_DIM - 1))

    row, col = pair_iota(c)
    lo = lax.broadcasted_iota(jnp.int32, (1, pw), 1) < HEAD_DIM
    lower2 = jnp.concatenate([row > col, row >= col], axis=0)
    eye = (row == col).astype(F32)
    krow, kcol = pair_iota(HEAD_DIM)
    eye_ch = (krow == kcol).astype(F32)
    ri = lax.broadcasted_iota(jnp.int32, (c, c), 0)
    tri = (ri >= lax.broadcasted_iota(jnp.int32, (c, c), 1)).astype(BF16)

    def bd(y):
        zero = jnp.zeros_like(y)
        parts = [jnp.where(lo, y, zero), jnp.where(lo, zero, y)]
        if y.shape[0] < HEAD_DIM:
            fill = jnp.zeros((HEAD_DIM - y.shape[0], pw), y.dtype)
            parts = [parts[0], fill, parts[1], fill]
        return jnp.concatenate(parts, axis=0)

    memo = {}

    def cached(x, tag, build):
        key = (id(x), tag)
        if key not in memo:
            memo[key] = (x, build())
        return memo[key][1]

    def split(x):
        return cached(x, "split", lambda: _split2(x))

    def lhs(x, passes, axis=1):
        hi, low = split(x)
        return cached(x, ("lhs", passes, axis), lambda: jnp.concatenate([hi, low, hi][:passes], axis=axis))

    def rhs(y, passes, axis=0):
        hi, low = split(y)

        def build():
            bh = bd(hi)
            return jnp.concatenate([bh, bh, bd(low)][:passes], axis=axis)
        return cached(y, ("rhs", passes, axis), build)

    def mm(x, y, group):
        return _dot(lhs(x, WKV_PASSES[group]), rhs(y, WKV_PASSES[group]))

    def mm_nt(x, y, group):
        return _dot(lhs(x, WKV_PASSES[group]), rhs(y, WKV_PASSES[group], 1), NT)

    def mm_tn(x, y, group):
        passes = WKV_PASSES[group]
        yh, yl = split(y)
        full = _dot(lhs(x, passes, 0), jnp.concatenate([yh, yh, yl][:passes], axis=0), TN)
        return jnp.where(lo, full[:HEAD_DIM], full[HEAD_DIM:])

    def prepare(keys, prep):
        for s, q in keys:
            tr = slice(q * real, (q + 1) * real)

            def tok(col):
                x = in_ref[s, tr, col * RWKV_WIDTH:(col + 1) * RWKV_WIDTH]
                return x if real == c else jnp.concatenate([x, jnp.zeros((c - real, x.shape[1]), x.dtype)], axis=0)

            lw = tok(1)
            cum = sum(_dot(tri, part) for part in _split3(lw))
            tot = cum[c - 1:c, :]
            en = jnp.exp(-cum)
            ed = jnp.exp(tot - cum)
            kk = tok(4)
            kb = tok(5)
            kx = tok(2)
            prep[s, q] = dict(a=-kk * jnp.exp(cum - lw), r=tok(0) * jnp.exp(cum), b=kb * en, k=kx * en,
                              bh=kb * ed, kh=kx * ed, v=tok(3), etot=jnp.exp(tot))
            yield

    def chunk_algebra(keys, prep, res):
        items = [(s, q, p) for s, q in keys for p in pairs]
        ps = lambda name, it: prep[it[0], it[1]][name][:, it[2] * pw:(it[2] + 1) * pw]
        each = lambda f: [f(i) for i in range(len(items))]
        pv = {name: each(lambda i: ps(name, items[i])) for name in ("a", "r", "b", "k", "bh", "kh", "v", "etot")}
        ar = each(lambda i: jnp.concatenate([pv["a"][i], pv["r"][i]], axis=0))
        g_b = each(lambda i: jnp.where(lower2, mm_nt(ar[i], pv["b"][i], "gram"), 0.0))
        yield
        g_k = each(lambda i: jnp.where(lower2, mm_nt(ar[i], pv["k"][i], "gram"), 0.0))
        a_rb = each(lambda i: g_b[i][c:])
        yield
        p_m = each(lambda i: g_b[i][:c])
        t_m = each(lambda i: eye + p_m[i])
        n = 2
        while n < c:
            p_m = each(lambda i: mm(p_m[i], p_m[i], "inverse"))
            yield
            t_m = each(lambda i: t_m[i] + mm(p_m[i], t_m[i], "inverse"))
            yield
            n *= 2
        gv = each(lambda i: mm(g_k[i], pv["v"][i], "apply"))
        av = each(lambda i: gv[i][:c])
        yield
        w_m = each(lambda i: mm(t_m[i], pv["a"][i], "apply"))
        yield
        u_m = each(lambda i: mm(t_m[i], av[i], "apply"))
        yield
        r_p = each(lambda i: pv["r"][i] + mm(a_rb[i], w_m[i], "apply"))
        yield
        o_i = each(lambda i: mm(a_rb[i], u_m[i], "apply") + gv[i][c:])
        yield
        m_m = each(lambda i: mm_tn(pv["bh"][i], w_m[i], "state") + eye_ch * pv["etot"][i])
        yield
        bk = each(lambda i: jnp.concatenate([pv["bh"][i], pv["kh"][i]], axis=0))
        uv = each(lambda i: jnp.concatenate([u_m[i], pv["v"][i]], axis=0))
        n_m = each(lambda i: mm_tn(bk[i], uv[i], "state"))
        yield
        for i, it in enumerate(items):
            res[it] = (jnp.concatenate([r_p[i], m_m[i]], axis=0), o_i[i], n_m[i])
        yield

    z = [z_ref[sp] for sp in range(nseq * npairs)]

    def carry(keys, res):
        for s, q in keys:
            for p in pairs:
                rm, o_i, n_m = res[s, q, p]
                sp = s * npairs + p
                oz = mm(rm, z[sp], "carry")
                o_ref[s, q * real:(q + 1) * real, p * pw:(p + 1) * pw] = (oz[:c] + o_i)[:real]
                z[sp] = oz[c:] + n_m
            yield

    def interleave(*gens):
        live = list(gens)
        while live:
            live = [g for g in live if next(g, StopIteration) is not StopIteration]

    keys = [(s, q) for q in range(nchunks) for s in range(nseq)]
    half = len(keys) // 2 if nchunks > 1 else len(keys)
    group_a, group_b = keys[:half], keys[half:]
    prep, res = {}, {}
    interleave(prepare(group_a, prep))
    interleave(chunk_algebra(group_a, prep, res), prepare(group_b, prep))
    interleave(chunk_algebra(group_b, prep, res), carry(group_a, res))
    interleave(carry(group_b, res))
    for sp in range(nseq * npairs):
        z_ref[sp] = z[sp]

    @pl.when(j == pl.num_programs(1) - 1)
    def _():
        for s in range(nseq):
            for p in pairs:
                zt = z_ref[s * npairs + p].T
                st_ref[s, 2 * p] = zt[:HEAD_DIM]
                st_ref[s, 2 * p + 1] = zt[HEAD_DIM:]


def _wkv(scan_in, s0, *, nseq, nchunks, rows, real=None):
    real = rows if real is None else real
    assert real == rows or nchunks == 1
    b, t, wide = scan_in.shape
    w = RWKV_WIDTH
    assert wide == N_SCAN_COLS * w
    blk = nchunks * real
    tok = lambda width: pl.BlockSpec((nseq, blk, width), lambda bi, j: (bi, j, 0))
    st = pl.BlockSpec((nseq, RWKV_HEADS, HEAD_DIM, HEAD_DIM), lambda bi, j: (bi, 0, 0, 0))
    return pl.pallas_call(
        functools.partial(_wkv_kernel, nseq=nseq, nchunks=nchunks, rows=rows, real=real),
        grid=(b // nseq, t // blk),
        in_specs=[tok(wide), st],
        out_specs=[tok(w), st],
        out_shape=[jax.ShapeDtypeStruct((b, t, w), F32),
                   jax.ShapeDtypeStruct((b, RWKV_HEADS, HEAD_DIM, HEAD_DIM), F32)],
        scratch_shapes=[pltpu.VMEM((nseq * RWKV_HEADS // 2, HEAD_DIM, LANES), F32)],
        compiler_params=_params("arbitrary", "arbitrary"),
        name="wkv_scan",
    )(scan_in, s0)


def _post_kernel(o_ref, gate_ref, att_ref, x_ref, gng_ref, gnb_ref, ones_ref, wo_ref, gf_ref,
                 wg_ref, wu_ref, wd_ref, gfin_ref, y_ref, *, final):
    tm = x_ref.shape[0]
    sub = min(tm, POST_SUB_ROWS)
    parts = [slice(i * sub, (i + 1) * sub) for i in range(tm // sub)]
    each = lambda f: [f(i) for i in range(len(parts))]
    ones_bd = ones_ref[...]
    o = each(lambda i: o_ref[parts[i], :])
    mean = each(lambda i: _seg_sum(o[i], ones_bd) * (1.0 / HEAD_DIM))
    oc = each(lambda i: o[i] - mean[i])
    var = each(lambda i: _seg_sum(oc[i] * oc[i], ones_bd) * (1.0 / HEAD_DIM))
    rw = each(lambda i: ((oc[i] * lax.rsqrt(var[i] + GN_EPS) * gng_ref[...] + gnb_ref[...]
                          + gate_ref[parts[i], RWKV_WIDTH:]) * gate_ref[parts[i], :RWKV_WIDTH]).astype(BF16))
    x1 = each(lambda i: x_ref[parts[i], :] + _dot(att_ref[parts[i], :].astype(BF16), wo_ref[:ATT_WIDTH, :])
              + _dot(rw[i], wo_ref[ATT_WIDTH:, :]))
    u = each(lambda i: _rms(x1[i], gf_ref[...]).astype(BF16))
    gate = each(lambda i: _dot(u[i], wg_ref[...]))
    up = each(lambda i: _dot(u[i], wu_ref[...]))
    hid = each(lambda i: (gate[i] * _sigmoid(gate[i]) * up[i]).astype(BF16))
    x2 = each(lambda i: x1[i] + _dot(hid[i], wd_ref[...]))
    for i in range(len(parts)):
        y_ref[parts[i], :] = _rms(x2[i], gfin_ref[...]) if final else x2[i]


def _post(o, gate, att, x, p, *, tm, final):
    n = x.shape[0]
    d_ff = p["w_gate"].shape[1]
    row = lambda w: pl.BlockSpec((tm, w), lambda i: (i, 0))
    once = lambda shape: pl.BlockSpec(shape, lambda i: (0,) * len(shape), pipeline_mode=pl.Buffered(1))
    w = RWKV_WIDTH
    return pl.pallas_call(
        functools.partial(_post_kernel, final=final),
        grid=(n // tm,),
        in_specs=[row(w), row(N_GATE_COLS * w), row(ATT_WIDTH), row(D_MODEL),
                  once((1, w)), once((1, w)), once((LANES, LANES)), once((ATT_WIDTH + w, D_MODEL)),
                  once((1, D_MODEL)),
                  once((D_MODEL, d_ff)), once((D_MODEL, d_ff)), once((d_ff, D_MODEL)), once((1, D_MODEL))],
        out_specs=row(D_MODEL),
        out_shape=jax.ShapeDtypeStruct((n, D_MODEL), F32),
        compiler_params=_params("arbitrary"),
        name="post_ffn",
    )(o, gate, att, x, p["gn_g"], p["gn_b"], p["ones_bd"], p["w_out"], p["g_ffn"],
      p["w_gate"], p["w_up"], p["w_down"], p["g_final"])


def _layer_params(l, g_mix, w_in, attn_sinks, rwkv_mu, w0, w2, a0, a2, g2, k_k, k_a, r_k, gn_g, gn_b,
                  w_out, g_ffn, w_gate, w_up, w_down, g_final):
    vec = lambda a: a.reshape(1, -1).astype(F32)
    hd = jnp.arange(LANES) // HEAD_DIM
    return dict(
        g_mix=g_mix[l], w_in=w_in[l].astype(BF16), sinks=attn_sinks[l].astype(F32),
        mu=vec(rwkv_mu[l]), w0=vec(w0[l]), w2=w2[l].astype(BF16), a0=vec(a0[l]), a2=a2[l].astype(BF16),
        g2=g2[l].astype(BF16), k_k=vec(k_k[l]), k_a=vec(k_a[l]), r_k=vec(r_k[l]),
        gn_g=vec(gn_g[l]), gn_b=vec(gn_b[l]), w_out=w_out[l].astype(BF16), g_ffn=vec(g_ffn[l]),
        w_gate=w_gate[l].astype(BF16), w_up=w_up[l].astype(BF16), w_down=w_down[l].astype(BF16),
        g_final=vec(g_final), ones_bd=(hd[:, None] == hd[None, :]).astype(BF16),
    )


def _pick(n, pref):
    t = pref
    while n % t:
        t //= 2
    return t


def _pick_rows(n, cap, unit):
    assert n % unit == 0
    return unit * max(d for d in range(1, cap // unit + 1) if (n // unit) % d == 0)


def _prompt_layer(x, p, final):
    b, t, d = x.shape
    n = b * t
    x2 = x.reshape(n, d)
    tm = _pick(t, INPROJ_ROWS)
    att, k, v, scan_in, gate = _inproj_prep(x2, p, tm=tm, blocks_per_seq=t // tm)
    s0 = jnp.zeros((b, RWKV_HEADS, HEAD_DIM, HEAD_DIM), F32)
    o, s_new = _wkv(scan_in.reshape(b, t, -1), s0, nseq=_pick(b, WKV_PROMPT_SEQS),
                    nchunks=_pick(t // HEAD_DIM, WKV_PROMPT_CHUNKS), rows=HEAD_DIM)
    y = _post(o.reshape(n, -1), gate, att, x2, p, tm=_pick(n, POST_ROWS), final=final)
    wp = min(WINDOW, t)
    k_win = k.reshape(b, t, KV_WIDTH)[:, t - wp:].reshape(b, wp, KV_HEADS, HEAD_DIM)
    v_win = v.reshape(b, t, KV_WIDTH)[:, t - wp:].reshape(b, wp, KV_HEADS, HEAD_DIM)
    shift = _rms_rows(x[:, -1], p["g_mix"])
    return y.reshape(b, t, d), k_win, v_win, s_new, shift


def _sample_layer(x, h_prev, k_buf, v_buf, s0, p, final):
    b, t, d = x.shape
    n = b * t
    x2 = x.reshape(n, d)
    wb = k_buf.shape[1]
    rows_all = jnp.concatenate([x2, h_prev.astype(x2.dtype)], axis=0)
    q, k, v, feat_all = _inproj(rows_all, p["g_mix"], p["w_in"], norm_rows=n, tm=_pick_rows(n + b, 512, 8))
    feat, feat_prev = feat_all[:n], feat_all[n:]
    scan_in, gate = _prep(feat, feat_prev, p, t=t, tm=_pick(n, 512))
    att, k_win, v_win = _attn_sample(q, k, v, k_buf.reshape(b, wb, -1), v_buf.reshape(b, wb, -1),
                                     p["sinks"], t=t, seqs=_pick(b, 8))
    tp = -(-t // BF16_ROWS) * BF16_ROWS
    assert tp <= HEAD_DIM and t % 8 == 0
    o, s_new = _wkv(scan_in.reshape(b, t, -1), s0, nseq=_pick(b, WKV_SAMPLE_SEQS), nchunks=1, rows=tp, real=t)
    y = _post(o.reshape(n, -1), gate, att, x2, p, tm=_pick(n, POST_ROWS), final=final)
    shift = _rms_rows(x[:, -1], p["g_mix"])
    return (y.reshape(b, t, d), k_win.reshape(b, wb, KV_HEADS, HEAD_DIM), v_win.reshape(b, wb, KV_HEADS, HEAD_DIM),
            s_new, shift)


def kernel(x_prompt, x_sample, cache_k, cache_v, state_wkv, state_shift, g_mix, w_in, attn_sinks, rwkv_mu, w0, w2,
           a0, a2, g2, k_k, k_a, r_k, gn_g, gn_b, w_out, g_ffn, w_gate, w_up, w_down, g_final):
    depth = w_in.shape[0]
    xp, xs = x_prompt, x_sample
    outs_p, outs_s = [], []
    for l in range(depth):
        p = _layer_params(l, g_mix, w_in, attn_sinks, rwkv_mu, w0, w2, a0, a2, g2, k_k, k_a, r_k, gn_g, gn_b,
                          w_out, g_ffn, w_gate, w_up, w_down, g_final)
        final = l == depth - 1
        xp, kp, vp, sp, hp = _prompt_layer(xp, p, final)
        xs, kn, vn, sn, hn = _sample_layer(xs, state_shift[l], cache_k[l], cache_v[l], state_wkv[l], p, final)
        outs_p.append((kp, vp, sp, hp))
        outs_s.append((kn, vn, sn, hn))
    stack = lambda outs, i: jnp.stack([o[i] for o in outs])
    return (xp, xs,
            stack(outs_p, 0), stack(outs_p, 1), stack(outs_p, 2), stack(outs_p, 3),
            stack(outs_s, 0), stack(outs_s, 1), stack(outs_s, 2), stack(outs_s, 3))
```

```python
import functools
import math

import jax
import jax.numpy as jnp
from jax import lax
from jax.experimental import pallas as pl
from jax.experimental.pallas import tpu as pltpu

F32 = jnp.float32
BF16 = jnp.bfloat16

D_MODEL = 1024
HEAD_DIM = 64
ATT_HEADS = 8
KV_HEADS = 2
ATT_WIDTH = ATT_HEADS * HEAD_DIM
KV_WIDTH = KV_HEADS * HEAD_DIM
RWKV_HEADS = 8
RWKV_WIDTH = RWKV_HEADS * HEAD_DIM
WINDOW = 128
DECAY_LORA = 64
ICL_LORA = 64
GATE_LORA = 128
RWKV_BLOCK = 3 * RWKV_WIDTH + DECAY_LORA + ICL_LORA + GATE_LORA
ATT_PROJ = ATT_WIDTH + 2 * KV_WIDTH
PROJ_WIDTH = ATT_PROJ + RWKV_BLOCK
RMS_EPS = 1e-6
GN_EPS = 64e-5
L2_EPS = 1e-12
NEG_INF = -1e30
DECAY_SCALE = math.exp(-0.5)

V7X_VMEM_BYTES = 64 * 1024 * 1024
VMEM_LIMIT_BYTES = V7X_VMEM_BYTES * 3 // 4
LANES = 128

WKV_PROMPT_SEQS = 2
WKV_PROMPT_CHUNKS = 8
WKV_SAMPLE_SEQS = 8
WKV_PASSES = dict(gram=1, inverse=1, apply=1, state=3, carry=3)
BF16_ROWS = 16
POST_ROWS = 512
POST_SUB_ROWS = 256
INPROJ_ROWS = 512
INPROJ_SUB_ROWS = 256

NN = (((1,), (0,)), ((), ()))
NT = (((1,), (1,)), ((), ()))
TN = (((0,), (0,)), ((), ()))


def _dot(a, b, dn=NN):
    return lax.dot_general(a, b, dn, preferred_element_type=F32)


def _split2(x):
    hi = x.astype(BF16)
    lo = (x - hi.astype(F32)).astype(BF16)
    return hi, lo


def _split3(x):
    hi = x.astype(BF16)
    r1 = x - hi.astype(F32)
    mid = r1.astype(BF16)
    lo = (r1 - mid.astype(F32)).astype(BF16)
    return hi, mid, lo


def _seg_sum(x, ones_bd):
    hi, lo = _split2(x)
    slabs = [slice(i, i + LANES) for i in range(0, x.shape[1], LANES)]
    return jnp.concatenate([_dot(hi[:, s], ones_bd) + _dot(lo[:, s], ones_bd) for s in slabs], axis=1)


def _rms(x, g):
    return x * lax.rsqrt(jnp.mean(x * x, axis=-1, keepdims=True) + RMS_EPS) * g


def _sigmoid(z):
    return 1.0 / (1.0 + jnp.exp(-z))


def _const_spec(shape):
    return pl.BlockSpec(shape, lambda *_: (0,) * len(shape))


def _params(*sem):
    return pltpu.CompilerParams(dimension_semantics=sem, vmem_limit_bytes=VMEM_LIMIT_BYTES)


def _rms_rows_kernel(x_ref, g_ref, o_ref):
    o_ref[...] = _rms(x_ref[...], g_ref[...])


def _rms_rows(x, g):
    n, d = x.shape
    return pl.pallas_call(
        _rms_rows_kernel,
        out_shape=jax.ShapeDtypeStruct((n, d), F32),
        name="rms_rows",
    )(x, g.reshape(1, d))


def _inproj_kernel(x_ref, g_ref, w_ref, q_ref, k_ref, v_ref, f_ref, *, norm_rows):
    x = x_ref[...]
    row = lax.broadcasted_iota(jnp.int32, (x.shape[0], 1), 0) + pl.program_id(0) * x.shape[0]
    h = jnp.where(row < norm_rows, _rms(x, g_ref[...]), x)
    proj = _dot(h.astype(BF16), w_ref[...])
    q_ref[...] = proj[:, :ATT_WIDTH]
    k_ref[...] = proj[:, ATT_WIDTH:ATT_WIDTH + KV_WIDTH]
    v_ref[...] = proj[:, ATT_WIDTH + KV_WIDTH:ATT_PROJ]
    f_ref[...] = proj[:, ATT_PROJ:]


def _inproj(x, g_mix, w_in_bf, *, norm_rows, tm):
    n = x.shape[0]
    row = lambda w: pl.BlockSpec((tm, w), lambda i: (i, 0))
    return pl.pallas_call(
        functools.partial(_inproj_kernel, norm_rows=norm_rows),
        grid=(n // tm,),
        in_specs=[row(D_MODEL), _const_spec((1, D_MODEL)), _const_spec((D_MODEL, PROJ_WIDTH))],
        out_specs=[row(ATT_WIDTH), row(KV_WIDTH), row(KV_WIDTH), row(RWKV_BLOCK)],
        out_shape=[
            jax.ShapeDtypeStruct((n, ATT_WIDTH), F32),
            jax.ShapeDtypeStruct((n, KV_WIDTH), F32),
            jax.ShapeDtypeStruct((n, KV_WIDTH), F32),
            jax.ShapeDtypeStruct((n, RWKV_BLOCK), F32),
        ],
        compiler_params=_params("arbitrary"),
        name="inproj",
    )(x, g_mix.reshape(1, D_MODEL), w_in_bf)


N_PREP_PARAMS = 10
N_SCAN_COLS = 6
N_GATE_COLS = 2


def _prep_out_views(scan_ref, gate_ref, rows):
    w = RWKV_WIDTH
    return ([scan_ref.at[rows, c * w:(c + 1) * w] for c in range(N_SCAN_COLS)]
            + [gate_ref.at[rows, c * w:(c + 1) * w] for c in range(N_GATE_COLS)])


def _carried_prev(f, carry_ref):
    tm = f.shape[0]
    rolled = pltpu.roll(f, 1, 0)
    row = lax.broadcasted_iota(jnp.int32, f.shape, 0)
    prev = jnp.where(row == 0, carry_ref[0:1, :], rolled)
    carry_ref[0:1, :] = f[tm - 1:tm, :]
    return prev


def _prep_kernel(f_ref, p_ref, *refs, t):
    f = f_ref[...]
    tm = f.shape[0]
    before = jnp.broadcast_to(p_ref[...], (tm // t, t, f.shape[1])).reshape(tm, f.shape[1])
    row = lax.broadcasted_iota(jnp.int32, f.shape, 0)
    prev = jnp.where(row % t == 0, before, pltpu.roll(f, 1, 0))
    scan_ref, gate_ref = refs[N_PREP_PARAMS:N_PREP_PARAMS + 2]
    _prep_math(f, prev, refs[:N_PREP_PARAMS], _prep_out_views(scan_ref, gate_ref, slice(None)))


def _inproj_prep_kernel(sink_ref, x_ref, g_ref, w_ref, *refs, blocks_per_seq):
    att_ref, k_ref, v_ref = refs[N_PREP_PARAMS:N_PREP_PARAMS + 3]
    scan_ref, gate_ref = refs[N_PREP_PARAMS + 3:N_PREP_PARAMS + 5]
    carry_ref, kprev_ref, vprev_ref = refs[N_PREP_PARAMS + 5:]
    first = pl.program_id(0) % blocks_per_seq == 0

    @pl.when(first)
    def _():
        carry_ref[...] = jnp.zeros_like(carry_ref)
        kprev_ref[...] = jnp.zeros_like(kprev_ref)
        vprev_ref[...] = jnp.zeros_like(vprev_ref)

    tm = x_ref.shape[0]
    sub = min(tm, INPROJ_SUB_ROWS)
    parts = [slice(i * sub, (i + 1) * sub) for i in range(tm // sub)]
    proj = [_dot(_rms(x_ref[rows, :], g_ref[...]).astype(BF16), w_ref[...]) for rows in parts]
    for rows, pj in zip(parts, proj):
        k_ref[rows, :] = pj[:, ATT_WIDTH:ATT_WIDTH + KV_WIDTH]
        v_ref[rows, :] = pj[:, ATT_WIDTH + KV_WIDTH:ATT_PROJ]
    for rows, pj in zip(parts, proj):
        f = pj[:, ATT_PROJ:]
        _prep_math(f, _carried_prev(f, carry_ref), refs[:N_PREP_PARAMS], _prep_out_views(scan_ref, gate_ref, rows))

    def blk(i, c0, c1):
        r0 = i * WINDOW
        return proj[r0 // sub][r0 % sub:r0 % sub + WINDOW, c0:c1]

    nblk = tm // WINDOW
    kcol, vcol = (ATT_WIDTH, ATT_WIDTH + KV_WIDTH), (ATT_WIDTH + KV_WIDTH, ATT_PROJ)
    ops = [_kv_operands(kprev_ref[...], vprev_ref[...])] + [_kv_operands(blk(i, *kcol), blk(i, *vcol))
                                                           for i in range(nblk)]
    band, kj = _band_mask(WINDOW, WINDOW)
    for i in range(nblk):
        mask = band & (jnp.logical_not(first) | (kj >= WINDOW)) if i == 0 else band
        kv_ops = [tuple(jnp.concatenate([ops[i][kv][n], ops[i + 1][kv][n]], axis=0) for n in range(3))
                  for kv in range(KV_HEADS)]
        _attend(blk(i, 0, ATT_WIDTH).astype(BF16), kv_ops, mask, sink_ref,
                att_ref.at[i * WINDOW:(i + 1) * WINDOW, :])
    kprev_ref[...] = blk(nblk - 1, *kcol)
    vprev_ref[...] = blk(nblk - 1, *vcol)


def _prep_math(f, prev, params, outs):
    (mu_ref, w0_ref, w2_ref, a0_ref, a2_ref, g2_ref, kk_ref, ka_ref, rk_ref, ones_ref) = params
    (r_out, l_out, k_out, v_out, kk_out, kb_out, g_out, bo_out) = outs
    xs = f + (prev - f) * mu_ref[...]
    w3 = 3 * RWKV_WIDTH
    r = xs[:, :RWKV_WIDTH]
    k = xs[:, RWKV_WIDTH:2 * RWKV_WIDTH]
    v = xs[:, 2 * RWKV_WIDTH:w3]
    wl = xs[:, w3:w3 + DECAY_LORA]
    al = xs[:, w3 + DECAY_LORA:w3 + DECAY_LORA + ICL_LORA]
    gl = xs[:, w3 + DECAY_LORA + ICL_LORA:]
    log_decay = -DECAY_SCALE * _sigmoid(w0_ref[...] + _dot(jnp.tanh(wl).astype(BF16), w2_ref[...]))
    a = _sigmoid(a0_ref[...] + _dot(al.astype(BF16), a2_ref[...]))
    g = _dot(_sigmoid(gl).astype(BF16), g2_ref[...])
    ones_bd = ones_ref[...]
    kk = k * kk_ref[...]
    kk = kk * lax.rsqrt(_seg_sum(kk * kk, ones_bd) + L2_EPS)
    k2 = k * (1.0 + (a - 1.0) * ka_ref[...])
    r_out[...] = r
    l_out[...] = log_decay
    k_out[...] = k2
    v_out[...] = v
    kk_out[...] = kk
    kb_out[...] = kk * a
    g_out[...] = g
    bo_out[...] = _seg_sum(r * k2 * rk_ref[...], ones_bd) * v


def _prep_params(p):
    w = RWKV_WIDTH
    ins = [p["mu"], p["w0"], p["w2"], p["a0"], p["a2"], p["g2"], p["k_k"], p["k_a"], p["r_k"], p["ones_bd"]]
    specs = [_const_spec((1, RWKV_BLOCK)), _const_spec((1, w)), _const_spec((DECAY_LORA, w)),
             _const_spec((1, w)), _const_spec((ICL_LORA, w)), _const_spec((GATE_LORA, w)),
             _const_spec((1, w)), _const_spec((1, w)), _const_spec((1, w)),
             _const_spec((LANES, LANES))]
    assert len(ins) == N_PREP_PARAMS
    return ins, specs


def _inproj_prep(x, p, *, tm, blocks_per_seq):
    n = x.shape[0]
    assert tm % WINDOW == 0
    row = lambda w: pl.BlockSpec((tm, w), lambda i: (i, 0))
    w = RWKV_WIDTH
    prm, prm_specs = _prep_params(p)
    return pl.pallas_call(
        functools.partial(_inproj_prep_kernel, blocks_per_seq=blocks_per_seq),
        grid=(n // tm,),
        in_specs=[pl.BlockSpec(memory_space=pltpu.SMEM), row(D_MODEL), _const_spec((1, D_MODEL)),
                  _const_spec((D_MODEL, PROJ_WIDTH))] + prm_specs,
        out_specs=[row(ATT_WIDTH), row(KV_WIDTH), row(KV_WIDTH), row(N_SCAN_COLS * w), row(N_GATE_COLS * w)],
        out_shape=[jax.ShapeDtypeStruct((n, ATT_WIDTH), BF16), jax.ShapeDtypeStruct((n, KV_WIDTH), F32),
                   jax.ShapeDtypeStruct((n, KV_WIDTH), F32), jax.ShapeDtypeStruct((n, N_SCAN_COLS * w), F32),
                   jax.ShapeDtypeStruct((n, N_GATE_COLS * w), F32)],
        scratch_shapes=[pltpu.VMEM((8, RWKV_BLOCK), F32), pltpu.VMEM((WINDOW, KV_WIDTH), F32),
                        pltpu.VMEM((WINDOW, KV_WIDTH), F32)],
        compiler_params=_params("arbitrary"),
        name="inproj_prep_attn",
    )(p["sinks"], x, p["g_mix"].reshape(1, D_MODEL), p["w_in"], *prm)


def _prep(feat, feat_before, p, *, t, tm):
    n = feat.shape[0]
    assert tm % t == 0
    row = lambda w: pl.BlockSpec((tm, w), lambda i: (i, 0))
    w = RWKV_WIDTH
    prm, prm_specs = _prep_params(p)
    return pl.pallas_call(
        functools.partial(_prep_kernel, t=t),
        grid=(n // tm,),
        in_specs=[row(RWKV_BLOCK), pl.BlockSpec((tm // t, 1, RWKV_BLOCK), lambda i: (i, 0, 0))] + prm_specs,
        out_specs=[row(N_SCAN_COLS * w), row(N_GATE_COLS * w)],
        out_shape=[jax.ShapeDtypeStruct((n, N_SCAN_COLS * w), F32), jax.ShapeDtypeStruct((n, N_GATE_COLS * w), F32)],
        compiler_params=_params("arbitrary"),
        name="rwkv_prep",
    )(feat, feat_before.reshape(n // t, 1, RWKV_BLOCK), *prm)


def _kv_operands(kblk, vblk):
    lo_m = lax.broadcasted_iota(jnp.int32, (1, LANES), 1) < HEAD_DIM
    k_sw = pltpu.roll(kblk, HEAD_DIM, 1)
    v_sw = pltpu.roll(vblk, HEAD_DIM, 1)
    ops = []
    for kv in range(KV_HEADS):
        own = lo_m if kv == 0 else ~lo_m
        kdup = jnp.where(own, kblk, k_sw).astype(BF16)
        v_own = jnp.where(own, vblk, 0.0).astype(BF16)
        v_oth = jnp.where(own, 0.0, v_sw).astype(BF16)
        ops.append((kdup,) + ((v_own, v_oth) if kv == 0 else (v_oth, v_own)))
    return ops


def _attend(q, kv_ops, mask, sink_ref, o_ref):
    nq = q.shape[0]
    lo_m = lax.broadcasted_iota(jnp.int32, (1, LANES), 1) < HEAD_DIM
    mask2 = jnp.concatenate([mask, mask], axis=0)
    rowi = lax.broadcasted_iota(jnp.int32, (2 * nq, 1), 0)
    for kv in range(KV_HEADS):
        kdup, v_lo, v_hi = kv_ops[kv]
        for pr in range(2):
            c0 = (kv * 2 + pr) * LANES
            qp = q[:, c0:c0 + LANES].astype(F32) * (HEAD_DIM ** -0.5)
            qs = jnp.concatenate([jnp.where(lo_m, qp, 0.0), jnp.where(lo_m, 0.0, qp)], axis=0)
            s = jnp.where(mask2, _dot(qs.astype(BF16), kdup, NT), NEG_INF)
            h0 = (kv * 2 + pr) * 2
            sink = jnp.where(rowi < nq, sink_ref[h0], sink_ref[h0 + 1])
            m = jnp.maximum(jnp.max(s, axis=-1, keepdims=True), sink)
            e = jnp.exp(s - m)
            rden = 1.0 / (jnp.sum(e, axis=-1, keepdims=True) + jnp.exp(sink - m))
            out = _dot(e[:nq].astype(BF16), v_lo) + _dot(e[nq:].astype(BF16), v_hi)
            out = out * jnp.where(lo_m, rden[:nq], rden[nq:])
            o_ref[:, c0:c0 + LANES] = out.astype(o_ref.dtype)


def _band_mask(nq, q0):
    qi = lax.broadcasted_iota(jnp.int32, (nq, 2 * WINDOW), 0) + q0
    kj = lax.broadcasted_iota(jnp.int32, (nq, 2 * WINDOW), 1)
    return (kj <= qi) & (qi - kj < WINDOW), kj


def _attn_sample_kernel(sink_ref, q_ref, kn_ref, vn_ref, ck_ref, cv_ref, o_ref, kw_ref, vw_ref,
                        kall_ref, vall_ref, *, seqs, t):
    wb = ck_ref.shape[1]

    @pl.when(pl.program_id(0) == 0)
    def _():
        kall_ref[...] = jnp.zeros_like(kall_ref)
        vall_ref[...] = jnp.zeros_like(vall_ref)

    mask, _ = _band_mask(t, wb)
    for s in range(seqs):
        rows = slice(s * t, (s + 1) * t)
        kall_ref[0:wb, :] = ck_ref[s]
        vall_ref[0:wb, :] = cv_ref[s]
        kall_ref[wb:wb + t, :] = kn_ref[rows, :]
        vall_ref[wb:wb + t, :] = vn_ref[rows, :]
        _attend(q_ref[rows, :], _kv_operands(kall_ref[...], vall_ref[...]), mask, sink_ref, o_ref.at[rows, :])
        kw_ref[s] = kall_ref[t:t + wb, :]
        vw_ref[s] = vall_ref[t:t + wb, :]


def _attn_sample(q, k, v, cache_k, cache_v, sinks, *, t, seqs):
    b, wb, _ = cache_k.shape
    n = b * t
    row = lambda w: pl.BlockSpec((seqs * t, w), lambda i: (i, 0))
    win = pl.BlockSpec((seqs, wb, KV_WIDTH), lambda i: (i, 0, 0))
    return pl.pallas_call(
        functools.partial(_attn_sample_kernel, seqs=seqs, t=t),
        grid=(b // seqs,),
        in_specs=[pl.BlockSpec(memory_space=pltpu.SMEM), row(ATT_WIDTH), row(KV_WIDTH), row(KV_WIDTH), win, win],
        out_specs=[row(ATT_WIDTH), win, win],
        out_shape=[jax.ShapeDtypeStruct((n, ATT_WIDTH), F32),
                   jax.ShapeDtypeStruct((b, wb, KV_WIDTH), F32),
                   jax.ShapeDtypeStruct((b, wb, KV_WIDTH), F32)],
        scratch_shapes=[pltpu.VMEM((2 * WINDOW, KV_WIDTH), F32), pltpu.VMEM((2 * WINDOW, KV_WIDTH), F32)],
        compiler_params=_params("arbitrary"),
        name="attn_sample",
    )(sinks, q, k, v, cache_k, cache_v)


def _wkv_kernel(in_ref, s0_ref, o_ref, st_ref, z_ref, *, nseq, nchunks, rows, real):
    c = rows
    pw = LANES
    j = pl.program_id(1)
    npairs = RWKV_HEADS // 2
    pairs = range(npairs)

    @pl.when(j == 0)
    def _():
        for s in range(nseq):
            for p in pairs:
                z_ref[s * npairs + p] = jnp.concatenate([s0_ref[s, 2 * p], s0_ref[s, 2 * p + 1]], axis=0).T

    def pair_iota(nrow):
        return (lax.broadcasted_iota(jnp.int32, (nrow, pw), 0),
                lax.broadcasted_iota(jnp.int32, (nrow, pw), 1) & (HEAD_DIM - 1))

    row, col = pair_iota(c)
    lo = lax.broadcasted_iota(jnp.int32, (1, pw), 1) < HEAD_DIM
    lower2 = jnp.concatenate([row > col, row >= col], axis=0)
    eye = (row == col).astype(F32)
    krow, kcol = pair_iota(HEAD_DIM)
    eye_ch = (krow == kcol).astype(F32)
    ri = lax.broadcasted_iota(jnp.int32, (c, c), 0)
    tri = (ri >= lax.broadcasted_iota(jnp.int32, (c, c), 1)).astype(BF16)

    def bd(y):
        zero = jnp.zeros_like(y)
        parts = [jnp.where(lo, y, zero), jnp.where(lo, zero, y)]
        if y.shape[0] < HEAD_DIM:
            fill = jnp.zeros((HEAD_DIM - y.shape[0], pw), y.dtype)
            parts = [parts[0], fill, parts[1], fill]
        return jnp.concatenate(parts, axis=0)

    memo = {}

    def cached(x, tag, build):
        key = (id(x), tag)
        if key not in memo:
            memo[key] = (x, build())
        return memo[key][1]

    def split(x):
        return cached(x, "split", lambda: _split2(x))

    def lhs(x, passes, axis=1):
        hi, low = split(x)
        return cached(x, ("lhs", passes, axis), lambda: jnp.concatenate([hi, low, hi][:passes], axis=axis))

    def rhs(y, passes, axis=0):
        hi, low = split(y)

        def build():
            bh = bd(hi)
            return jnp.concatenate([bh, bh, bd(low)][:passes], axis=axis)
        return cached(y, ("rhs", passes, axis), build)

    def mm(x, y, group):
        return _dot(lhs(x, WKV_PASSES[group]), rhs(y, WKV_PASSES[group]))

    def mm_nt(x, y, group):
        return _dot(lhs(x, WKV_PASSES[group]), rhs(y, WKV_PASSES[group], 1), NT)

    def mm_tn(x, y, group):
        passes = WKV_PASSES[group]
        yh, yl = split(y)
        full = _dot(lhs(x, passes, 0), jnp.concatenate([yh, yh, yl][:passes], axis=0), TN)
        return jnp.where(lo, full[:HEAD_DIM], full[HEAD_DIM:])

    def prepare(keys, prep):
        for s, q in keys:
            tr = slice(q * real, (q + 1) * real)

            def tok(col):
                x = in_ref[s, tr, col * RWKV_WIDTH:(col + 1) * RWKV_WIDTH]
                return x if real == c else jnp.concatenate([x, jnp.zeros((c - real, x.shape[1]), x.dtype)], axis=0)

            lw = tok(1)
            cum = sum(_dot(tri, part) for part in _split3(lw))
            tot = cum[c - 1:c, :]
            en = jnp.exp(-cum)
            ed = jnp.exp(tot - cum)
            kk = tok(4)
            kb = tok(5)
            kx = tok(2)
            prep[s, q] = dict(a=-kk * jnp.exp(cum - lw), r=tok(0) * jnp.exp(cum), b=kb * en, k=kx * en,
                              bh=kb * ed, kh=kx * ed, v=tok(3), etot=jnp.exp(tot))
            yield

    def chunk_algebra(keys, prep, res):
        items = [(s, q, p) for s, q in keys for p in pairs]
        ps = lambda name, it: prep[it[0], it[1]][name][:, it[2] * pw:(it[2] + 1) * pw]
        each = lambda f: [f(i) for i in range(len(items))]
        pv = {name: each(lambda i: ps(name, items[i])) for name in ("a", "r", "b", "k", "bh", "kh", "v", "etot")}
        ar = each(lambda i: jnp.concatenate([pv["a"][i], pv["r"][i]], axis=0))
        g_b = each(lambda i: jnp.where(lower2, mm_nt(ar[i], pv["b"][i], "gram"), 0.0))
        yield
        g_k = each(lambda i: jnp.where(lower2, mm_nt(ar[i], pv["k"][i], "gram"), 0.0))
        a_rb = each(lambda i: g_b[i][c:])
        yield
        p_m = each(lambda i: g_b[i][:c])
        t_m = each(lambda i: eye + p_m[i])
        n = 2
        while n < c:
            p_m = each(lambda i: mm(p_m[i], p_m[i], "inverse"))
            yield
            t_m = each(lambda i: t_m[i] + mm(p_m[i], t_m[i], "inverse"))
            yield
            n *= 2
        gv = each(lambda i: mm(g_k[i], pv["v"][i], "apply"))
        av = each(lambda i: gv[i][:c])
        yield
        w_m = each(lambda i: mm(t_m[i], pv["a"][i], "apply"))
        yield
        u_m = each(lambda i: mm(t_m[i], av[i], "apply"))
        yield
        r_p = each(lambda i: pv["r"][i] + mm(a_rb[i], w_m[i], "apply"))
        yield
        o_i = each(lambda i: mm(a_rb[i], u_m[i], "apply") + gv[i][c:])
        yield
        m_m = each(lambda i: mm_tn(pv["bh"][i], w_m[i], "state") + eye_ch * pv["etot"][i])
        yield
        bk = each(lambda i: jnp.concatenate([pv["bh"][i], pv["kh"][i]], axis=0))
        uv = each(lambda i: jnp.concatenate([u_m[i], pv["v"][i]], axis=0))
        n_m = each(lambda i: mm_tn(bk[i], uv[i], "state"))
        yield
        for i, it in enumerate(items):
            res[it] = (jnp.concatenate([r_p[i], m_m[i]], axis=0), o_i[i], n_m[i])
        yield

    z = [z_ref[sp] for sp in range(nseq * npairs)]

    def carry(keys, res):
        for s, q in keys:
            for p in pairs:
                rm, o_i, n_m = res[s, q, p]
                sp = s * npairs + p
                oz = mm(rm, z[sp], "carry")
                o_ref[s, q * real:(q + 1) * real, p * pw:(p + 1) * pw] = (oz[:c] + o_i)[:real]
                z[sp] = oz[c:] + n_m
            yield

    def interleave(*gens):
        live = list(gens)
        while live:
            live = [g for g in live if next(g, StopIteration) is not StopIteration]

    keys = [(s, q) for q in range(nchunks) for s in range(nseq)]
    half = len(keys) // 2 if nchunks > 1 else len(keys)
    group_a, group_b = keys[:half], keys[half:]
    prep, res = {}, {}
    interleave(prepare(group_a, prep))
    interleave(chunk_algebra(group_a, prep, res), prepare(group_b, prep))
    interleave(chunk_algebra(group_b, prep, res), carry(group_a, res))
    interleave(carry(group_b, res))
    for sp in range(nseq * npairs):
        z_ref[sp] = z[sp]

    @pl.when(j == pl.num_programs(1) - 1)
    def _():
        for s in range(nseq):
            for p in pairs:
                zt = z_ref[s * npairs + p].T
                st_ref[s, 2 * p] = zt[:HEAD_DIM]
                st_ref[s, 2 * p + 1] = zt[HEAD_DIM:]


def _wkv(scan_in, s0, *, nseq, nchunks, rows, real=None):
    real = rows if real is None else real
    assert real == rows or nchunks == 1
    b, t, wide = scan_in.shape
    w = RWKV_WIDTH
    assert wide == N_SCAN_COLS * w
    blk = nchunks * real
    tok = lambda width: pl.BlockSpec((nseq, blk, width), lambda bi, j: (bi, j, 0))
    st = pl.BlockSpec((nseq, RWKV_HEADS, HEAD_DIM, HEAD_DIM), lambda bi, j: (bi, 0, 0, 0))
    return pl.pallas_call(
        functools.partial(_wkv_kernel, nseq=nseq, nchunks=nchunks, rows=rows, real=real),
        grid=(b // nseq, t // blk),
        in_specs=[tok(wide), st],
        out_specs=[tok(w), st],
        out_shape=[jax.ShapeDtypeStruct((b, t, w), F32),
                   jax.ShapeDtypeStruct((b, RWKV_HEADS, HEAD_DIM, HEAD_DIM), F32)],
        scratch_shapes=[pltpu.VMEM((nseq * RWKV_HEADS // 2, HEAD_DIM, LANES), F32)],
        compiler_params=_params("arbitrary", "arbitrary"),
        name="wkv_scan",
    )(scan_in, s0)


def _post_kernel(o_ref, gate_ref, att_ref, x_ref, gng_ref, gnb_ref, ones_ref, wo_ref, gf_ref,
                 wg_ref, wu_ref, wd_ref, gfin_ref, y_ref, *, final):
    tm = x_ref.shape[0]
    sub = min(tm, POST_SUB_ROWS)
    parts = [slice(i * sub, (i + 1) * sub) for i in range(tm // sub)]
    each = lambda f: [f(i) for i in range(len(parts))]
    ones_bd = ones_ref[...]
    o = each(lambda i: o_ref[parts[i], :])
    mean = each(lambda i: _seg_sum(o[i], ones_bd) * (1.0 / HEAD_DIM))
    oc = each(lambda i: o[i] - mean[i])
    var = each(lambda i: _seg_sum(oc[i] * oc[i], ones_bd) * (1.0 / HEAD_DIM))
    rw = each(lambda i: ((oc[i] * lax.rsqrt(var[i] + GN_EPS) * gng_ref[...] + gnb_ref[...]
                          + gate_ref[parts[i], RWKV_WIDTH:]) * gate_ref[parts[i], :RWKV_WIDTH]).astype(BF16))
    x1 = each(lambda i: x_ref[parts[i], :] + _dot(att_ref[parts[i], :].astype(BF16), wo_ref[:ATT_WIDTH, :])
              + _dot(rw[i], wo_ref[ATT_WIDTH:, :]))
    u = each(lambda i: _rms(x1[i], gf_ref[...]).astype(BF16))
    gate = each(lambda i: _dot(u[i], wg_ref[...]))
    up = each(lambda i: _dot(u[i], wu_ref[...]))
    hid = each(lambda i: (gate[i] * _sigmoid(gate[i]) * up[i]).astype(BF16))
    x2 = each(lambda i: x1[i] + _dot(hid[i], wd_ref[...]))
    for i in range(len(parts)):
        y_ref[parts[i], :] = _rms(x2[i], gfin_ref[...]) if final else x2[i]


def _post(o, gate, att, x, p, *, tm, final):
    n = x.shape[0]
    d_ff = p["w_gate"].shape[1]
    row = lambda w: pl.BlockSpec((tm, w), lambda i: (i, 0))
    once = lambda shape: pl.BlockSpec(shape, lambda i: (0,) * len(shape), pipeline_mode=pl.Buffered(1))
    w = RWKV_WIDTH
    return pl.pallas_call(
        functools.partial(_post_kernel, final=final),
        grid=(n // tm,),
        in_specs=[row(w), row(N_GATE_COLS * w), row(ATT_WIDTH), row(D_MODEL),
                  once((1, w)), once((1, w)), once((LANES, LANES)), once((ATT_WIDTH + w, D_MODEL)),
                  once((1, D_MODEL)),
                  once((D_MODEL, d_ff)), once((D_MODEL, d_ff)), once((d_ff, D_MODEL)), once((1, D_MODEL))],
        out_specs=row(D_MODEL),
        out_shape=jax.ShapeDtypeStruct((n, D_MODEL), F32),
        compiler_params=_params("arbitrary"),
        name="post_ffn",
    )(o, gate, att, x, p["gn_g"], p["gn_b"], p["ones_bd"], p["w_out"], p["g_ffn"],
      p["w_gate"], p["w_up"], p["w_down"], p["g_final"])


def _layer_params(l, g_mix, w_in, attn_sinks, rwkv_mu, w0, w2, a0, a2, g2, k_k, k_a, r_k, gn_g, gn_b,
                  w_out, g_ffn, w_gate, w_up, w_down, g_final):
    vec = lambda a: a.reshape(1, -1).astype(F32)
    hd = jnp.arange(LANES) // HEAD_DIM
    return dict(
        g_mix=g_mix[l], w_in=w_in[l].astype(BF16), sinks=attn_sinks[l].astype(F32),
        mu=vec(rwkv_mu[l]), w0=vec(w0[l]), w2=w2[l].astype(BF16), a0=vec(a0[l]), a2=a2[l].astype(BF16),
        g2=g2[l].astype(BF16), k_k=vec(k_k[l]), k_a=vec(k_a[l]), r_k=vec(r_k[l]),
        gn_g=vec(gn_g[l]), gn_b=vec(gn_b[l]), w_out=w_out[l].astype(BF16), g_ffn=vec(g_ffn[l]),
        w_gate=w_gate[l].astype(BF16), w_up=w_up[l].astype(BF16), w_down=w_down[l].astype(BF16),
        g_final=vec(g_final), ones_bd=(hd[:, None] == hd[None, :]).astype(BF16),
    )


def _pick(n, pref):
    t = pref
    while n % t:
        t //= 2
    return t


def _pick_rows(n, cap, unit):
    assert n % unit == 0
    return unit * max(d for d in range(1, cap // unit + 1) if (n // unit) % d == 0)


def _prompt_layer(x, p, final):
    b, t, d = x.shape
    n = b * t
    x2 = x.reshape(n, d)
    tm = _pick(t, INPROJ_ROWS)
    att, k, v, scan_in, gate = _inproj_prep(x2, p, tm=tm, blocks_per_seq=t // tm)
    s0 = jnp.zeros((b, RWKV_HEADS, HEAD_DIM, HEAD_DIM), F32)
    o, s_new = _wkv(scan_in.reshape(b, t, -1), s0, nseq=_pick(b, WKV_PROMPT_SEQS),
                    nchunks=_pick(t // HEAD_DIM, WKV_PROMPT_CHUNKS), rows=HEAD_DIM)
    y = _post(o.reshape(n, -1), gate, att, x2, p, tm=_pick(n, POST_ROWS), final=final)
    wp = min(WINDOW, t)
    k_win = k.reshape(b, t, KV_WIDTH)[:, t - wp:].reshape(b, wp, KV_HEADS, HEAD_DIM)
    v_win = v.reshape(b, t, KV_WIDTH)[:, t - wp:].reshape(b, wp, KV_HEADS, HEAD_DIM)
    shift = _rms_rows(x[:, -1], p["g_mix"])
    return y.reshape(b, t, d), k_win, v_win, s_new, shift


def _sample_layer(x, h_prev, k_buf, v_buf, s0, p, final):
    b, t, d = x.shape
    n = b * t
    x2 = x.reshape(n, d)
    wb = k_buf.shape[1]
    rows_all = jnp.concatenate([x2, h_prev.astype(x2.dtype)], axis=0)
    q, k, v, feat_all = _inproj(rows_all, p["g_mix"], p["w_in"], norm_rows=n, tm=_pick_rows(n + b, 512, 8))
    feat, feat_prev = feat_all[:n], feat_all[n:]
    scan_in, gate = _prep(feat, feat_prev, p, t=t, tm=_pick(n, 512))
    att, k_win, v_win = _attn_sample(q, k, v, k_buf.reshape(b, wb, -1), v_buf.reshape(b, wb, -1),
                                     p["sinks"], t=t, seqs=_pick(b, 8))
    tp = -(-t // BF16_ROWS) * BF16_ROWS
    assert tp <= HEAD_DIM and t % 8 == 0
    o, s_new = _wkv(scan_in.reshape(b, t, -1), s0, nseq=_pick(b, WKV_SAMPLE_SEQS), nchunks=1, rows=tp, real=t)
    y = _post(o.reshape(n, -1), gate, att, x2, p, tm=_pick(n, POST_ROWS), final=final)
    shift = _rms_rows(x[:, -1], p["g_mix"])
    return (y.reshape(b, t, d), k_win.reshape(b, wb, KV_HEADS, HEAD_DIM), v_win.reshape(b, wb, KV_HEADS, HEAD_DIM),
            s_new, shift)


def kernel(x_prompt, x_sample, cache_k, cache_v, state_wkv, state_shift, g_mix, w_in, attn_sinks, rwkv_mu, w0, w2,
           a0, a2, g2, k_k, k_a, r_k, gn_g, gn_b, w_out, g_ffn, w_gate, w_up, w_down, g_final):
    depth = w_in.shape[0]
    xp, xs = x_prompt, x_sample
    outs_p, outs_s = [], []
    for l in range(depth):
        p = _layer_params(l, g_mix, w_in, attn_sinks, rwkv_mu, w0, w2, a0, a2, g2, k_k, k_a, r_k, gn_g, gn_b,
                          w_out, g_ffn, w_gate, w_up, w_down, g_final)
        final = l == depth - 1
        xp, kp, vp, sp, hp = _prompt_layer(xp, p, final)
        xs, kn, vn, sn, hn = _sample_layer(xs, state_shift[l], cache_k[l], cache_v[l], state_wkv[l], p, final)
        outs_p.append((kp, vp, sp, hp))
        outs_s.append((kn, vn, sn, hn))
    stack = lambda outs, i: jnp.stack([o[i] for o in outs])
    return (xp, xs,
            stack(outs_p, 0), stack(outs_p, 1), stack(outs_p, 2), stack(outs_p, 3),
            stack(outs_s, 0), stack(outs_s, 1), stack(outs_s, 2), stack(outs_s, 3))
```

```python
import functools
import math

import jax
import jax.numpy as jnp
from jax import lax
from jax.experimental import pallas as pl
from jax.experimental.pallas import tpu as pltpu

F32 = jnp.float32
BF16 = jnp.bfloat16

D_MODEL = 1024
HEAD_DIM = 64
ATT_HEADS = 8
KV_HEADS = 2
ATT_WIDTH = ATT_HEADS * HEAD_DIM
KV_WIDTH = KV_HEADS * HEAD_DIM
RWKV_HEADS = 8
RWKV_WIDTH = RWKV_HEADS * HEAD_DIM
WINDOW = 128
DECAY_LORA = 64
ICL_LORA = 64
GATE_LORA = 128
RWKV_BLOCK = 3 * RWKV_WIDTH + DECAY_LORA + ICL_LORA + GATE_LORA
ATT_PROJ = ATT_WIDTH + 2 * KV_WIDTH
PROJ_WIDTH = ATT_PROJ + RWKV_BLOCK
RMS_EPS = 1e-6
GN_EPS = 64e-5
L2_EPS = 1e-12
NEG_INF = -1e30
DECAY_SCALE = math.exp(-0.5)

V7X_VMEM_BYTES = 64 * 1024 * 1024
VMEM_LIMIT_BYTES = V7X_VMEM_BYTES * 3 // 4
LANES = 128

WKV_PROMPT_SEQS = 2
WKV_PROMPT_CHUNKS = 4
WKV_SAMPLE_SEQS = 8
ATTN_SAMPLE_SEQS = 16
WKV_PASSES = dict(gram=1, inverse=1, apply=1, state=3, carry=3)
BF16_ROWS = 16
POST_ROWS = 512
POST_SUB_ROWS = 256
INPROJ_ROWS = 512
INPROJ_SUB_ROWS = 256

NN = (((1,), (0,)), ((), ()))
NT = (((1,), (1,)), ((), ()))
TN = (((0,), (0,)), ((), ()))


def _dot(a, b, dn=NN):
    return lax.dot_general(a, b, dn, preferred_element_type=F32)


def _split2(x):
    hi = x.astype(BF16)
    lo = (x - hi.astype(F32)).astype(BF16)
    return hi, lo


def _split3(x):
    hi = x.astype(BF16)
    r1 = x - hi.astype(F32)
    mid = r1.astype(BF16)
    lo = (r1 - mid.astype(F32)).astype(BF16)
    return hi, mid, lo


def _seg_sum(x, ones_bd):
    hi, lo = _split2(x)
    slabs = [slice(i, i + LANES) for i in range(0, x.shape[1], LANES)]
    return jnp.concatenate([_dot(hi[:, s], ones_bd) + _dot(lo[:, s], ones_bd) for s in slabs], axis=1)


def _rms(x, g):
    return x * lax.rsqrt(jnp.mean(x * x, axis=-1, keepdims=True) + RMS_EPS) * g


def _sigmoid(z):
    return 1.0 / (1.0 + jnp.exp(-z))


def _const_spec(shape):
    return pl.BlockSpec(shape, lambda *_: (0,) * len(shape))


def _params(*sem):
    return pltpu.CompilerParams(dimension_semantics=sem, vmem_limit_bytes=VMEM_LIMIT_BYTES)


def _rms_rows_kernel(x_ref, g_ref, o_ref):
    o_ref[...] = _rms(x_ref[...], g_ref[...])


def _rms_rows(x, g):
    n, d = x.shape
    return pl.pallas_call(
        _rms_rows_kernel,
        out_shape=jax.ShapeDtypeStruct((n, d), F32),
        name="rms_rows",
    )(x, g.reshape(1, d))


def _inproj_kernel(x_ref, g_ref, w_ref, q_ref, k_ref, v_ref, f_ref, *, norm_rows):
    x = x_ref[...]
    row = lax.broadcasted_iota(jnp.int32, (x.shape[0], 1), 0) + pl.program_id(0) * x.shape[0]
    h = jnp.where(row < norm_rows, _rms(x, g_ref[...]), x)
    proj = _dot(h.astype(BF16), w_ref[...])
    q_ref[...] = proj[:, :ATT_WIDTH]
    k_ref[...] = proj[:, ATT_WIDTH:ATT_WIDTH + KV_WIDTH]
    v_ref[...] = proj[:, ATT_WIDTH + KV_WIDTH:ATT_PROJ]
    f_ref[...] = proj[:, ATT_PROJ:]


def _inproj(x, g_mix, w_in_bf, *, norm_rows, tm):
    n = x.shape[0]
    row = lambda w: pl.BlockSpec((tm, w), lambda i: (i, 0))
    return pl.pallas_call(
        functools.partial(_inproj_kernel, norm_rows=norm_rows),
        grid=(n // tm,),
        in_specs=[row(D_MODEL), _const_spec((1, D_MODEL)), _const_spec((D_MODEL, PROJ_WIDTH))],
        out_specs=[row(ATT_WIDTH), row(KV_WIDTH), row(KV_WIDTH), row(RWKV_BLOCK)],
        out_shape=[
            jax.ShapeDtypeStruct((n, ATT_WIDTH), F32),
            jax.ShapeDtypeStruct((n, KV_WIDTH), F32),
            jax.ShapeDtypeStruct((n, KV_WIDTH), F32),
            jax.ShapeDtypeStruct((n, RWKV_BLOCK), F32),
        ],
        compiler_params=_params("arbitrary"),
        name="inproj",
    )(x, g_mix.reshape(1, D_MODEL), w_in_bf)


N_PREP_PARAMS = 10
N_SCAN_COLS = 6
N_GATE_COLS = 2


def _prep_out_views(scan_ref, gate_ref, rows):
    w = RWKV_WIDTH
    return ([scan_ref.at[rows, c * w:(c + 1) * w] for c in range(N_SCAN_COLS)]
            + [gate_ref.at[rows, c * w:(c + 1) * w] for c in range(N_GATE_COLS)])


def _carried_prev(f, carry_ref):
    tm = f.shape[0]
    rolled = pltpu.roll(f, 1, 0)
    row = lax.broadcasted_iota(jnp.int32, f.shape, 0)
    prev = jnp.where(row == 0, carry_ref[0:1, :], rolled)
    carry_ref[0:1, :] = f[tm - 1:tm, :]
    return prev


def _prep_kernel(f_ref, p_ref, *refs, t):
    f = f_ref[...]
    tm = f.shape[0]
    before = jnp.broadcast_to(p_ref[...], (tm // t, t, f.shape[1])).reshape(tm, f.shape[1])
    row = lax.broadcasted_iota(jnp.int32, f.shape, 0)
    prev = jnp.where(row % t == 0, before, pltpu.roll(f, 1, 0))
    scan_ref, gate_ref = refs[N_PREP_PARAMS:N_PREP_PARAMS + 2]
    _prep_math(f, prev, refs[:N_PREP_PARAMS], _prep_out_views(scan_ref, gate_ref, slice(None)))


def _inproj_prep_kernel(sink_ref, x_ref, g_ref, w_ref, *refs, blocks_per_seq):
    att_ref, k_ref, v_ref = refs[N_PREP_PARAMS:N_PREP_PARAMS + 3]
    scan_ref, gate_ref = refs[N_PREP_PARAMS + 3:N_PREP_PARAMS + 5]
    carry_ref, kprev_ref, vprev_ref = refs[N_PREP_PARAMS + 5:]
    first = pl.program_id(0) % blocks_per_seq == 0

    @pl.when(first)
    def _():
        carry_ref[...] = jnp.zeros_like(carry_ref)
        kprev_ref[...] = jnp.zeros_like(kprev_ref)
        vprev_ref[...] = jnp.zeros_like(vprev_ref)

    tm = x_ref.shape[0]
    sub = min(tm, INPROJ_SUB_ROWS)
    parts = [slice(i * sub, (i + 1) * sub) for i in range(tm // sub)]
    proj = [_dot(_rms(x_ref[rows, :], g_ref[...]).astype(BF16), w_ref[...]) for rows in parts]
    for rows, pj in zip(parts, proj):
        k_ref[rows, :] = pj[:, ATT_WIDTH:ATT_WIDTH + KV_WIDTH]
        v_ref[rows, :] = pj[:, ATT_WIDTH + KV_WIDTH:ATT_PROJ]
    for rows, pj in zip(parts, proj):
        f = pj[:, ATT_PROJ:]
        _prep_math(f, _carried_prev(f, carry_ref), refs[:N_PREP_PARAMS], _prep_out_views(scan_ref, gate_ref, rows))

    def blk(i, c0, c1):
        r0 = i * WINDOW
        return proj[r0 // sub][r0 % sub:r0 % sub + WINDOW, c0:c1]

    nblk = tm // WINDOW
    kcol, vcol = (ATT_WIDTH, ATT_WIDTH + KV_WIDTH), (ATT_WIDTH + KV_WIDTH, ATT_PROJ)
    ops = [_kv_operands(kprev_ref[...], vprev_ref[...])] + [_kv_operands(blk(i, *kcol), blk(i, *vcol))
                                                           for i in range(nblk)]
    band, kj = _band_mask(WINDOW, WINDOW)
    for i in range(nblk):
        mask = band & (jnp.logical_not(first) | (kj >= WINDOW)) if i == 0 else band
        kv_ops = [tuple(jnp.concatenate([ops[i][kv][n], ops[i + 1][kv][n]], axis=0) for n in range(3))
                  for kv in range(KV_HEADS)]
        _attend(blk(i, 0, ATT_WIDTH).astype(BF16), kv_ops, mask, sink_ref,
                att_ref.at[i * WINDOW:(i + 1) * WINDOW, :])
    kprev_ref[...] = blk(nblk - 1, *kcol)
    vprev_ref[...] = blk(nblk - 1, *vcol)


def _prep_math(f, prev, params, outs):
    (mu_ref, w0_ref, w2_ref, a0_ref, a2_ref, g2_ref, kk_ref, ka_ref, rk_ref, ones_ref) = params
    (r_out, l_out, k_out, v_out, kk_out, kb_out, g_out, bo_out) = outs
    xs = f + (prev - f) * mu_ref[...]
    w3 = 3 * RWKV_WIDTH
    r = xs[:, :RWKV_WIDTH]
    k = xs[:, RWKV_WIDTH:2 * RWKV_WIDTH]
    v = xs[:, 2 * RWKV_WIDTH:w3]
    wl = xs[:, w3:w3 + DECAY_LORA]
    al = xs[:, w3 + DECAY_LORA:w3 + DECAY_LORA + ICL_LORA]
    gl = xs[:, w3 + DECAY_LORA + ICL_LORA:]
    log_decay = -DECAY_SCALE * _sigmoid(w0_ref[...] + _dot(jnp.tanh(wl).astype(BF16), w2_ref[...]))
    a = _sigmoid(a0_ref[...] + _dot(al.astype(BF16), a2_ref[...]))
    g = _dot(_sigmoid(gl).astype(BF16), g2_ref[...])
    ones_bd = ones_ref[...]
    kk = k * kk_ref[...]
    kk = kk * lax.rsqrt(_seg_sum(kk * kk, ones_bd) + L2_EPS)
    k2 = k * (1.0 + (a - 1.0) * ka_ref[...])
    r_out[...] = r
    l_out[...] = log_decay
    k_out[...] = k2
    v_out[...] = v
    kk_out[...] = kk
    kb_out[...] = kk * a
    g_out[...] = g
    bo_out[...] = _seg_sum(r * k2 * rk_ref[...], ones_bd) * v


def _prep_params(p):
    w = RWKV_WIDTH
    ins = [p["mu"], p["w0"], p["w2"], p["a0"], p["a2"], p["g2"], p["k_k"], p["k_a"], p["r_k"], p["ones_bd"]]
    specs = [_const_spec((1, RWKV_BLOCK)), _const_spec((1, w)), _const_spec((DECAY_LORA, w)),
             _const_spec((1, w)), _const_spec((ICL_LORA, w)), _const_spec((GATE_LORA, w)),
             _const_spec((1, w)), _const_spec((1, w)), _const_spec((1, w)),
             _const_spec((LANES, LANES))]
    assert len(ins) == N_PREP_PARAMS
    return ins, specs


def _inproj_prep(x, p, *, tm, blocks_per_seq):
    n = x.shape[0]
    assert tm % WINDOW == 0
    row = lambda w: pl.BlockSpec((tm, w), lambda i: (i, 0))
    w = RWKV_WIDTH
    prm, prm_specs = _prep_params(p)
    return pl.pallas_call(
        functools.partial(_inproj_prep_kernel, blocks_per_seq=blocks_per_seq),
        grid=(n // tm,),
        in_specs=[pl.BlockSpec(memory_space=pltpu.SMEM), row(D_MODEL), _const_spec((1, D_MODEL)),
                  _const_spec((D_MODEL, PROJ_WIDTH))] + prm_specs,
        out_specs=[row(ATT_WIDTH), row(KV_WIDTH), row(KV_WIDTH), row(N_SCAN_COLS * w), row(N_GATE_COLS * w)],
        out_shape=[jax.ShapeDtypeStruct((n, ATT_WIDTH), BF16), jax.ShapeDtypeStruct((n, KV_WIDTH), F32),
                   jax.ShapeDtypeStruct((n, KV_WIDTH), F32), jax.ShapeDtypeStruct((n, N_SCAN_COLS * w), F32),
                   jax.ShapeDtypeStruct((n, N_GATE_COLS * w), F32)],
        scratch_shapes=[pltpu.VMEM((8, RWKV_BLOCK), F32), pltpu.VMEM((WINDOW, KV_WIDTH), F32),
                        pltpu.VMEM((WINDOW, KV_WIDTH), F32)],
        compiler_params=_params("arbitrary"),
        name="inproj_prep_attn",
    )(p["sinks"], x, p["g_mix"].reshape(1, D_MODEL), p["w_in"], *prm)


def _prep(feat, feat_before, p, *, t, tm):
    n = feat.shape[0]
    assert tm % t == 0
    row = lambda w: pl.BlockSpec((tm, w), lambda i: (i, 0))
    w = RWKV_WIDTH
    prm, prm_specs = _prep_params(p)
    return pl.pallas_call(
        functools.partial(_prep_kernel, t=t),
        grid=(n // tm,),
        in_specs=[row(RWKV_BLOCK), pl.BlockSpec((tm // t, 1, RWKV_BLOCK), lambda i: (i, 0, 0))] + prm_specs,
        out_specs=[row(N_SCAN_COLS * w), row(N_GATE_COLS * w)],
        out_shape=[jax.ShapeDtypeStruct((n, N_SCAN_COLS * w), F32), jax.ShapeDtypeStruct((n, N_GATE_COLS * w), F32)],
        compiler_params=_params("arbitrary"),
        name="rwkv_prep",
    )(feat, feat_before.reshape(n // t, 1, RWKV_BLOCK), *prm)


def _kv_operands(kblk, vblk):
    lo_m = lax.broadcasted_iota(jnp.int32, (1, LANES), 1) < HEAD_DIM
    k_sw = pltpu.roll(kblk, HEAD_DIM, 1)
    v_sw = pltpu.roll(vblk, HEAD_DIM, 1)
    ops = []
    for kv in range(KV_HEADS):
        own = lo_m if kv == 0 else ~lo_m
        kdup = jnp.where(own, kblk, k_sw).astype(BF16)
        v_own = jnp.where(own, vblk, 0.0).astype(BF16)
        v_oth = jnp.where(own, 0.0, v_sw).astype(BF16)
        ops.append((kdup,) + ((v_own, v_oth) if kv == 0 else (v_oth, v_own)))
    return ops


def _attend(q, kv_ops, mask, sink_ref, o_ref):
    nq = q.shape[0]
    lo_m = lax.broadcasted_iota(jnp.int32, (1, LANES), 1) < HEAD_DIM
    mask2 = jnp.concatenate([mask, mask], axis=0)
    rowi = lax.broadcasted_iota(jnp.int32, (2 * nq, 1), 0)
    for kv in range(KV_HEADS):
        kdup, v_lo, v_hi = kv_ops[kv]
        for pr in range(2):
            c0 = (kv * 2 + pr) * LANES
            qp = q[:, c0:c0 + LANES].astype(F32) * (HEAD_DIM ** -0.5)
            qs = jnp.concatenate([jnp.where(lo_m, qp, 0.0), jnp.where(lo_m, 0.0, qp)], axis=0)
            s = jnp.where(mask2, _dot(qs.astype(BF16), kdup, NT), NEG_INF)
            h0 = (kv * 2 + pr) * 2
            sink = jnp.where(rowi < nq, sink_ref[h0], sink_ref[h0 + 1])
            m = jnp.maximum(jnp.max(s, axis=-1, keepdims=True), sink)
            e = jnp.exp(s - m)
            rden = 1.0 / (jnp.sum(e, axis=-1, keepdims=True) + jnp.exp(sink - m))
            out = _dot(e[:nq].astype(BF16), v_lo) + _dot(e[nq:].astype(BF16), v_hi)
            out = out * jnp.where(lo_m, rden[:nq], rden[nq:])
            o_ref[:, c0:c0 + LANES] = out.astype(o_ref.dtype)


def _band_mask(nq, q0):
    qi = lax.broadcasted_iota(jnp.int32, (nq, 2 * WINDOW), 0) + q0
    kj = lax.broadcasted_iota(jnp.int32, (nq, 2 * WINDOW), 1)
    return (kj <= qi) & (qi - kj < WINDOW), kj


def _attn_sample_kernel(sink_ref, q_ref, kn_ref, vn_ref, ck_ref, cv_ref, o_ref, kw_ref, vw_ref, *, seqs, t):
    wb = ck_ref.shape[3]
    lane = lax.broadcasted_iota(jnp.int32, (1, LANES), 1)
    lo_m = lane < HEAD_DIM
    rowi = lax.broadcasted_iota(jnp.int32, (2 * t, 1), 0)
    top = rowi < t
    qpos = jnp.where(top, rowi, rowi - t)
    keyl = lax.broadcasted_iota(jnp.int32, (2 * t, LANES), 1)
    mask_c = keyl > qpos
    mask_n = keyl <= qpos
    zpad = jnp.zeros((LANES - t, KV_WIDTH), F32)
    zhalf = jnp.zeros((HEAD_DIM, LANES), BF16)
    rows = [slice(s * t, (s + 1) * t) for s in range(seqs)]
    k_t = [ck_ref[s].reshape(KV_WIDTH, wb) for s in range(seqs)]
    v_t = [cv_ref[s].reshape(KV_WIDTH, wb) for s in range(seqs)]
    kn_t = [jnp.concatenate([kn_ref[r, :], zpad], axis=0).T for r in rows]
    vn_t = [jnp.concatenate([vn_ref[r, :], zpad], axis=0).T for r in rows]
    for s in range(seqs):
        kw_ref[s] = pltpu.roll(jnp.where(lane < t, kn_t[s], k_t[s]), LANES - t, 1).reshape(KV_HEADS, HEAD_DIM, wb)
        vw_ref[s] = pltpu.roll(jnp.where(lane < t, vn_t[s], v_t[s]), LANES - t, 1).reshape(KV_HEADS, HEAD_DIM, wb)
    k_b, v_b, kn_b, vn_b = ([x.astype(BF16) for x in xs] for xs in (k_t, v_t, kn_t, vn_t))
    items = [(s, kv, pr) for s in range(seqs) for kv in range(KV_HEADS) for pr in range(2)]
    each = lambda f: [f(*it) for it in items]
    hs = lambda kv: slice(kv * HEAD_DIM, (kv + 1) * HEAD_DIM)
    dup = lambda x, kv: jnp.concatenate([x[hs(kv)], x[hs(kv)]], axis=0)
    low = lambda x, kv: jnp.concatenate([x[hs(kv)], zhalf], axis=0)
    high = lambda x, kv: jnp.concatenate([zhalf, x[hs(kv)]], axis=0)
    col0 = lambda kv, pr: (kv * 2 + pr) * LANES

    def stacked_query(s, kv, pr):
        qp = q_ref[rows[s], col0(kv, pr):col0(kv, pr) + LANES] * (HEAD_DIM ** -0.5)
        return jnp.concatenate([jnp.where(lo_m, qp, 0.0), jnp.where(lo_m, 0.0, qp)], axis=0).astype(BF16)

    qs = each(stacked_query)
    s_c = [jnp.where(mask_c, _dot(qs[i], dup(k_b[s], kv)), NEG_INF) for i, (s, kv, pr) in enumerate(items)]
    s_n = [jnp.where(mask_n, _dot(qs[i], dup(kn_b[s], kv)), NEG_INF) for i, (s, kv, pr) in enumerate(items)]
    half = lambda e, keep_top: jnp.where(top == keep_top, e, 0.0).astype(BF16)
    e_c, e_n, rden = [], [], []
    for i, (s, kv, pr) in enumerate(items):
        h0 = (kv * 2 + pr) * 2
        sink = jnp.where(top, sink_ref[h0], sink_ref[h0 + 1])
        m = jnp.maximum(jnp.maximum(jnp.max(s_c[i], axis=-1, keepdims=True),
                                    jnp.max(s_n[i], axis=-1, keepdims=True)), sink)
        ec = jnp.exp(s_c[i] - m)
        en = jnp.exp(s_n[i] - m)
        rden.append(1.0 / (jnp.sum(ec, axis=-1, keepdims=True) + jnp.sum(en, axis=-1, keepdims=True)
                           + jnp.exp(sink - m)))
        e_c.append((half(ec, True), half(ec, False)))
        e_n.append((half(en, True), half(en, False)))
    res = [(_dot(e_c[i][0], low(v_b[s], kv), NT) + _dot(e_c[i][1], high(v_b[s], kv), NT)
            + _dot(e_n[i][0], low(vn_b[s], kv), NT) + _dot(e_n[i][1], high(vn_b[s], kv), NT)) * rden[i]
           for i, (s, kv, pr) in enumerate(items)]
    for i, (s, kv, pr) in enumerate(items):
        o_ref[rows[s], col0(kv, pr):col0(kv, pr) + LANES] = res[i][:t] + res[i][t:]


def _attn_sample(q, k, v, cache_kt, cache_vt, sinks, *, t, seqs):
    b, _, _, wb = cache_kt.shape
    assert wb == LANES and t % 8 == 0 and t <= LANES
    n = b * t
    row = lambda w: pl.BlockSpec((seqs * t, w), lambda i: (i, 0))
    win = pl.BlockSpec((seqs, KV_HEADS, HEAD_DIM, wb), lambda i: (i, 0, 0, 0))
    return pl.pallas_call(
        functools.partial(_attn_sample_kernel, seqs=seqs, t=t),
        grid=(b // seqs,),
        in_specs=[pl.BlockSpec(memory_space=pltpu.SMEM), row(ATT_WIDTH), row(KV_WIDTH), row(KV_WIDTH), win, win],
        out_specs=[row(ATT_WIDTH), win, win],
        out_shape=[jax.ShapeDtypeStruct((n, ATT_WIDTH), F32),
                   jax.ShapeDtypeStruct((b, KV_HEADS, HEAD_DIM, wb), F32),
                   jax.ShapeDtypeStruct((b, KV_HEADS, HEAD_DIM, wb), F32)],
        compiler_params=_params("arbitrary"),
        name="attn_sample",
    )(sinks, q, k, v, cache_kt, cache_vt)


def _wkv_kernel(in_ref, s0_ref, o_ref, st_ref, z_ref, *, nseq, nchunks, rows, real):
    c = rows
    pw = LANES
    j = pl.program_id(1)
    npairs = RWKV_HEADS // 2
    pairs = range(npairs)

    @pl.when(j == 0)
    def _():
        for s in range(nseq):
            for p in pairs:
                z_ref[s * npairs + p] = jnp.concatenate([s0_ref[s, 2 * p], s0_ref[s, 2 * p + 1]], axis=0).T

    def pair_iota(nrow):
        return (lax.broadcasted_iota(jnp.int32, (nrow, pw), 0),
                lax.broadcasted_iota(jnp.int32, (nrow, pw), 1) & (HEAD_DIM - 1))

    row, col = pair_iota(c)
    lo = lax.broadcasted_iota(jnp.int32, (1, pw), 1) < HEAD_DIM
    lower2 = jnp.concatenate([row > col, row >= col], axis=0)
    eye = (row == col).astype(F32)
    krow, kcol = pair_iota(HEAD_DIM)
    eye_ch = (krow == kcol).astype(F32)
    ri = lax.broadcasted_iota(jnp.int32, (c, c), 0)
    tri = (ri >= lax.broadcasted_iota(jnp.int32, (c, c), 1)).astype(BF16)

    def bd(y):
        zero = jnp.zeros_like(y)
        parts = [jnp.where(lo, y, zero), jnp.where(lo, zero, y)]
        if y.shape[0] < HEAD_DIM:
            fill = jnp.zeros((HEAD_DIM - y.shape[0], pw), y.dtype)
            parts = [parts[0], fill, parts[1], fill]
        return jnp.concatenate(parts, axis=0)

    memo = {}

    def cached(x, tag, build):
        key = (id(x), tag)
        if key not in memo:
            memo[key] = (x, build())
        return memo[key][1]

    def split(x):
        return cached(x, "split", lambda: _split2(x))

    def lhs(x, passes, axis=1):
        hi, low = split(x)
        return cached(x, ("lhs", passes, axis), lambda: jnp.concatenate([hi, low, hi][:passes], axis=axis))

    def rhs(y, passes, axis=0):
        hi, low = split(y)

        def build():
            bh = bd(hi)
            return jnp.concatenate([bh, bh, bd(low)][:passes], axis=axis)
        return cached(y, ("rhs", passes, axis), build)

    def mm(x, y, group):
        return _dot(lhs(x, WKV_PASSES[group]), rhs(y, WKV_PASSES[group]))

    def mm_nt(x, y, group):
        return _dot(lhs(x, WKV_PASSES[group]), rhs(y, WKV_PASSES[group], 1), NT)

    def mm_tn(x, y, group):
        passes = WKV_PASSES[group]
        yh, yl = split(y)
        full = _dot(lhs(x, passes, 0), jnp.concatenate([yh, yh, yl][:passes], axis=0), TN)
        return jnp.where(lo, full[:HEAD_DIM], full[HEAD_DIM:])

    def prepare(keys, prep):
        for s, q in keys:
            tr = slice(q * real, (q + 1) * real)

            def tok(col):
                x = in_ref[s, tr, col * RWKV_WIDTH:(col + 1) * RWKV_WIDTH]
                return x if real == c else jnp.concatenate([x, jnp.zeros((c - real, x.shape[1]), x.dtype)], axis=0)

            lw = tok(1)
            cum = sum(_dot(tri, part) for part in _split3(lw))
            tot = cum[c - 1:c, :]
            en = jnp.exp(-cum)
            ed = jnp.exp(tot - cum)
            kk = tok(4)
            kb = tok(5)
            kx = tok(2)
            prep[s, q] = dict(a=-kk * jnp.exp(cum - lw), r=tok(0) * jnp.exp(cum), b=kb * en, k=kx * en,
                              bh=kb * ed, kh=kx * ed, v=tok(3), etot=jnp.exp(tot))
            yield

    def chunk_algebra(keys, prep, res):
        items = [(s, q, p) for s, q in keys for p in pairs]
        ps = lambda name, it: prep[it[0], it[1]][name][:, it[2] * pw:(it[2] + 1) * pw]
        each = lambda f: [f(i) for i in range(len(items))]
        pv = {name: each(lambda i: ps(name, items[i])) for name in ("a", "r", "b", "k", "bh", "kh", "v", "etot")}
        ar = each(lambda i: jnp.concatenate([pv["a"][i], pv["r"][i]], axis=0))
        g_b = each(lambda i: jnp.where(lower2, mm_nt(ar[i], pv["b"][i], "gram"), 0.0))
        yield
        g_k = each(lambda i: jnp.where(lower2, mm_nt(ar[i], pv["k"][i], "gram"), 0.0))
        a_rb = each(lambda i: g_b[i][c:])
        yield
        p_m = each(lambda i: g_b[i][:c])
        t_m = each(lambda i: eye + p_m[i])
        n = 2
        while n < c:
            p_m = each(lambda i: mm(p_m[i], p_m[i], "inverse"))
            yield
            t_m = each(lambda i: t_m[i] + mm(p_m[i], t_m[i], "inverse"))
            yield
            n *= 2
        gv = each(lambda i: mm(g_k[i], pv["v"][i], "apply"))
        av = each(lambda i: gv[i][:c])
        yield
        w_m = each(lambda i: mm(t_m[i], pv["a"][i], "apply"))
        yield
        u_m = each(lambda i: mm(t_m[i], av[i], "apply"))
        yield
        r_p = each(lambda i: pv["r"][i] + mm(a_rb[i], w_m[i], "apply"))
        yield
        o_i = each(lambda i: mm(a_rb[i], u_m[i], "apply") + gv[i][c:])
        yield
        m_m = each(lambda i: mm_tn(pv["bh"][i], w_m[i], "state") + eye_ch * pv["etot"][i])
        yield
        bk = each(lambda i: jnp.concatenate([pv["bh"][i], pv["kh"][i]], axis=0))
        uv = each(lambda i: jnp.concatenate([u_m[i], pv["v"][i]], axis=0))
        n_m = each(lambda i: mm_tn(bk[i], uv[i], "state"))
        yield
        for i, it in enumerate(items):
            res[it] = (jnp.concatenate([r_p[i], m_m[i]], axis=0), o_i[i], n_m[i])
        yield

    z = [z_ref[sp] for sp in range(nseq * npairs)]

    def carry(keys, res):
        for s, q in keys:
            for p in pairs:
                rm, o_i, n_m = res[s, q, p]
                sp = s * npairs + p
                oz = mm(rm, z[sp], "carry")
                o_ref[s, q * real:(q + 1) * real, p * pw:(p + 1) * pw] = (oz[:c] + o_i)[:real]
                z[sp] = oz[c:] + n_m
            yield

    def interleave(*gens):
        live = list(gens)
        while live:
            live = [g for g in live if next(g, StopIteration) is not StopIteration]

    keys = [(s, q) for q in range(nchunks) for s in range(nseq)]
    half = len(keys) // 2 if nchunks > 1 else len(keys)
    group_a, group_b = keys[:half], keys[half:]
    prep, res = {}, {}
    interleave(prepare(group_a, prep))
    interleave(chunk_algebra(group_a, prep, res), prepare(group_b, prep))
    interleave(chunk_algebra(group_b, prep, res), carry(group_a, res))
    interleave(carry(group_b, res))
    for sp in range(nseq * npairs):
        z_ref[sp] = z[sp]

    @pl.when(j == pl.num_programs(1) - 1)
    def _():
        for s in range(nseq):
            for p in pairs:
                zt = z_ref[s * npairs + p].T
                st_ref[s, 2 * p] = zt[:HEAD_DIM]
                st_ref[s, 2 * p + 1] = zt[HEAD_DIM:]


def _wkv(scan_in, s0, *, nseq, nchunks, rows, real=None):
    real = rows if real is None else real
    assert real == rows or nchunks == 1
    b, t, wide = scan_in.shape
    w = RWKV_WIDTH
    assert wide == N_SCAN_COLS * w
    blk = nchunks * real
    tok = lambda width: pl.BlockSpec((nseq, blk, width), lambda bi, j: (bi, j, 0))
    st = pl.BlockSpec((nseq, RWKV_HEADS, HEAD_DIM, HEAD_DIM), lambda bi, j: (bi, 0, 0, 0))
    return pl.pallas_call(
        functools.partial(_wkv_kernel, nseq=nseq, nchunks=nchunks, rows=rows, real=real),
        grid=(b // nseq, t // blk),
        in_specs=[tok(wide), st],
        out_specs=[tok(w), st],
        out_shape=[jax.ShapeDtypeStruct((b, t, w), F32),
                   jax.ShapeDtypeStruct((b, RWKV_HEADS, HEAD_DIM, HEAD_DIM), F32)],
        scratch_shapes=[pltpu.VMEM((nseq * RWKV_HEADS // 2, HEAD_DIM, LANES), F32)],
        compiler_params=_params("arbitrary", "arbitrary"),
        name="wkv_scan",
    )(scan_in, s0)


def _post_kernel(o_ref, gate_ref, att_ref, x_ref, gng_ref, gnb_ref, ones_ref, wo_ref, gf_ref,
                 wg_ref, wu_ref, wd_ref, gfin_ref, y_ref, *, final):
    tm = x_ref.shape[0]
    sub = min(tm, POST_SUB_ROWS)
    parts = [slice(i * sub, (i + 1) * sub) for i in range(tm // sub)]
    each = lambda f: [f(i) for i in range(len(parts))]
    ones_bd = ones_ref[...]
    o = each(lambda i: o_ref[parts[i], :])
    mean = each(lambda i: _seg_sum(o[i], ones_bd) * (1.0 / HEAD_DIM))
    oc = each(lambda i: o[i] - mean[i])
    var = each(lambda i: _seg_sum(oc[i] * oc[i], ones_bd) * (1.0 / HEAD_DIM))
    rw = each(lambda i: ((oc[i] * lax.rsqrt(var[i] + GN_EPS) * gng_ref[...] + gnb_ref[...]
                          + gate_ref[parts[i], RWKV_WIDTH:]) * gate_ref[parts[i], :RWKV_WIDTH]).astype(BF16))
    x1 = each(lambda i: x_ref[parts[i], :] + _dot(att_ref[parts[i], :].astype(BF16), wo_ref[:ATT_WIDTH, :])
              + _dot(rw[i], wo_ref[ATT_WIDTH:, :]))
    u = each(lambda i: _rms(x1[i], gf_ref[...]).astype(BF16))
    gate = each(lambda i: _dot(u[i], wg_ref[...]))
    up = each(lambda i: _dot(u[i], wu_ref[...]))
    hid = each(lambda i: (gate[i] * _sigmoid(gate[i]) * up[i]).astype(BF16))
    x2 = each(lambda i: x1[i] + _dot(hid[i], wd_ref[...]))
    for i in range(len(parts)):
        y_ref[parts[i], :] = _rms(x2[i], gfin_ref[...]) if final else x2[i]


def _post(o, gate, att, x, p, *, tm, final):
    n = x.shape[0]
    d_ff = p["w_gate"].shape[1]
    row = lambda w: pl.BlockSpec((tm, w), lambda i: (i, 0))
    once = lambda shape: pl.BlockSpec(shape, lambda i: (0,) * len(shape), pipeline_mode=pl.Buffered(1))
    w = RWKV_WIDTH
    return pl.pallas_call(
        functools.partial(_post_kernel, final=final),
        grid=(n // tm,),
        in_specs=[row(w), row(N_GATE_COLS * w), row(ATT_WIDTH), row(D_MODEL),
                  once((1, w)), once((1, w)), once((LANES, LANES)), once((ATT_WIDTH + w, D_MODEL)),
                  once((1, D_MODEL)),
                  once((D_MODEL, d_ff)), once((D_MODEL, d_ff)), once((d_ff, D_MODEL)), once((1, D_MODEL))],
        out_specs=row(D_MODEL),
        out_shape=jax.ShapeDtypeStruct((n, D_MODEL), F32),
        compiler_params=_params("arbitrary"),
        name="post_ffn",
    )(o, gate, att, x, p["gn_g"], p["gn_b"], p["ones_bd"], p["w_out"], p["g_ffn"],
      p["w_gate"], p["w_up"], p["w_down"], p["g_final"])


def _layer_params(l, g_mix, w_in, attn_sinks, rwkv_mu, w0, w2, a0, a2, g2, k_k, k_a, r_k, gn_g, gn_b,
                  w_out, g_ffn, w_gate, w_up, w_down, g_final):
    vec = lambda a: a.reshape(1, -1).astype(F32)
    hd = jnp.arange(LANES) // HEAD_DIM
    return dict(
        g_mix=g_mix[l], w_in=w_in[l].astype(BF16), sinks=attn_sinks[l].astype(F32),
        mu=vec(rwkv_mu[l]), w0=vec(w0[l]), w2=w2[l].astype(BF16), a0=vec(a0[l]), a2=a2[l].astype(BF16),
        g2=g2[l].astype(BF16), k_k=vec(k_k[l]), k_a=vec(k_a[l]), r_k=vec(r_k[l]),
        gn_g=vec(gn_g[l]), gn_b=vec(gn_b[l]), w_out=w_out[l].astype(BF16), g_ffn=vec(g_ffn[l]),
        w_gate=w_gate[l].astype(BF16), w_up=w_up[l].astype(BF16), w_down=w_down[l].astype(BF16),
        g_final=vec(g_final), ones_bd=(hd[:, None] == hd[None, :]).astype(BF16),
    )


def _pick(n, pref):
    t = pref
    while n % t:
        t //= 2
    return t


def _pick_rows(n, cap, unit):
    assert n % unit == 0
    return unit * max(d for d in range(1, cap // unit + 1) if (n // unit) % d == 0)


def _prompt_layer(x, p, final):
    b, t, d = x.shape
    n = b * t
    x2 = x.reshape(n, d)
    tm = _pick(t, INPROJ_ROWS)
    att, k, v, scan_in, gate = _inproj_prep(x2, p, tm=tm, blocks_per_seq=t // tm)
    s0 = jnp.zeros((b, RWKV_HEADS, HEAD_DIM, HEAD_DIM), F32)
    o, s_new = _wkv(scan_in.reshape(b, t, -1), s0, nseq=_pick(b, WKV_PROMPT_SEQS),
                    nchunks=_pick(t // HEAD_DIM, WKV_PROMPT_CHUNKS), rows=HEAD_DIM)
    y = _post(o.reshape(n, -1), gate, att, x2, p, tm=_pick(n, POST_ROWS), final=final)
    wp = min(WINDOW, t)
    k_win = k.reshape(b, t, KV_WIDTH)[:, t - wp:].reshape(b, wp, KV_HEADS, HEAD_DIM)
    v_win = v.reshape(b, t, KV_WIDTH)[:, t - wp:].reshape(b, wp, KV_HEADS, HEAD_DIM)
    shift = _rms_rows(x[:, -1], p["g_mix"])
    return y.reshape(b, t, d), k_win, v_win, s_new, shift


def _sample_layer(x, h_prev, k_buf, v_buf, s0, p, final):
    b, t, d = x.shape
    n = b * t
    x2 = x.reshape(n, d)
    rows_all = jnp.concatenate([x2, h_prev.astype(x2.dtype)], axis=0)
    q, k, v, feat_all = _inproj(rows_all, p["g_mix"], p["w_in"], norm_rows=n, tm=_pick_rows(n + b, 512, 8))
    feat, feat_prev = feat_all[:n], feat_all[n:]
    scan_in, gate = _prep(feat, feat_prev, p, t=t, tm=_pick(n, 512))
    to_t = lambda a: jnp.transpose(a, (0, 2, 3, 1))
    from_t = lambda a: jnp.transpose(a, (0, 3, 1, 2))
    att, k_win, v_win = _attn_sample(q, k, v, to_t(k_buf), to_t(v_buf), p["sinks"], t=t, seqs=_pick(b, ATTN_SAMPLE_SEQS))
    tp = -(-t // BF16_ROWS) * BF16_ROWS
    assert tp <= HEAD_DIM and t % 8 == 0
    o, s_new = _wkv(scan_in.reshape(b, t, -1), s0, nseq=_pick(b, WKV_SAMPLE_SEQS), nchunks=1, rows=tp, real=t)
    y = _post(o.reshape(n, -1), gate, att, x2, p, tm=_pick(n, POST_ROWS), final=final)
    shift = _rms_rows(x[:, -1], p["g_mix"])
    return y.reshape(b, t, d), from_t(k_win), from_t(v_win), s_new, shift


def kernel(x_prompt, x_sample, cache_k, cache_v, state_wkv, state_shift, g_mix, w_in, attn_sinks, rwkv_mu, w0, w2,
           a0, a2, g2, k_k, k_a, r_k, gn_g, gn_b, w_out, g_ffn, w_gate, w_up, w_down, g_final):
    depth = w_in.shape[0]
    xp, xs = x_prompt, x_sample
    outs_p, outs_s = [], []
    for l in range(depth):
        p = _layer_params(l, g_mix, w_in, attn_sinks, rwkv_mu, w0, w2, a0, a2, g2, k_k, k_a, r_k, gn_g, gn_b,
                          w_out, g_ffn, w_gate, w_up, w_down, g_final)
        final = l == depth - 1
        xp, kp, vp, sp, hp = _prompt_layer(xp, p, final)
        xs, kn, vn, sn, hn = _sample_layer(xs, state_shift[l], cache_k[l], cache_v[l], state_wkv[l], p, final)
        outs_p.append((kp, vp, sp, hp))
        outs_s.append((kn, vn, sn, hn))
    stack = lambda outs, i: jnp.stack([o[i] for o in outs])
    return (xp, xs,
            stack(outs_p, 0), stack(outs_p, 1), stack(outs_p, 2), stack(outs_p, 3),
            stack(outs_s, 0), stack(outs_s, 1), stack(outs_s, 2), stack(outs_s, 3))
```

```python
import functools
import math

import jax
import jax.numpy as jnp
from jax import lax
from jax.experimental import pallas as pl
from jax.experimental.pallas import tpu as pltpu

F32 = jnp.float32
BF16 = jnp.bfloat16

D_MODEL = 1024
HEAD_DIM = 64
ATT_HEADS = 8
KV_HEADS = 2
ATT_WIDTH = ATT_HEADS * HEAD_DIM
KV_WIDTH = KV_HEADS * HEAD_DIM
RWKV_HEADS = 8
RWKV_WIDTH = RWKV_HEADS * HEAD_DIM
WINDOW = 128
DECAY_LORA = 64
ICL_LORA = 64
GATE_LORA = 128
RWKV_BLOCK = 3 * RWKV_WIDTH + DECAY_LORA + ICL_LORA + GATE_LORA
ATT_PROJ = ATT_WIDTH + 2 * KV_WIDTH
PROJ_WIDTH = ATT_PROJ + RWKV_BLOCK
RMS_EPS = 1e-6
GN_EPS = 64e-5
L2_EPS = 1e-12
NEG_INF = -1e30
DECAY_SCALE = math.exp(-0.5)

V7X_VMEM_BYTES = 64 * 1024 * 1024
VMEM_LIMIT_BYTES = V7X_VMEM_BYTES * 3 // 4
LANES = 128

WKV_PROMPT_SEQS = 2
WKV_PROMPT_CHUNKS = 4
WKV_SAMPLE_SEQS = 8
ATTN_SAMPLE_SEQS = 16
WKV_PASSES = dict(gram=1, inverse=1, apply=1, state=3, carry=3)
BF16_ROWS = 16
POST_ROWS = 512
POST_SUB_ROWS = 256
INPROJ_ROWS = 512
INPROJ_SUB_ROWS = 256

NN = (((1,), (0,)), ((), ()))
NT = (((1,), (1,)), ((), ()))
TN = (((0,), (0,)), ((), ()))


def _dot(a, b, dn=NN):
    return lax.dot_general(a, b, dn, preferred_element_type=F32)


def _split2(x):
    hi = x.astype(BF16)
    lo = (x - hi.astype(F32)).astype(BF16)
    return hi, lo


def _split3(x):
    hi = x.astype(BF16)
    r1 = x - hi.astype(F32)
    mid = r1.astype(BF16)
    lo = (r1 - mid.astype(F32)).astype(BF16)
    return hi, mid, lo


def _seg_sum(x, ones_bd):
    hi, lo = _split2(x)
    slabs = [slice(i, i + LANES) for i in range(0, x.shape[1], LANES)]
    return jnp.concatenate([_dot(hi[:, s], ones_bd) + _dot(lo[:, s], ones_bd) for s in slabs], axis=1)


def _rms(x, g):
    return x * lax.rsqrt(jnp.mean(x * x, axis=-1, keepdims=True) + RMS_EPS) * g


def _sigmoid(z):
    return 1.0 / (1.0 + jnp.exp(-z))


def _const_spec(shape):
    return pl.BlockSpec(shape, lambda *_: (0,) * len(shape))


def _params(*sem):
    return pltpu.CompilerParams(dimension_semantics=sem, vmem_limit_bytes=VMEM_LIMIT_BYTES)


def _rms_rows_kernel(x_ref, g_ref, o_ref):
    o_ref[...] = _rms(x_ref[...], g_ref[...])


def _rms_rows(x, g):
    n, d = x.shape
    return pl.pallas_call(
        _rms_rows_kernel,
        out_shape=jax.ShapeDtypeStruct((n, d), F32),
        name="rms_rows",
    )(x, g.reshape(1, d))


def _inproj_kernel(x_ref, g_ref, w_ref, q_ref, k_ref, v_ref, f_ref, *, norm_rows):
    x = x_ref[...]
    row = lax.broadcasted_iota(jnp.int32, (x.shape[0], 1), 0) + pl.program_id(0) * x.shape[0]
    h = jnp.where(row < norm_rows, _rms(x, g_ref[...]), x)
    proj = _dot(h.astype(BF16), w_ref[...])
    q_ref[...] = proj[:, :ATT_WIDTH]
    k_ref[...] = proj[:, ATT_WIDTH:ATT_WIDTH + KV_WIDTH]
    v_ref[...] = proj[:, ATT_WIDTH + KV_WIDTH:ATT_PROJ]
    f_ref[...] = proj[:, ATT_PROJ:]


def _inproj(x, g_mix, w_in_bf, *, norm_rows, tm):
    n = x.shape[0]
    row = lambda w: pl.BlockSpec((tm, w), lambda i: (i, 0))
    return pl.pallas_call(
        functools.partial(_inproj_kernel, norm_rows=norm_rows),
        grid=(n // tm,),
        in_specs=[row(D_MODEL), _const_spec((1, D_MODEL)), _const_spec((D_MODEL, PROJ_WIDTH))],
        out_specs=[row(ATT_WIDTH), row(KV_WIDTH), row(KV_WIDTH), row(RWKV_BLOCK)],
        out_shape=[
            jax.ShapeDtypeStruct((n, ATT_WIDTH), F32),
            jax.ShapeDtypeStruct((n, KV_WIDTH), F32),
            jax.ShapeDtypeStruct((n, KV_WIDTH), F32),
            jax.ShapeDtypeStruct((n, RWKV_BLOCK), F32),
        ],
        compiler_params=_params("arbitrary"),
        name="inproj",
    )(x, g_mix.reshape(1, D_MODEL), w_in_bf)


N_PREP_PARAMS = 10
N_SCAN_COLS = 6
N_GATE_COLS = 2


def _prep_out_views(scan_ref, gate_ref, rows):
    w = RWKV_WIDTH
    return ([scan_ref.at[rows, c * w:(c + 1) * w] for c in range(N_SCAN_COLS)]
            + [gate_ref.at[rows, c * w:(c + 1) * w] for c in range(N_GATE_COLS)])


def _carried_prev(f, carry_ref):
    tm = f.shape[0]
    rolled = pltpu.roll(f, 1, 0)
    row = lax.broadcasted_iota(jnp.int32, f.shape, 0)
    prev = jnp.where(row == 0, carry_ref[0:1, :], rolled)
    carry_ref[0:1, :] = f[tm - 1:tm, :]
    return prev


def _prep_kernel(f_ref, p_ref, *refs, t):
    f = f_ref[...]
    tm = f.shape[0]
    before = jnp.broadcast_to(p_ref[...], (tm // t, t, f.shape[1])).reshape(tm, f.shape[1])
    row = lax.broadcasted_iota(jnp.int32, f.shape, 0)
    prev = jnp.where(row % t == 0, before, pltpu.roll(f, 1, 0))
    scan_ref, gate_ref = refs[N_PREP_PARAMS:N_PREP_PARAMS + 2]
    _prep_math(f, prev, refs[:N_PREP_PARAMS], _prep_out_views(scan_ref, gate_ref, slice(None)))


def _inproj_prep_kernel(sink_ref, x_ref, g_ref, w_ref, *refs, blocks_per_seq):
    att_ref, k_ref, v_ref = refs[N_PREP_PARAMS:N_PREP_PARAMS + 3]
    scan_ref, gate_ref = refs[N_PREP_PARAMS + 3:N_PREP_PARAMS + 5]
    carry_ref, kprev_ref, vprev_ref = refs[N_PREP_PARAMS + 5:]
    first = pl.program_id(0) % blocks_per_seq == 0

    @pl.when(first)
    def _():
        carry_ref[...] = jnp.zeros_like(carry_ref)
        kprev_ref[...] = jnp.zeros_like(kprev_ref)
        vprev_ref[...] = jnp.zeros_like(vprev_ref)

    tm = x_ref.shape[0]
    sub = min(tm, INPROJ_SUB_ROWS)
    parts = [slice(i * sub, (i + 1) * sub) for i in range(tm // sub)]
    proj = [_dot(_rms(x_ref[rows, :], g_ref[...]).astype(BF16), w_ref[...]) for rows in parts]
    for rows, pj in zip(parts, proj):
        k_ref[rows, :] = pj[:, ATT_WIDTH:ATT_WIDTH + KV_WIDTH]
        v_ref[rows, :] = pj[:, ATT_WIDTH + KV_WIDTH:ATT_PROJ]
    for rows, pj in zip(parts, proj):
        f = pj[:, ATT_PROJ:]
        _prep_math(f, _carried_prev(f, carry_ref), refs[:N_PREP_PARAMS], _prep_out_views(scan_ref, gate_ref, rows))

    def blk(i, c0, c1):
        r0 = i * WINDOW
        return proj[r0 // sub][r0 % sub:r0 % sub + WINDOW, c0:c1]

    nblk = tm // WINDOW
    kcol, vcol = (ATT_WIDTH, ATT_WIDTH + KV_WIDTH), (ATT_WIDTH + KV_WIDTH, ATT_PROJ)
    ops = [_kv_operands(kprev_ref[...], vprev_ref[...])] + [_kv_operands(blk(i, *kcol), blk(i, *vcol))
                                                           for i in range(nblk)]
    band, kj = _band_mask(WINDOW, WINDOW)
    for i in range(nblk):
        mask = band & (jnp.logical_not(first) | (kj >= WINDOW)) if i == 0 else band
        kv_ops = [tuple(jnp.concatenate([ops[i][kv][n], ops[i + 1][kv][n]], axis=0) for n in range(3))
                  for kv in range(KV_HEADS)]
        _attend(blk(i, 0, ATT_WIDTH).astype(BF16), kv_ops, mask, sink_ref,
                att_ref.at[i * WINDOW:(i + 1) * WINDOW, :])
    kprev_ref[...] = blk(nblk - 1, *kcol)
    vprev_ref[...] = blk(nblk - 1, *vcol)


def _prep_math(f, prev, params, outs):
    (mu_ref, w0_ref, w2_ref, a0_ref, a2_ref, g2_ref, kk_ref, ka_ref, rk_ref, ones_ref) = params
    (r_out, l_out, k_out, v_out, kk_out, kb_out, g_out, bo_out) = outs
    xs = f + (prev - f) * mu_ref[...]
    w3 = 3 * RWKV_WIDTH
    r = xs[:, :RWKV_WIDTH]
    k = xs[:, RWKV_WIDTH:2 * RWKV_WIDTH]
    v = xs[:, 2 * RWKV_WIDTH:w3]
    wl = xs[:, w3:w3 + DECAY_LORA]
    al = xs[:, w3 + DECAY_LORA:w3 + DECAY_LORA + ICL_LORA]
    gl = xs[:, w3 + DECAY_LORA + ICL_LORA:]
    log_decay = -DECAY_SCALE * _sigmoid(w0_ref[...] + _dot(jnp.tanh(wl).astype(BF16), w2_ref[...]))
    a = _sigmoid(a0_ref[...] + _dot(al.astype(BF16), a2_ref[...]))
    g = _dot(_sigmoid(gl).astype(BF16), g2_ref[...])
    ones_bd = ones_ref[...]
    kk = k * kk_ref[...]
    kk = kk * lax.rsqrt(_seg_sum(kk * kk, ones_bd) + L2_EPS)
    k2 = k * (1.0 + (a - 1.0) * ka_ref[...])
    r_out[...] = r
    l_out[...] = log_decay
    k_out[...] = k2
    v_out[...] = v
    kk_out[...] = kk
    kb_out[...] = kk * a
    g_out[...] = g
    bo_out[...] = _seg_sum(r * k2 * rk_ref[...], ones_bd) * v


def _prep_params(p):
    w = RWKV_WIDTH
    ins = [p["mu"], p["w0"], p["w2"], p["a0"], p["a2"], p["g2"], p["k_k"], p["k_a"], p["r_k"], p["ones_bd"]]
    specs = [_const_spec((1, RWKV_BLOCK)), _const_spec((1, w)), _const_spec((DECAY_LORA, w)),
             _const_spec((1, w)), _const_spec((ICL_LORA, w)), _const_spec((GATE_LORA, w)),
             _const_spec((1, w)), _const_spec((1, w)), _const_spec((1, w)),
             _const_spec((LANES, LANES))]
    assert len(ins) == N_PREP_PARAMS
    return ins, specs


def _inproj_prep(x, p, *, tm, blocks_per_seq):
    n = x.shape[0]
    assert tm % WINDOW == 0
    row = lambda w: pl.BlockSpec((tm, w), lambda i: (i, 0))
    w = RWKV_WIDTH
    prm, prm_specs = _prep_params(p)
    return pl.pallas_call(
        functools.partial(_inproj_prep_kernel, blocks_per_seq=blocks_per_seq),
        grid=(n // tm,),
        in_specs=[pl.BlockSpec(memory_space=pltpu.SMEM), row(D_MODEL), _const_spec((1, D_MODEL)),
                  _const_spec((D_MODEL, PROJ_WIDTH))] + prm_specs,
        out_specs=[row(ATT_WIDTH), row(KV_WIDTH), row(KV_WIDTH), row(N_SCAN_COLS * w), row(N_GATE_COLS * w)],
        out_shape=[jax.ShapeDtypeStruct((n, ATT_WIDTH), BF16), jax.ShapeDtypeStruct((n, KV_WIDTH), F32),
                   jax.ShapeDtypeStruct((n, KV_WIDTH), F32), jax.ShapeDtypeStruct((n, N_SCAN_COLS * w), F32),
                   jax.ShapeDtypeStruct((n, N_GATE_COLS * w), F32)],
        scratch_shapes=[pltpu.VMEM((8, RWKV_BLOCK), F32), pltpu.VMEM((WINDOW, KV_WIDTH), F32),
                        pltpu.VMEM((WINDOW, KV_WIDTH), F32)],
        compiler_params=_params("arbitrary"),
        name="inproj_prep_attn",
    )(p["sinks"], x, p["g_mix"].reshape(1, D_MODEL), p["w_in"], *prm)


def _prep(feat, feat_before, p, *, n, t, tm):
    assert tm % t == 0 and n % tm == 0
    row = lambda w: pl.BlockSpec((tm, w), lambda i: (i, 0))
    w = RWKV_WIDTH
    prm, prm_specs = _prep_params(p)
    return pl.pallas_call(
        functools.partial(_prep_kernel, t=t),
        grid=(n // tm,),
        in_specs=[row(RWKV_BLOCK), pl.BlockSpec((tm // t, 1, RWKV_BLOCK), lambda i: (i, 0, 0))] + prm_specs,
        out_specs=[row(N_SCAN_COLS * w), row(N_GATE_COLS * w)],
        out_shape=[jax.ShapeDtypeStruct((n, N_SCAN_COLS * w), F32), jax.ShapeDtypeStruct((n, N_GATE_COLS * w), F32)],
        compiler_params=_params("arbitrary"),
        name="rwkv_prep",
    )(feat, feat_before.reshape(n // t, 1, RWKV_BLOCK), *prm)


def _kv_operands(kblk, vblk):
    lo_m = lax.broadcasted_iota(jnp.int32, (1, LANES), 1) < HEAD_DIM
    k_sw = pltpu.roll(kblk, HEAD_DIM, 1)
    v_sw = pltpu.roll(vblk, HEAD_DIM, 1)
    ops = []
    for kv in range(KV_HEADS):
        own = lo_m if kv == 0 else ~lo_m
        kdup = jnp.where(own, kblk, k_sw).astype(BF16)
        v_own = jnp.where(own, vblk, 0.0).astype(BF16)
        v_oth = jnp.where(own, 0.0, v_sw).astype(BF16)
        ops.append((kdup,) + ((v_own, v_oth) if kv == 0 else (v_oth, v_own)))
    return ops


def _attend(q, kv_ops, mask, sink_ref, o_ref):
    nq = q.shape[0]
    lo_m = lax.broadcasted_iota(jnp.int32, (1, LANES), 1) < HEAD_DIM
    mask2 = jnp.concatenate([mask, mask], axis=0)
    rowi = lax.broadcasted_iota(jnp.int32, (2 * nq, 1), 0)
    for kv in range(KV_HEADS):
        kdup, v_lo, v_hi = kv_ops[kv]
        for pr in range(2):
            c0 = (kv * 2 + pr) * LANES
            qp = q[:, c0:c0 + LANES].astype(F32) * (HEAD_DIM ** -0.5)
            qs = jnp.concatenate([jnp.where(lo_m, qp, 0.0), jnp.where(lo_m, 0.0, qp)], axis=0)
            s = jnp.where(mask2, _dot(qs.astype(BF16), kdup, NT), NEG_INF)
            h0 = (kv * 2 + pr) * 2
            sink = jnp.where(rowi < nq, sink_ref[h0], sink_ref[h0 + 1])
            m = jnp.maximum(jnp.max(s, axis=-1, keepdims=True), sink)
            e = jnp.exp(s - m)
            rden = 1.0 / (jnp.sum(e, axis=-1, keepdims=True) + jnp.exp(sink - m))
            out = _dot(e[:nq].astype(BF16), v_lo) + _dot(e[nq:].astype(BF16), v_hi)
            out = out * jnp.where(lo_m, rden[:nq], rden[nq:])
            o_ref[:, c0:c0 + LANES] = out.astype(o_ref.dtype)


def _band_mask(nq, q0):
    qi = lax.broadcasted_iota(jnp.int32, (nq, 2 * WINDOW), 0) + q0
    kj = lax.broadcasted_iota(jnp.int32, (nq, 2 * WINDOW), 1)
    return (kj <= qi) & (qi - kj < WINDOW), kj


def _attn_sample_kernel(sink_ref, q_ref, kn_ref, vn_ref, ck_ref, cv_ref, o_ref, kw_ref, vw_ref, *, seqs, t):
    wb = ck_ref.shape[3]
    lane = lax.broadcasted_iota(jnp.int32, (1, LANES), 1)
    lo_m = lane < HEAD_DIM
    rowi = lax.broadcasted_iota(jnp.int32, (2 * t, 1), 0)
    top = rowi < t
    qpos = jnp.where(top, rowi, rowi - t)
    keyl = lax.broadcasted_iota(jnp.int32, (2 * t, LANES), 1)
    mask_c = keyl > qpos
    mask_n = keyl <= qpos
    zpad = jnp.zeros((LANES - t, KV_WIDTH), F32)
    zhalf = jnp.zeros((HEAD_DIM, LANES), BF16)
    rows = [slice(s * t, (s + 1) * t) for s in range(seqs)]
    k_t = [ck_ref[s].reshape(KV_WIDTH, wb) for s in range(seqs)]
    v_t = [cv_ref[s].reshape(KV_WIDTH, wb) for s in range(seqs)]
    kn_t = [jnp.concatenate([kn_ref[r, :], zpad], axis=0).T for r in rows]
    vn_t = [jnp.concatenate([vn_ref[r, :], zpad], axis=0).T for r in rows]
    for s in range(seqs):
        kw_ref[s] = pltpu.roll(jnp.where(lane < t, kn_t[s], k_t[s]), LANES - t, 1).reshape(KV_HEADS, HEAD_DIM, wb)
        vw_ref[s] = pltpu.roll(jnp.where(lane < t, vn_t[s], v_t[s]), LANES - t, 1).reshape(KV_HEADS, HEAD_DIM, wb)
    k_b, v_b, kn_b, vn_b = ([x.astype(BF16) for x in xs] for xs in (k_t, v_t, kn_t, vn_t))
    items = [(s, kv, pr) for s in range(seqs) for kv in range(KV_HEADS) for pr in range(2)]
    each = lambda f: [f(*it) for it in items]
    hs = lambda kv: slice(kv * HEAD_DIM, (kv + 1) * HEAD_DIM)
    dup = lambda x, kv: jnp.concatenate([x[hs(kv)], x[hs(kv)]], axis=0)
    low = lambda x, kv: jnp.concatenate([x[hs(kv)], zhalf], axis=0)
    high = lambda x, kv: jnp.concatenate([zhalf, x[hs(kv)]], axis=0)
    col0 = lambda kv, pr: (kv * 2 + pr) * LANES

    def stacked_query(s, kv, pr):
        qp = q_ref[rows[s], col0(kv, pr):col0(kv, pr) + LANES] * (HEAD_DIM ** -0.5)
        return jnp.concatenate([jnp.where(lo_m, qp, 0.0), jnp.where(lo_m, 0.0, qp)], axis=0).astype(BF16)

    qs = each(stacked_query)
    s_c = [jnp.where(mask_c, _dot(qs[i], dup(k_b[s], kv)), NEG_INF) for i, (s, kv, pr) in enumerate(items)]
    s_n = [jnp.where(mask_n, _dot(qs[i], dup(kn_b[s], kv)), NEG_INF) for i, (s, kv, pr) in enumerate(items)]
    half = lambda e, keep_top: jnp.where(top == keep_top, e, 0.0).astype(BF16)
    e_c, e_n, rden = [], [], []
    for i, (s, kv, pr) in enumerate(items):
        h0 = (kv * 2 + pr) * 2
        sink = jnp.where(top, sink_ref[h0], sink_ref[h0 + 1])
        m = jnp.maximum(jnp.maximum(jnp.max(s_c[i], axis=-1, keepdims=True),
                                    jnp.max(s_n[i], axis=-1, keepdims=True)), sink)
        ec = jnp.exp(s_c[i] - m)
        en = jnp.exp(s_n[i] - m)
        rden.append(1.0 / (jnp.sum(ec, axis=-1, keepdims=True) + jnp.sum(en, axis=-1, keepdims=True)
                           + jnp.exp(sink - m)))
        e_c.append((half(ec, True), half(ec, False)))
        e_n.append((half(en, True), half(en, False)))
    res = [(_dot(e_c[i][0], low(v_b[s], kv), NT) + _dot(e_c[i][1], high(v_b[s], kv), NT)
            + _dot(e_n[i][0], low(vn_b[s], kv), NT) + _dot(e_n[i][1], high(vn_b[s], kv), NT)) * rden[i]
           for i, (s, kv, pr) in enumerate(items)]
    for i, (s, kv, pr) in enumerate(items):
        o_ref[rows[s], col0(kv, pr):col0(kv, pr) + LANES] = res[i][:t] + res[i][t:]


def _attn_sample(q, k, v, cache_kt, cache_vt, sinks, *, t, seqs):
    b, _, _, wb = cache_kt.shape
    assert wb == LANES and t % 8 == 0 and t <= LANES
    n = b * t
    row = lambda w: pl.BlockSpec((seqs * t, w), lambda i: (i, 0))
    win = pl.BlockSpec((seqs, KV_HEADS, HEAD_DIM, wb), lambda i: (i, 0, 0, 0))
    return pl.pallas_call(
        functools.partial(_attn_sample_kernel, seqs=seqs, t=t),
        grid=(b // seqs,),
        in_specs=[pl.BlockSpec(memory_space=pltpu.SMEM), row(ATT_WIDTH), row(KV_WIDTH), row(KV_WIDTH), win, win],
        out_specs=[row(ATT_WIDTH), win, win],
        out_shape=[jax.ShapeDtypeStruct((n, ATT_WIDTH), F32),
                   jax.ShapeDtypeStruct((b, KV_HEADS, HEAD_DIM, wb), F32),
                   jax.ShapeDtypeStruct((b, KV_HEADS, HEAD_DIM, wb), F32)],
        compiler_params=_params("arbitrary"),
        name="attn_sample",
    )(sinks, q, k, v, cache_kt, cache_vt)


def _wkv_kernel(in_ref, s0_ref, o_ref, st_ref, z_ref, *, nseq, nchunks, rows, real):
    c = rows
    pw = LANES
    j = pl.program_id(1)
    npairs = RWKV_HEADS // 2
    pairs = range(npairs)

    @pl.when(j == 0)
    def _():
        for s in range(nseq):
            for p in pairs:
                z_ref[s * npairs + p] = jnp.concatenate([s0_ref[s, 2 * p], s0_ref[s, 2 * p + 1]], axis=0).T

    def pair_iota(nrow):
        return (lax.broadcasted_iota(jnp.int32, (nrow, pw), 0),
                lax.broadcasted_iota(jnp.int32, (nrow, pw), 1) & (HEAD_DIM - 1))

    row, col = pair_iota(c)
    lo = lax.broadcasted_iota(jnp.int32, (1, pw), 1) < HEAD_DIM
    lower2 = jnp.concatenate([row > col, row >= col], axis=0)
    eye = (row == col).astype(F32)
    krow, kcol = pair_iota(HEAD_DIM)
    eye_ch = (krow == kcol).astype(F32)
    ri = lax.broadcasted_iota(jnp.int32, (c, c), 0)
    tri = (ri >= lax.broadcasted_iota(jnp.int32, (c, c), 1)).astype(BF16)

    def bd(y):
        zero = jnp.zeros_like(y)
        parts = [jnp.where(lo, y, zero), jnp.where(lo, zero, y)]
        if y.shape[0] < HEAD_DIM:
            fill = jnp.zeros((HEAD_DIM - y.shape[0], pw), y.dtype)
            parts = [parts[0], fill, parts[1], fill]
        return jnp.concatenate(parts, axis=0)

    memo = {}

    def cached(x, tag, build):
        key = (id(x), tag)
        if key not in memo:
            memo[key] = (x, build())
        return memo[key][1]

    def split(x):
        return cached(x, "split", lambda: _split2(x))

    def lhs(x, passes, axis=1):
        hi, low = split(x)
        return cached(x, ("lhs", passes, axis), lambda: jnp.concatenate([hi, low, hi][:passes], axis=axis))

    def rhs(y, passes, axis=0):
        hi, low = split(y)

        def build():
            bh = bd(hi)
            return jnp.concatenate([bh, bh, bd(low)][:passes], axis=axis)
        return cached(y, ("rhs", passes, axis), build)

    def mm(x, y, group):
        return _dot(lhs(x, WKV_PASSES[group]), rhs(y, WKV_PASSES[group]))

    def mm_nt(x, y, group):
        return _dot(lhs(x, WKV_PASSES[group]), rhs(y, WKV_PASSES[group], 1), NT)

    def mm_tn(x, y, group):
        passes = WKV_PASSES[group]
        yh, yl = split(y)
        full = _dot(lhs(x, passes, 0), jnp.concatenate([yh, yh, yl][:passes], axis=0), TN)
        return jnp.where(lo, full[:HEAD_DIM], full[HEAD_DIM:])

    def prepare(keys, prep):
        for s, q in keys:
            tr = slice(q * real, (q + 1) * real)

            def tok(col):
                x = in_ref[s, tr, col * RWKV_WIDTH:(col + 1) * RWKV_WIDTH]
                return x if real == c else jnp.concatenate([x, jnp.zeros((c - real, x.shape[1]), x.dtype)], axis=0)

            lw = tok(1)
            cum = sum(_dot(tri, part) for part in _split3(lw))
            tot = cum[c - 1:c, :]
            en = jnp.exp(-cum)
            ed = jnp.exp(tot - cum)
            kk = tok(4)
            kb = tok(5)
            kx = tok(2)
            prep[s, q] = dict(a=-kk * jnp.exp(cum - lw), r=tok(0) * jnp.exp(cum), b=kb * en, k=kx * en,
                              bh=kb * ed, kh=kx * ed, v=tok(3), etot=jnp.exp(tot))
            yield

    def chunk_algebra(keys, prep, res):
        items = [(s, q, p) for s, q in keys for p in pairs]
        ps = lambda name, it: prep[it[0], it[1]][name][:, it[2] * pw:(it[2] + 1) * pw]
        each = lambda f: [f(i) for i in range(len(items))]
        pv = {name: each(lambda i: ps(name, items[i])) for name in ("a", "r", "b", "k", "bh", "kh", "v", "etot")}
        ar = each(lambda i: jnp.concatenate([pv["a"][i], pv["r"][i]], axis=0))
        g_b = each(lambda i: jnp.where(lower2, mm_nt(ar[i], pv["b"][i], "gram"), 0.0))
        yield
        g_k = each(lambda i: jnp.where(lower2, mm_nt(ar[i], pv["k"][i], "gram"), 0.0))
        a_rb = each(lambda i: g_b[i][c:])
        yield
        p_m = each(lambda i: g_b[i][:c])
        t_m = each(lambda i: eye + p_m[i])
        n = 2
        while n < c:
            p_m = each(lambda i: mm(p_m[i], p_m[i], "inverse"))
            yield
            t_m = each(lambda i: t_m[i] + mm(p_m[i], t_m[i], "inverse"))
            yield
            n *= 2
        gv = each(lambda i: mm(g_k[i], pv["v"][i], "apply"))
        av = each(lambda i: gv[i][:c])
        yield
        w_m = each(lambda i: mm(t_m[i], pv["a"][i], "apply"))
        yield
        u_m = each(lambda i: mm(t_m[i], av[i], "apply"))
        yield
        r_p = each(lambda i: pv["r"][i] + mm(a_rb[i], w_m[i], "apply"))
        yield
        o_i = each(lambda i: mm(a_rb[i], u_m[i], "apply") + gv[i][c:])
        yield
        m_m = each(lambda i: mm_tn(pv["bh"][i], w_m[i], "state") + eye_ch * pv["etot"][i])
        yield
        bk = each(lambda i: jnp.concatenate([pv["bh"][i], pv["kh"][i]], axis=0))
        uv = each(lambda i: jnp.concatenate([u_m[i], pv["v"][i]], axis=0))
        n_m = each(lambda i: mm_tn(bk[i], uv[i], "state"))
        yield
        for i, it in enumerate(items):
            res[it] = (jnp.concatenate([r_p[i], m_m[i]], axis=0), o_i[i], n_m[i])
        yield

    z = [z_ref[sp] for sp in range(nseq * npairs)]

    def carry(keys, res):
        for s, q in keys:
            for p in pairs:
                rm, o_i, n_m = res[s, q, p]
                sp = s * npairs + p
                oz = mm(rm, z[sp], "carry")
                o_ref[s, q * real:(q + 1) * real, p * pw:(p + 1) * pw] = (oz[:c] + o_i)[:real]
                z[sp] = oz[c:] + n_m
            yield

    def interleave(*gens):
        live = list(gens)
        while live:
            live = [g for g in live if next(g, StopIteration) is not StopIteration]

    keys = [(s, q) for q in range(nchunks) for s in range(nseq)]
    half = len(keys) // 2 if nchunks > 1 else len(keys)
    group_a, group_b = keys[:half], keys[half:]
    prep, res = {}, {}
    interleave(prepare(group_a, prep))
    interleave(chunk_algebra(group_a, prep, res), prepare(group_b, prep))
    interleave(chunk_algebra(group_b, prep, res), carry(group_a, res))
    interleave(carry(group_b, res))
    for sp in range(nseq * npairs):
        z_ref[sp] = z[sp]

    @pl.when(j == pl.num_programs(1) - 1)
    def _():
        for s in range(nseq):
            for p in pairs:
                zt = z_ref[s * npairs + p].T
                st_ref[s, 2 * p] = zt[:HEAD_DIM]
                st_ref[s, 2 * p + 1] = zt[HEAD_DIM:]


def _wkv(scan_in, s0, *, nseq, nchunks, rows, real=None):
    real = rows if real is None else real
    assert real == rows or nchunks == 1
    b, t, wide = scan_in.shape
    w = RWKV_WIDTH
    assert wide == N_SCAN_COLS * w
    blk = nchunks * real
    tok = lambda width: pl.BlockSpec((nseq, blk, width), lambda bi, j: (bi, j, 0))
    st = pl.BlockSpec((nseq, RWKV_HEADS, HEAD_DIM, HEAD_DIM), lambda bi, j: (bi, 0, 0, 0))
    return pl.pallas_call(
        functools.partial(_wkv_kernel, nseq=nseq, nchunks=nchunks, rows=rows, real=real),
        grid=(b // nseq, t // blk),
        in_specs=[tok(wide), st],
        out_specs=[tok(w), st],
        out_shape=[jax.ShapeDtypeStruct((b, t, w), F32),
                   jax.ShapeDtypeStruct((b, RWKV_HEADS, HEAD_DIM, HEAD_DIM), F32)],
        scratch_shapes=[pltpu.VMEM((nseq * RWKV_HEADS // 2, HEAD_DIM, LANES), F32)],
        compiler_params=_params("arbitrary", "arbitrary"),
        name="wkv_scan",
    )(scan_in, s0)


def _post_kernel(o_ref, gate_ref, att_ref, x_ref, gng_ref, gnb_ref, ones_ref, wo_ref, gf_ref,
                 wg_ref, wu_ref, wd_ref, gfin_ref, y_ref, *, final):
    tm = x_ref.shape[0]
    sub = min(tm, POST_SUB_ROWS)
    parts = [slice(i * sub, (i + 1) * sub) for i in range(tm // sub)]
    each = lambda f: [f(i) for i in range(len(parts))]
    ones_bd = ones_ref[...]
    o = each(lambda i: o_ref[parts[i], :])
    mean = each(lambda i: _seg_sum(o[i], ones_bd) * (1.0 / HEAD_DIM))
    oc = each(lambda i: o[i] - mean[i])
    var = each(lambda i: _seg_sum(oc[i] * oc[i], ones_bd) * (1.0 / HEAD_DIM))
    rw = each(lambda i: ((oc[i] * lax.rsqrt(var[i] + GN_EPS) * gng_ref[...] + gnb_ref[...]
                          + gate_ref[parts[i], RWKV_WIDTH:]) * gate_ref[parts[i], :RWKV_WIDTH]).astype(BF16))
    x1 = each(lambda i: x_ref[parts[i], :] + _dot(att_ref[parts[i], :].astype(BF16), wo_ref[:ATT_WIDTH, :])
              + _dot(rw[i], wo_ref[ATT_WIDTH:, :]))
    u = each(lambda i: _rms(x1[i], gf_ref[...]).astype(BF16))
    gate = each(lambda i: _dot(u[i], wg_ref[...]))
    up = each(lambda i: _dot(u[i], wu_ref[...]))
    hid = each(lambda i: (gate[i] * _sigmoid(gate[i]) * up[i]).astype(BF16))
    x2 = each(lambda i: x1[i] + _dot(hid[i], wd_ref[...]))
    for i in range(len(parts)):
        y_ref[parts[i], :] = _rms(x2[i], gfin_ref[...]) if final else x2[i]


def _post(o, gate, att, x, p, *, tm, final):
    n = x.shape[0]
    d_ff = p["w_gate"].shape[1]
    row = lambda w: pl.BlockSpec((tm, w), lambda i: (i, 0))
    once = lambda shape: pl.BlockSpec(shape, lambda i: (0,) * len(shape), pipeline_mode=pl.Buffered(1))
    w = RWKV_WIDTH
    return pl.pallas_call(
        functools.partial(_post_kernel, final=final),
        grid=(n // tm,),
        in_specs=[row(w), row(N_GATE_COLS * w), row(ATT_WIDTH), row(D_MODEL),
                  once((1, w)), once((1, w)), once((LANES, LANES)), once((ATT_WIDTH + w, D_MODEL)),
                  once((1, D_MODEL)),
                  once((D_MODEL, d_ff)), once((D_MODEL, d_ff)), once((d_ff, D_MODEL)), once((1, D_MODEL))],
        out_specs=row(D_MODEL),
        out_shape=jax.ShapeDtypeStruct((n, D_MODEL), F32),
        compiler_params=_params("arbitrary"),
        name="post_ffn",
    )(o, gate, att, x, p["gn_g"], p["gn_b"], p["ones_bd"], p["w_out"], p["g_ffn"],
      p["w_gate"], p["w_up"], p["w_down"], p["g_final"])


def _layer_params(l, g_mix, w_in, attn_sinks, rwkv_mu, w0, w2, a0, a2, g2, k_k, k_a, r_k, gn_g, gn_b,
                  w_out, g_ffn, w_gate, w_up, w_down, g_final):
    vec = lambda a: a.reshape(1, -1).astype(F32)
    hd = jnp.arange(LANES) // HEAD_DIM
    return dict(
        g_mix=g_mix[l], w_in=w_in[l].astype(BF16), sinks=attn_sinks[l].astype(F32),
        mu=vec(rwkv_mu[l]), w0=vec(w0[l]), w2=w2[l].astype(BF16), a0=vec(a0[l]), a2=a2[l].astype(BF16),
        g2=g2[l].astype(BF16), k_k=vec(k_k[l]), k_a=vec(k_a[l]), r_k=vec(r_k[l]),
        gn_g=vec(gn_g[l]), gn_b=vec(gn_b[l]), w_out=w_out[l].astype(BF16), g_ffn=vec(g_ffn[l]),
        w_gate=w_gate[l].astype(BF16), w_up=w_up[l].astype(BF16), w_down=w_down[l].astype(BF16),
        g_final=vec(g_final), ones_bd=(hd[:, None] == hd[None, :]).astype(BF16),
    )


def _pick(n, pref):
    t = pref
    while n % t:
        t //= 2
    return t


def _pick_rows(n, cap, unit):
    assert n % unit == 0
    return unit * max(d for d in range(1, cap // unit + 1) if (n // unit) % d == 0)


def _prompt_layer(x, p, final):
    b, t, d = x.shape
    n = b * t
    x2 = x.reshape(n, d)
    tm = _pick(t, INPROJ_ROWS)
    att, k, v, scan_in, gate = _inproj_prep(x2, p, tm=tm, blocks_per_seq=t // tm)
    s0 = jnp.zeros((b, RWKV_HEADS, HEAD_DIM, HEAD_DIM), F32)
    o, s_new = _wkv(scan_in.reshape(b, t, -1), s0, nseq=_pick(b, WKV_PROMPT_SEQS),
                    nchunks=_pick(t // HEAD_DIM, WKV_PROMPT_CHUNKS), rows=HEAD_DIM)
    y = _post(o.reshape(n, -1), gate, att, x2, p, tm=_pick(n, POST_ROWS), final=final)
    wp = min(WINDOW, t)
    k_win = k.reshape(b, t, KV_WIDTH)[:, t - wp:].reshape(b, wp, KV_HEADS, HEAD_DIM)
    v_win = v.reshape(b, t, KV_WIDTH)[:, t - wp:].reshape(b, wp, KV_HEADS, HEAD_DIM)
    shift = _rms_rows(x[:, -1], p["g_mix"])
    return y.reshape(b, t, d), k_win, v_win, s_new, shift


def _sample_layer(x, h_prev, k_buf, v_buf, s0, p, final):
    b, t, d = x.shape
    n = b * t
    x2 = x.reshape(n, d)
    rows_all = jnp.concatenate([x2, h_prev.astype(x2.dtype)], axis=0)
    q, k, v, feat_all = _inproj(rows_all, p["g_mix"], p["w_in"], norm_rows=n, tm=_pick_rows(n + b, 512, 8))
    scan_in, gate = _prep(feat_all, feat_all[n:], p, n=n, t=t, tm=_pick(n, 512))
    to_t = lambda a: jnp.transpose(a, (0, 2, 3, 1))
    from_t = lambda a: jnp.transpose(a, (0, 3, 1, 2))
    att, k_win, v_win = _attn_sample(q, k, v, to_t(k_buf), to_t(v_buf), p["sinks"], t=t, seqs=_pick(b, ATTN_SAMPLE_SEQS))
    tp = -(-t // BF16_ROWS) * BF16_ROWS
    assert tp <= HEAD_DIM and t % 8 == 0
    o, s_new = _wkv(scan_in.reshape(b, t, -1), s0, nseq=_pick(b, WKV_SAMPLE_SEQS), nchunks=1, rows=tp, real=t)
    y = _post(o.reshape(n, -1), gate, att, x2, p, tm=_pick(n, POST_ROWS), final=final)
    shift = _rms_rows(x[:, -1], p["g_mix"])
    return y.reshape(b, t, d), from_t(k_win), from_t(v_win), s_new, shift


def kernel(x_prompt, x_sample, cache_k, cache_v, state_wkv, state_shift, g_mix, w_in, attn_sinks, rwkv_mu, w0, w2,
           a0, a2, g2, k_k, k_a, r_k, gn_g, gn_b, w_out, g_ffn, w_gate, w_up, w_down, g_final):
    depth = w_in.shape[0]
    xp, xs = x_prompt, x_sample
    outs_p, outs_s = [], []
    for l in range(depth):
        p = _layer_params(l, g_mix, w_in, attn_sinks, rwkv_mu, w0, w2, a0, a2, g2, k_k, k_a, r_k, gn_g, gn_b,
                          w_out, g_ffn, w_gate, w_up, w_down, g_final)
        final = l == depth - 1
        xp, kp, vp, sp, hp = _prompt_layer(xp, p, final)
        xs, kn, vn, sn, hn = _sample_layer(xs, state_shift[l], cache_k[l], cache_v[l], state_wkv[l], p, final)
        outs_p.append((kp, vp, sp, hp))
        outs_s.append((kn, vn, sn, hn))
    stack = lambda outs, i: jnp.stack([o[i] for o in outs])
    return (xp, xs,
            stack(outs_p, 0), stack(outs_p, 1), stack(outs_p, 2), stack(outs_p, 3),
            stack(outs_s, 0), stack(outs_s, 1), stack(outs_s, 2), stack(outs_s, 3))
```

```python
import functools
import math

import jax
import jax.numpy as jnp
from jax import lax
from jax.experimental import pallas as pl
from jax.experimental.pallas import tpu as pltpu

F32 = jnp.float32
BF16 = jnp.bfloat16

D_MODEL = 1024
HEAD_DIM = 64
ATT_HEADS = 8
KV_HEADS = 2
ATT_WIDTH = ATT_HEADS * HEAD_DIM
KV_WIDTH = KV_HEADS * HEAD_DIM
RWKV_HEADS = 8
RWKV_WIDTH = RWKV_HEADS * HEAD_DIM
WINDOW = 128
DECAY_LORA = 64
ICL_LORA = 64
GATE_LORA = 128
RWKV_BLOCK = 3 * RWKV_WIDTH + DECAY_LORA + ICL_LORA + GATE_LORA
ATT_PROJ = ATT_WIDTH + 2 * KV_WIDTH
PROJ_WIDTH = ATT_PROJ + RWKV_BLOCK
RMS_EPS = 1e-6
GN_EPS = 64e-5
L2_EPS = 1e-12
NEG_INF = -1e30
DECAY_SCALE = math.exp(-0.5)

V7X_VMEM_BYTES = 64 * 1024 * 1024
VMEM_LIMIT_BYTES = V7X_VMEM_BYTES * 3 // 4
POST_VMEM_LIMIT_BYTES = V7X_VMEM_BYTES * 15 // 16
LANES = 128

WKV_PROMPT_SEQS = 2
WKV_PROMPT_CHUNKS = 4
WKV_SAMPLE_SEQS = 8
ATTN_SAMPLE_SEQS = 16
WKV_PASSES = dict(gram=1, inverse=1, apply=1, state=3, carry=3)
BF16_ROWS = 16
POST_ROWS = 512
POST_SUB_ROWS = 256
INPROJ_ROWS = 512
INPROJ_SUB_ROWS = 256

NN = (((1,), (0,)), ((), ()))
NT = (((1,), (1,)), ((), ()))
TN = (((0,), (0,)), ((), ()))


def _dot(a, b, dn=NN):
    return lax.dot_general(a, b, dn, preferred_element_type=F32)


def _split2(x):
    hi = x.astype(BF16)
    lo = (x - hi.astype(F32)).astype(BF16)
    return hi, lo


def _split3(x):
    hi = x.astype(BF16)
    r1 = x - hi.astype(F32)
    mid = r1.astype(BF16)
    lo = (r1 - mid.astype(F32)).astype(BF16)
    return hi, mid, lo


def _seg_sum(x, ones_bd):
    hi, lo = _split2(x)
    slabs = [slice(i, i + LANES) for i in range(0, x.shape[1], LANES)]
    return jnp.concatenate([_dot(hi[:, s], ones_bd) + _dot(lo[:, s], ones_bd) for s in slabs], axis=1)


def _rms(x, g):
    return x * lax.rsqrt(jnp.mean(x * x, axis=-1, keepdims=True) + RMS_EPS) * g


def _sigmoid(z):
    return 1.0 / (1.0 + jnp.exp(-z))


def _const_spec(shape):
    return pl.BlockSpec(shape, lambda *_: (0,) * len(shape))


def _params(*sem, vmem_limit_bytes=VMEM_LIMIT_BYTES):
    return pltpu.CompilerParams(dimension_semantics=sem, vmem_limit_bytes=vmem_limit_bytes)


def _rms_rows_kernel(x_ref, g_ref, o_ref):
    o_ref[...] = _rms(x_ref[...], g_ref[...])


def _rms_rows(x, g):
    n, d = x.shape
    return pl.pallas_call(
        _rms_rows_kernel,
        out_shape=jax.ShapeDtypeStruct((n, d), F32),
        name="rms_rows",
    )(x, g.reshape(1, d))


def _inproj_kernel(x_ref, g_ref, w_ref, q_ref, k_ref, v_ref, f_ref, *, norm_rows):
    x = x_ref[...]
    row = lax.broadcasted_iota(jnp.int32, (x.shape[0], 1), 0) + pl.program_id(0) * x.shape[0]
    h = jnp.where(row < norm_rows, _rms(x, g_ref[...]), x)
    proj = _dot(h.astype(BF16), w_ref[...])
    q_ref[...] = proj[:, :ATT_WIDTH]
    k_ref[...] = proj[:, ATT_WIDTH:ATT_WIDTH + KV_WIDTH]
    v_ref[...] = proj[:, ATT_WIDTH + KV_WIDTH:ATT_PROJ]
    f_ref[...] = proj[:, ATT_PROJ:]


def _inproj(x, g_mix, w_in_bf, *, norm_rows, tm):
    n = x.shape[0]
    row = lambda w: pl.BlockSpec((tm, w), lambda i: (i, 0))
    return pl.pallas_call(
        functools.partial(_inproj_kernel, norm_rows=norm_rows),
        grid=(n // tm,),
        in_specs=[row(D_MODEL), _const_spec((1, D_MODEL)), _const_spec((D_MODEL, PROJ_WIDTH))],
        out_specs=[row(ATT_WIDTH), row(KV_WIDTH), row(KV_WIDTH), row(RWKV_BLOCK)],
        out_shape=[
            jax.ShapeDtypeStruct((n, ATT_WIDTH), F32),
            jax.ShapeDtypeStruct((n, KV_WIDTH), F32),
            jax.ShapeDtypeStruct((n, KV_WIDTH), F32),
            jax.ShapeDtypeStruct((n, RWKV_BLOCK), F32),
        ],
        compiler_params=_params("arbitrary"),
        name="inproj",
    )(x, g_mix.reshape(1, D_MODEL), w_in_bf)


N_PREP_PARAMS = 10
N_SCAN_COLS = 6
N_GATE_COLS = 2


def _prep_out_views(scan_ref, gate_ref, rows):
    w = RWKV_WIDTH
    return ([scan_ref.at[rows, c * w:(c + 1) * w] for c in range(N_SCAN_COLS)]
            + [gate_ref.at[rows, c * w:(c + 1) * w] for c in range(N_GATE_COLS)])


def _carried_prev(f, carry_ref):
    tm = f.shape[0]
    rolled = pltpu.roll(f, 1, 0)
    row = lax.broadcasted_iota(jnp.int32, f.shape, 0)
    prev = jnp.where(row == 0, carry_ref[0:1, :], rolled)
    carry_ref[0:1, :] = f[tm - 1:tm, :]
    return prev


def _prep_kernel(f_ref, p_ref, *refs, t):
    f = f_ref[...]
    tm = f.shape[0]
    before = jnp.broadcast_to(p_ref[...], (tm // t, t, f.shape[1])).reshape(tm, f.shape[1])
    row = lax.broadcasted_iota(jnp.int32, f.shape, 0)
    prev = jnp.where(row % t == 0, before, pltpu.roll(f, 1, 0))
    scan_ref, gate_ref = refs[N_PREP_PARAMS:N_PREP_PARAMS + 2]
    _prep_math(f, prev, refs[:N_PREP_PARAMS], _prep_out_views(scan_ref, gate_ref, slice(None)))


def _inproj_prep_kernel(sink_ref, x_ref, g_ref, w_ref, *refs, blocks_per_seq):
    att_ref, k_ref, v_ref = refs[N_PREP_PARAMS:N_PREP_PARAMS + 3]
    scan_ref, gate_ref = refs[N_PREP_PARAMS + 3:N_PREP_PARAMS + 5]
    carry_ref, kprev_ref, vprev_ref = refs[N_PREP_PARAMS + 5:]
    first = pl.program_id(0) % blocks_per_seq == 0

    @pl.when(first)
    def _():
        carry_ref[...] = jnp.zeros_like(carry_ref)
        kprev_ref[...] = jnp.zeros_like(kprev_ref)
        vprev_ref[...] = jnp.zeros_like(vprev_ref)

    tm = x_ref.shape[0]
    sub = min(tm, INPROJ_SUB_ROWS)
    parts = [slice(i * sub, (i + 1) * sub) for i in range(tm // sub)]
    proj = [_dot(_rms(x_ref[rows, :], g_ref[...]).astype(BF16), w_ref[...]) for rows in parts]
    for rows, pj in zip(parts, proj):
        k_ref[rows, :] = pj[:, ATT_WIDTH:ATT_WIDTH + KV_WIDTH]
        v_ref[rows, :] = pj[:, ATT_WIDTH + KV_WIDTH:ATT_PROJ]
    for rows, pj in zip(parts, proj):
        f = pj[:, ATT_PROJ:]
        _prep_math(f, _carried_prev(f, carry_ref), refs[:N_PREP_PARAMS], _prep_out_views(scan_ref, gate_ref, rows))

    def blk(i, c0, c1):
        r0 = i * WINDOW
        return proj[r0 // sub][r0 % sub:r0 % sub + WINDOW, c0:c1]

    nblk = tm // WINDOW
    kcol, vcol = (ATT_WIDTH, ATT_WIDTH + KV_WIDTH), (ATT_WIDTH + KV_WIDTH, ATT_PROJ)
    ops = [_kv_operands(kprev_ref[...], vprev_ref[...])] + [_kv_operands(blk(i, *kcol), blk(i, *vcol))
                                                           for i in range(nblk)]
    band, kj = _band_mask(WINDOW, WINDOW)
    for i in range(nblk):
        mask = band & (jnp.logical_not(first) | (kj >= WINDOW)) if i == 0 else band
        kv_ops = [tuple(jnp.concatenate([ops[i][kv][n], ops[i + 1][kv][n]], axis=0) for n in range(3))
                  for kv in range(KV_HEADS)]
        _attend(blk(i, 0, ATT_WIDTH).astype(BF16), kv_ops, mask, sink_ref,
                att_ref.at[i * WINDOW:(i + 1) * WINDOW, :])
    kprev_ref[...] = blk(nblk - 1, *kcol)
    vprev_ref[...] = blk(nblk - 1, *vcol)


def _prep_math(f, prev, params, outs):
    (mu_ref, w0_ref, w2_ref, a0_ref, a2_ref, g2_ref, kk_ref, ka_ref, rk_ref, ones_ref) = params
    (r_out, l_out, k_out, v_out, kk_out, kb_out, g_out, bo_out) = outs
    xs = f + (prev - f) * mu_ref[...]
    w3 = 3 * RWKV_WIDTH
    r = xs[:, :RWKV_WIDTH]
    k = xs[:, RWKV_WIDTH:2 * RWKV_WIDTH]
    v = xs[:, 2 * RWKV_WIDTH:w3]
    wl = xs[:, w3:w3 + DECAY_LORA]
    al = xs[:, w3 + DECAY_LORA:w3 + DECAY_LORA + ICL_LORA]
    gl = xs[:, w3 + DECAY_LORA + ICL_LORA:]
    log_decay = -DECAY_SCALE * _sigmoid(w0_ref[...] + _dot(jnp.tanh(wl).astype(BF16), w2_ref[...]))
    a = _sigmoid(a0_ref[...] + _dot(al.astype(BF16), a2_ref[...]))
    g = _dot(_sigmoid(gl).astype(BF16), g2_ref[...])
    ones_bd = ones_ref[...]
    kk = k * kk_ref[...]
    kk = kk * lax.rsqrt(_seg_sum(kk * kk, ones_bd) + L2_EPS)
    k2 = k * (1.0 + (a - 1.0) * ka_ref[...])
    r_out[...] = r
    l_out[...] = log_decay
    k_out[...] = k2
    v_out[...] = v
    kk_out[...] = kk
    kb_out[...] = kk * a
    g_out[...] = g
    bo_out[...] = _seg_sum(r * k2 * rk_ref[...], ones_bd) * v


def _prep_params(p):
    w = RWKV_WIDTH
    ins = [p["mu"], p["w0"], p["w2"], p["a0"], p["a2"], p["g2"], p["k_k"], p["k_a"], p["r_k"], p["ones_bd"]]
    specs = [_const_spec((1, RWKV_BLOCK)), _const_spec((1, w)), _const_spec((DECAY_LORA, w)),
             _const_spec((1, w)), _const_spec((ICL_LORA, w)), _const_spec((GATE_LORA, w)),
             _const_spec((1, w)), _const_spec((1, w)), _const_spec((1, w)),
             _const_spec((LANES, LANES))]
    assert len(ins) == N_PREP_PARAMS
    return ins, specs


def _inproj_prep(x, p, *, tm, blocks_per_seq):
    n = x.shape[0]
    assert tm % WINDOW == 0
    row = lambda w: pl.BlockSpec((tm, w), lambda i: (i, 0))
    w = RWKV_WIDTH
    prm, prm_specs = _prep_params(p)
    return pl.pallas_call(
        functools.partial(_inproj_prep_kernel, blocks_per_seq=blocks_per_seq),
        grid=(n // tm,),
        in_specs=[pl.BlockSpec(memory_space=pltpu.SMEM), row(D_MODEL), _const_spec((1, D_MODEL)),
                  _const_spec((D_MODEL, PROJ_WIDTH))] + prm_specs,
        out_specs=[row(ATT_WIDTH), row(KV_WIDTH), row(KV_WIDTH), row(N_SCAN_COLS * w), row(N_GATE_COLS * w)],
        out_shape=[jax.ShapeDtypeStruct((n, ATT_WIDTH), BF16), jax.ShapeDtypeStruct((n, KV_WIDTH), F32),
                   jax.ShapeDtypeStruct((n, KV_WIDTH), F32), jax.ShapeDtypeStruct((n, N_SCAN_COLS * w), F32),
                   jax.ShapeDtypeStruct((n, N_GATE_COLS * w), F32)],
        scratch_shapes=[pltpu.VMEM((8, RWKV_BLOCK), F32), pltpu.VMEM((WINDOW, KV_WIDTH), F32),
                        pltpu.VMEM((WINDOW, KV_WIDTH), F32)],
        compiler_params=_params("arbitrary"),
        name="inproj_prep_attn",
    )(p["sinks"], x, p["g_mix"].reshape(1, D_MODEL), p["w_in"], *prm)


def _prep(feat, feat_before, p, *, n, t, tm):
    assert tm % t == 0 and n % tm == 0
    row = lambda w: pl.BlockSpec((tm, w), lambda i: (i, 0))
    w = RWKV_WIDTH
    prm, prm_specs = _prep_params(p)
    return pl.pallas_call(
        functools.partial(_prep_kernel, t=t),
        grid=(n // tm,),
        in_specs=[row(RWKV_BLOCK), pl.BlockSpec((tm // t, 1, RWKV_BLOCK), lambda i: (i, 0, 0))] + prm_specs,
        out_specs=[row(N_SCAN_COLS * w), row(N_GATE_COLS * w)],
        out_shape=[jax.ShapeDtypeStruct((n, N_SCAN_COLS * w), F32), jax.ShapeDtypeStruct((n, N_GATE_COLS * w), F32)],
        compiler_params=_params("arbitrary"),
        name="rwkv_prep",
    )(feat, feat_before.reshape(n // t, 1, RWKV_BLOCK), *prm)


def _kv_operands(kblk, vblk):
    lo_m = lax.broadcasted_iota(jnp.int32, (1, LANES), 1) < HEAD_DIM
    k_sw = pltpu.roll(kblk, HEAD_DIM, 1)
    v_sw = pltpu.roll(vblk, HEAD_DIM, 1)
    ops = []
    for kv in range(KV_HEADS):
        own = lo_m if kv == 0 else ~lo_m
        kdup = jnp.where(own, kblk, k_sw).astype(BF16)
        v_own = jnp.where(own, vblk, 0.0).astype(BF16)
        v_oth = jnp.where(own, 0.0, v_sw).astype(BF16)
        ops.append((kdup,) + ((v_own, v_oth) if kv == 0 else (v_oth, v_own)))
    return ops


def _attend(q, kv_ops, mask, sink_ref, o_ref):
    nq = q.shape[0]
    lo_m = lax.broadcasted_iota(jnp.int32, (1, LANES), 1) < HEAD_DIM
    mask2 = jnp.concatenate([mask, mask], axis=0)
    rowi = lax.broadcasted_iota(jnp.int32, (2 * nq, 1), 0)
    for kv in range(KV_HEADS):
        kdup, v_lo, v_hi = kv_ops[kv]
        for pr in range(2):
            c0 = (kv * 2 + pr) * LANES
            qp = q[:, c0:c0 + LANES].astype(F32) * (HEAD_DIM ** -0.5)
            qs = jnp.concatenate([jnp.where(lo_m, qp, 0.0), jnp.where(lo_m, 0.0, qp)], axis=0)
            s = jnp.where(mask2, _dot(qs.astype(BF16), kdup, NT), NEG_INF)
            h0 = (kv * 2 + pr) * 2
            sink = jnp.where(rowi < nq, sink_ref[h0], sink_ref[h0 + 1])
            m = jnp.maximum(jnp.max(s, axis=-1, keepdims=True), sink)
            e = jnp.exp(s - m)
            rden = 1.0 / (jnp.sum(e, axis=-1, keepdims=True) + jnp.exp(sink - m))
            out = _dot(e[:nq].astype(BF16), v_lo) + _dot(e[nq:].astype(BF16), v_hi)
            out = out * jnp.where(lo_m, rden[:nq], rden[nq:])
            o_ref[:, c0:c0 + LANES] = out.astype(o_ref.dtype)


def _band_mask(nq, q0):
    qi = lax.broadcasted_iota(jnp.int32, (nq, 2 * WINDOW), 0) + q0
    kj = lax.broadcasted_iota(jnp.int32, (nq, 2 * WINDOW), 1)
    return (kj <= qi) & (qi - kj < WINDOW), kj


def _attn_sample_kernel(sink_ref, q_ref, kn_ref, vn_ref, ck_ref, cv_ref, o_ref, kw_ref, vw_ref, *, seqs, t):
    wb = ck_ref.shape[3]
    lane = lax.broadcasted_iota(jnp.int32, (1, LANES), 1)
    lo_m = lane < HEAD_DIM
    rowi = lax.broadcasted_iota(jnp.int32, (2 * t, 1), 0)
    top = rowi < t
    qpos = jnp.where(top, rowi, rowi - t)
    keyl = lax.broadcasted_iota(jnp.int32, (2 * t, LANES), 1)
    mask_c = keyl > qpos
    mask_n = keyl <= qpos
    zpad = jnp.zeros((LANES - t, KV_WIDTH), F32)
    zhalf = jnp.zeros((HEAD_DIM, LANES), BF16)
    rows = [slice(s * t, (s + 1) * t) for s in range(seqs)]
    k_t = [ck_ref[s].reshape(KV_WIDTH, wb) for s in range(seqs)]
    v_t = [cv_ref[s].reshape(KV_WIDTH, wb) for s in range(seqs)]
    kn_t = [jnp.concatenate([kn_ref[r, :], zpad], axis=0).T for r in rows]
    vn_t = [jnp.concatenate([vn_ref[r, :], zpad], axis=0).T for r in rows]
    for s in range(seqs):
        kw_ref[s] = pltpu.roll(jnp.where(lane < t, kn_t[s], k_t[s]), LANES - t, 1).reshape(KV_HEADS, HEAD_DIM, wb)
        vw_ref[s] = pltpu.roll(jnp.where(lane < t, vn_t[s], v_t[s]), LANES - t, 1).reshape(KV_HEADS, HEAD_DIM, wb)
    k_b, v_b, kn_b, vn_b = ([x.astype(BF16) for x in xs] for xs in (k_t, v_t, kn_t, vn_t))
    items = [(s, kv, pr) for s in range(seqs) for kv in range(KV_HEADS) for pr in range(2)]
    each = lambda f: [f(*it) for it in items]
    hs = lambda kv: slice(kv * HEAD_DIM, (kv + 1) * HEAD_DIM)
    dup = lambda x, kv: jnp.concatenate([x[hs(kv)], x[hs(kv)]], axis=0)
    low = lambda x, kv: jnp.concatenate([x[hs(kv)], zhalf], axis=0)
    high = lambda x, kv: jnp.concatenate([zhalf, x[hs(kv)]], axis=0)
    col0 = lambda kv, pr: (kv * 2 + pr) * LANES

    def stacked_query(s, kv, pr):
        qp = q_ref[rows[s], col0(kv, pr):col0(kv, pr) + LANES] * (HEAD_DIM ** -0.5)
        return jnp.concatenate([jnp.where(lo_m, qp, 0.0), jnp.where(lo_m, 0.0, qp)], axis=0).astype(BF16)

    qs = each(stacked_query)
    s_c = [jnp.where(mask_c, _dot(qs[i], dup(k_b[s], kv)), NEG_INF) for i, (s, kv, pr) in enumerate(items)]
    s_n = [jnp.where(mask_n, _dot(qs[i], dup(kn_b[s], kv)), NEG_INF) for i, (s, kv, pr) in enumerate(items)]
    half = lambda e, keep_top: jnp.where(top == keep_top, e, 0.0).astype(BF16)
    e_c, e_n, rden = [], [], []
    for i, (s, kv, pr) in enumerate(items):
        h0 = (kv * 2 + pr) * 2
        sink = jnp.where(top, sink_ref[h0], sink_ref[h0 + 1])
        m = jnp.maximum(jnp.maximum(jnp.max(s_c[i], axis=-1, keepdims=True),
                                    jnp.max(s_n[i], axis=-1, keepdims=True)), sink)
        ec = jnp.exp(s_c[i] - m)
        en = jnp.exp(s_n[i] - m)
        rden.append(1.0 / (jnp.sum(ec, axis=-1, keepdims=True) + jnp.sum(en, axis=-1, keepdims=True)
                           + jnp.exp(sink - m)))
        e_c.append((half(ec, True), half(ec, False)))
        e_n.append((half(en, True), half(en, False)))
    res = [(_dot(e_c[i][0], low(v_b[s], kv), NT) + _dot(e_c[i][1], high(v_b[s], kv), NT)
            + _dot(e_n[i][0], low(vn_b[s], kv), NT) + _dot(e_n[i][1], high(vn_b[s], kv), NT)) * rden[i]
           for i, (s, kv, pr) in enumerate(items)]
    for i, (s, kv, pr) in enumerate(items):
        o_ref[rows[s], col0(kv, pr):col0(kv, pr) + LANES] = res[i][:t] + res[i][t:]


def _attn_sample(q, k, v, cache_kt, cache_vt, sinks, *, t, seqs):
    b, _, _, wb = cache_kt.shape
    assert wb == LANES and t % 8 == 0 and t <= LANES
    n = b * t
    row = lambda w: pl.BlockSpec((seqs * t, w), lambda i: (i, 0))
    win = pl.BlockSpec((seqs, KV_HEADS, HEAD_DIM, wb), lambda i: (i, 0, 0, 0))
    return pl.pallas_call(
        functools.partial(_attn_sample_kernel, seqs=seqs, t=t),
        grid=(b // seqs,),
        in_specs=[pl.BlockSpec(memory_space=pltpu.SMEM), row(ATT_WIDTH), row(KV_WIDTH), row(KV_WIDTH), win, win],
        out_specs=[row(ATT_WIDTH), win, win],
        out_shape=[jax.ShapeDtypeStruct((n, ATT_WIDTH), F32),
                   jax.ShapeDtypeStruct((b, KV_HEADS, HEAD_DIM, wb), F32),
                   jax.ShapeDtypeStruct((b, KV_HEADS, HEAD_DIM, wb), F32)],
        compiler_params=_params("arbitrary"),
        name="attn_sample",
    )(sinks, q, k, v, cache_kt, cache_vt)


def _wkv_kernel(in_ref, s0_ref, o_ref, st_ref, z_ref, *, nseq, nchunks, rows, real):
    c = rows
    pw = LANES
    j = pl.program_id(1)
    npairs = RWKV_HEADS // 2
    pairs = range(npairs)

    @pl.when(j == 0)
    def _():
        for s in range(nseq):
            for p in pairs:
                z_ref[s * npairs + p] = jnp.concatenate([s0_ref[s, 2 * p], s0_ref[s, 2 * p + 1]], axis=0).T

    def pair_iota(nrow):
        return (lax.broadcasted_iota(jnp.int32, (nrow, pw), 0),
                lax.broadcasted_iota(jnp.int32, (nrow, pw), 1) & (HEAD_DIM - 1))

    row, col = pair_iota(c)
    lo = lax.broadcasted_iota(jnp.int32, (1, pw), 1) < HEAD_DIM
    lower2 = jnp.concatenate([row > col, row >= col], axis=0)
    eye = (row == col).astype(F32)
    krow, kcol = pair_iota(HEAD_DIM)
    eye_ch = (krow == kcol).astype(F32)
    ri = lax.broadcasted_iota(jnp.int32, (c, c), 0)
    tri = (ri >= lax.broadcasted_iota(jnp.int32, (c, c), 1)).astype(BF16)

    def bd(y):
        zero = jnp.zeros_like(y)
        parts = [jnp.where(lo, y, zero), jnp.where(lo, zero, y)]
        if y.shape[0] < HEAD_DIM:
            fill = jnp.zeros((HEAD_DIM - y.shape[0], pw), y.dtype)
            parts = [parts[0], fill, parts[1], fill]
        return jnp.concatenate(parts, axis=0)

    memo = {}

    def cached(x, tag, build):
        key = (id(x), tag)
        if key not in memo:
            memo[key] = (x, build())
        return memo[key][1]

    def split(x):
        return cached(x, "split", lambda: _split2(x))

    def lhs(x, passes, axis=1):
        hi, low = split(x)
        return cached(x, ("lhs", passes, axis), lambda: jnp.concatenate([hi, low, hi][:passes], axis=axis))

    def rhs(y, passes, axis=0):
        hi, low = split(y)

        def build():
            bh = bd(hi)
            return jnp.concatenate([bh, bh, bd(low)][:passes], axis=axis)
        return cached(y, ("rhs", passes, axis), build)

    def mm(x, y, group):
        return _dot(lhs(x, WKV_PASSES[group]), rhs(y, WKV_PASSES[group]))

    def mm_nt(x, y, group):
        return _dot(lhs(x, WKV_PASSES[group]), rhs(y, WKV_PASSES[group], 1), NT)

    def mm_tn(x, y, group):
        passes = WKV_PASSES[group]
        yh, yl = split(y)
        full = _dot(lhs(x, passes, 0), jnp.concatenate([yh, yh, yl][:passes], axis=0), TN)
        return jnp.where(lo, full[:HEAD_DIM], full[HEAD_DIM:])

    def prepare(keys, prep):
        for s, q in keys:
            tr = slice(q * real, (q + 1) * real)

            def tok(col):
                x = in_ref[s, tr, col * RWKV_WIDTH:(col + 1) * RWKV_WIDTH]
                return x if real == c else jnp.concatenate([x, jnp.zeros((c - real, x.shape[1]), x.dtype)], axis=0)

            lw = tok(1)
            cum = sum(_dot(tri, part) for part in _split3(lw))
            tot = cum[c - 1:c, :]
            en = jnp.exp(-cum)
            ed = jnp.exp(tot - cum)
            kk = tok(4)
            kb = tok(5)
            kx = tok(2)
            prep[s, q] = dict(a=-kk * jnp.exp(cum - lw), r=tok(0) * jnp.exp(cum), b=kb * en, k=kx * en,
                              bh=kb * ed, kh=kx * ed, v=tok(3), etot=jnp.exp(tot))
            yield

    def chunk_algebra(keys, prep, res):
        items = [(s, q, p) for s, q in keys for p in pairs]
        ps = lambda name, it: prep[it[0], it[1]][name][:, it[2] * pw:(it[2] + 1) * pw]
        each = lambda f: [f(i) for i in range(len(items))]
        pv = {name: each(lambda i: ps(name, items[i])) for name in ("a", "r", "b", "k", "bh", "kh", "v", "etot")}
        ar = each(lambda i: jnp.concatenate([pv["a"][i], pv["r"][i]], axis=0))
        g_b = each(lambda i: jnp.where(lower2, mm_nt(ar[i], pv["b"][i], "gram"), 0.0))
        yield
        g_k = each(lambda i: jnp.where(lower2, mm_nt(ar[i], pv["k"][i], "gram"), 0.0))
        a_rb = each(lambda i: g_b[i][c:])
        yield
        p_m = each(lambda i: g_b[i][:c])
        t_m = each(lambda i: eye + p_m[i])
        n = 2
        while n < c:
            p_m = each(lambda i: mm(p_m[i], p_m[i], "inverse"))
            yield
            t_m = each(lambda i: t_m[i] + mm(p_m[i], t_m[i], "inverse"))
            yield
            n *= 2
        gv = each(lambda i: mm(g_k[i], pv["v"][i], "apply"))
        av = each(lambda i: gv[i][:c])
        yield
        w_m = each(lambda i: mm(t_m[i], pv["a"][i], "apply"))
        yield
        u_m = each(lambda i: mm(t_m[i], av[i], "apply"))
        yield
        r_p = each(lambda i: pv["r"][i] + mm(a_rb[i], w_m[i], "apply"))
        yield
        o_i = each(lambda i: mm(a_rb[i], u_m[i], "apply") + gv[i][c:])
        yield
        m_m = each(lambda i: mm_tn(pv["bh"][i], w_m[i], "state") + eye_ch * pv["etot"][i])
        yield
        bk = each(lambda i: jnp.concatenate([pv["bh"][i], pv["kh"][i]], axis=0))
        uv = each(lambda i: jnp.concatenate([u_m[i], pv["v"][i]], axis=0))
        n_m = each(lambda i: mm_tn(bk[i], uv[i], "state"))
        yield
        for i, it in enumerate(items):
            res[it] = (jnp.concatenate([r_p[i], m_m[i]], axis=0), o_i[i], n_m[i])
        yield

    z = [z_ref[sp] for sp in range(nseq * npairs)]

    def carry(keys, res):
        for s, q in keys:
            for p in pairs:
                rm, o_i, n_m = res[s, q, p]
                sp = s * npairs + p
                oz = mm(rm, z[sp], "carry")
                o_ref[s, q * real:(q + 1) * real, p * pw:(p + 1) * pw] = (oz[:c] + o_i)[:real]
                z[sp] = oz[c:] + n_m
            yield

    def interleave(*gens):
        live = list(gens)
        while live:
            live = [g for g in live if next(g, StopIteration) is not StopIteration]

    keys = [(s, q) for q in range(nchunks) for s in range(nseq)]
    half = len(keys) // 2 if nchunks > 1 else len(keys)
    group_a, group_b = keys[:half], keys[half:]
    prep, res = {}, {}
    interleave(prepare(group_a, prep))
    interleave(chunk_algebra(group_a, prep, res), prepare(group_b, prep))
    interleave(chunk_algebra(group_b, prep, res), carry(group_a, res))
    interleave(carry(group_b, res))
    for sp in range(nseq * npairs):
        z_ref[sp] = z[sp]

    @pl.when(j == pl.num_programs(1) - 1)
    def _():
        for s in range(nseq):
            for p in pairs:
                zt = z_ref[s * npairs + p].T
                st_ref[s, 2 * p] = zt[:HEAD_DIM]
                st_ref[s, 2 * p + 1] = zt[HEAD_DIM:]


def _wkv(scan_in, s0, *, nseq, nchunks, rows, real=None):
    real = rows if real is None else real
    assert real == rows or nchunks == 1
    b, t, wide = scan_in.shape
    w = RWKV_WIDTH
    assert wide == N_SCAN_COLS * w
    blk = nchunks * real
    tok = lambda width: pl.BlockSpec((nseq, blk, width), lambda bi, j: (bi, j, 0))
    st = pl.BlockSpec((nseq, RWKV_HEADS, HEAD_DIM, HEAD_DIM), lambda bi, j: (bi, 0, 0, 0))
    return pl.pallas_call(
        functools.partial(_wkv_kernel, nseq=nseq, nchunks=nchunks, rows=rows, real=real),
        grid=(b // nseq, t // blk),
        in_specs=[tok(wide), st],
        out_specs=[tok(w), st],
        out_shape=[jax.ShapeDtypeStruct((b, t, w), F32),
                   jax.ShapeDtypeStruct((b, RWKV_HEADS, HEAD_DIM, HEAD_DIM), F32)],
        scratch_shapes=[pltpu.VMEM((nseq * RWKV_HEADS // 2, HEAD_DIM, LANES), F32)],
        compiler_params=_params("arbitrary", "arbitrary"),
        name="wkv_scan",
    )(scan_in, s0)


N_POST_ROW_INPUTS = 4
N_POST_WEIGHTS = 9


def _post_kernel(*refs, final, prompt_steps):
    rows_p, rows_s = refs[:N_POST_ROW_INPUTS], refs[N_POST_ROW_INPUTS:2 * N_POST_ROW_INPUTS]
    weights = refs[2 * N_POST_ROW_INPUTS:2 * N_POST_ROW_INPUTS + N_POST_WEIGHTS]
    y_p, y_s = refs[2 * N_POST_ROW_INPUTS + N_POST_WEIGHTS:]
    i = pl.program_id(0)

    @pl.when(i < prompt_steps)
    def _():
        _post_body(*rows_p, *weights, y_p, final=final)

    @pl.when(i >= prompt_steps)
    def _():
        _post_body(*rows_s, *weights, y_s, final=final)


def _post_body(o_ref, gate_ref, att_ref, x_ref, gng_ref, gnb_ref, ones_ref, wo_ref, gf_ref,
               wg_ref, wu_ref, wd_ref, gfin_ref, y_ref, *, final):
    tm = x_ref.shape[0]
    sub = min(tm, POST_SUB_ROWS)
    parts = [slice(i * sub, (i + 1) * sub) for i in range(tm // sub)]
    each = lambda f: [f(i) for i in range(len(parts))]
    ones_bd = ones_ref[...]
    o = each(lambda i: o_ref[parts[i], :])
    mean = each(lambda i: _seg_sum(o[i], ones_bd) * (1.0 / HEAD_DIM))
    oc = each(lambda i: o[i] - mean[i])
    var = each(lambda i: _seg_sum(oc[i] * oc[i], ones_bd) * (1.0 / HEAD_DIM))
    rw = each(lambda i: ((oc[i] * lax.rsqrt(var[i] + GN_EPS) * gng_ref[...] + gnb_ref[...]
                          + gate_ref[parts[i], RWKV_WIDTH:]) * gate_ref[parts[i], :RWKV_WIDTH]).astype(BF16))
    x1 = each(lambda i: x_ref[parts[i], :] + _dot(att_ref[parts[i], :].astype(BF16), wo_ref[:ATT_WIDTH, :])
              + _dot(rw[i], wo_ref[ATT_WIDTH:, :]))
    u = each(lambda i: _rms(x1[i], gf_ref[...]).astype(BF16))
    gate = each(lambda i: _dot(u[i], wg_ref[...]))
    up = each(lambda i: _dot(u[i], wu_ref[...]))
    hid = each(lambda i: (gate[i] * _sigmoid(gate[i]) * up[i]).astype(BF16))
    x2 = each(lambda i: x1[i] + _dot(hid[i], wd_ref[...]))
    for i in range(len(parts)):
        y_ref[parts[i], :] = _rms(x2[i], gfin_ref[...]) if final else x2[i]


def _post(rows_p, rows_s, p, *, tm, final):
    n_p, n_s = rows_p[3].shape[0], rows_s[3].shape[0]
    assert n_p % tm == 0 and n_s % tm == 0
    steps_p, steps_s = n_p // tm, n_s // tm
    d_ff = p["w_gate"].shape[1]
    row_p = lambda w: pl.BlockSpec((tm, w), lambda i: (jnp.minimum(i, steps_p - 1), 0))
    row_s = lambda w: pl.BlockSpec((tm, w), lambda i: (jnp.maximum(i - steps_p, 0), 0))
    once = lambda shape: pl.BlockSpec(shape, lambda i: (0,) * len(shape), pipeline_mode=pl.Buffered(1))
    w = RWKV_WIDTH
    widths = (w, N_GATE_COLS * w, ATT_WIDTH, D_MODEL)
    weights = [p["gn_g"], p["gn_b"], p["ones_bd"], p["w_out"], p["g_ffn"], p["w_gate"], p["w_up"], p["w_down"],
               p["g_final"]]
    assert len(rows_p) == len(rows_s) == N_POST_ROW_INPUTS and len(weights) == N_POST_WEIGHTS
    return pl.pallas_call(
        functools.partial(_post_kernel, final=final, prompt_steps=steps_p),
        grid=(steps_p + steps_s,),
        in_specs=[row_p(x) for x in widths] + [row_s(x) for x in widths]
        + [once((1, w)), once((1, w)), once((LANES, LANES)), once((ATT_WIDTH + w, D_MODEL)), once((1, D_MODEL)),
           once((D_MODEL, d_ff)), once((D_MODEL, d_ff)), once((d_ff, D_MODEL)), once((1, D_MODEL))],
        out_specs=[row_p(D_MODEL), row_s(D_MODEL)],
        out_shape=[jax.ShapeDtypeStruct((n_p, D_MODEL), F32), jax.ShapeDtypeStruct((n_s, D_MODEL), F32)],
        compiler_params=_params("arbitrary", vmem_limit_bytes=POST_VMEM_LIMIT_BYTES),
        name="post_ffn",
    )(*rows_p, *rows_s, *weights)


def _layer_params(l, g_mix, w_in, attn_sinks, rwkv_mu, w0, w2, a0, a2, g2, k_k, k_a, r_k, gn_g, gn_b,
                  w_out, g_ffn, w_gate, w_up, w_down, g_final):
    vec = lambda a: a.reshape(1, -1).astype(F32)
    hd = jnp.arange(LANES) // HEAD_DIM
    return dict(
        g_mix=g_mix[l], w_in=w_in[l].astype(BF16), sinks=attn_sinks[l].astype(F32),
        mu=vec(rwkv_mu[l]), w0=vec(w0[l]), w2=w2[l].astype(BF16), a0=vec(a0[l]), a2=a2[l].astype(BF16),
        g2=g2[l].astype(BF16), k_k=vec(k_k[l]), k_a=vec(k_a[l]), r_k=vec(r_k[l]),
        gn_g=vec(gn_g[l]), gn_b=vec(gn_b[l]), w_out=w_out[l].astype(BF16), g_ffn=vec(g_ffn[l]),
        w_gate=w_gate[l].astype(BF16), w_up=w_up[l].astype(BF16), w_down=w_down[l].astype(BF16),
        g_final=vec(g_final), ones_bd=(hd[:, None] == hd[None, :]).astype(BF16),
    )


def _pick(n, pref):
    t = pref
    while n % t:
        t //= 2
    return t


def _pick_rows(n, cap, unit):
    assert n % unit == 0
    return unit * max(d for d in range(1, cap // unit + 1) if (n // unit) % d == 0)


def _prompt_layer(x, p):
    b, t, d = x.shape
    n = b * t
    x2 = x.reshape(n, d)
    tm = _pick(t, INPROJ_ROWS)
    att, k, v, scan_in, gate = _inproj_prep(x2, p, tm=tm, blocks_per_seq=t // tm)
    s0 = jnp.zeros((b, RWKV_HEADS, HEAD_DIM, HEAD_DIM), F32)
    o, s_new = _wkv(scan_in.reshape(b, t, -1), s0, nseq=_pick(b, WKV_PROMPT_SEQS),
                    nchunks=_pick(t // HEAD_DIM, WKV_PROMPT_CHUNKS), rows=HEAD_DIM)
    wp = min(WINDOW, t)
    k_win = k.reshape(b, t, KV_WIDTH)[:, t - wp:].reshape(b, wp, KV_HEADS, HEAD_DIM)
    v_win = v.reshape(b, t, KV_WIDTH)[:, t - wp:].reshape(b, wp, KV_HEADS, HEAD_DIM)
    shift = _rms_rows(x[:, -1], p["g_mix"])
    return (o.reshape(n, -1), gate, att, x2), k_win, v_win, s_new, shift


def _sample_layer(x, h_prev, k_buf, v_buf, s0, p):
    b, t, d = x.shape
    n = b * t
    x2 = x.reshape(n, d)
    rows_all = jnp.concatenate([x2, h_prev.astype(x2.dtype)], axis=0)
    q, k, v, feat_all = _inproj(rows_all, p["g_mix"], p["w_in"], norm_rows=n, tm=_pick_rows(n + b, 512, 8))
    scan_in, gate = _prep(feat_all, feat_all[n:], p, n=n, t=t, tm=_pick(n, 512))
    to_t = lambda a: jnp.transpose(a, (0, 2, 3, 1))
    from_t = lambda a: jnp.transpose(a, (0, 3, 1, 2))
    att, k_win, v_win = _attn_sample(q, k, v, to_t(k_buf), to_t(v_buf), p["sinks"], t=t, seqs=_pick(b, ATTN_SAMPLE_SEQS))
    tp = -(-t // BF16_ROWS) * BF16_ROWS
    assert tp <= HEAD_DIM and t % 8 == 0
    o, s_new = _wkv(scan_in.reshape(b, t, -1), s0, nseq=_pick(b, WKV_SAMPLE_SEQS), nchunks=1, rows=tp, real=t)
    shift = _rms_rows(x[:, -1], p["g_mix"])
    return (o.reshape(n, -1), gate, att, x2), from_t(k_win), from_t(v_win), s_new, shift


def kernel(x_prompt, x_sample, cache_k, cache_v, state_wkv, state_shift, g_mix, w_in, attn_sinks, rwkv_mu, w0, w2,
           a0, a2, g2, k_k, k_a, r_k, gn_g, gn_b, w_out, g_ffn, w_gate, w_up, w_down, g_final):
    depth = w_in.shape[0]
    xp, xs = x_prompt, x_sample
    outs_p, outs_s = [], []
    for l in range(depth):
        p = _layer_params(l, g_mix, w_in, attn_sinks, rwkv_mu, w0, w2, a0, a2, g2, k_k, k_a, r_k, gn_g, gn_b,
                          w_out, g_ffn, w_gate, w_up, w_down, g_final)
        final = l == depth - 1
        rows_p, kp, vp, sp, hp = _prompt_layer(xp, p)
        rows_s, kn, vn, sn, hn = _sample_layer(xs, state_shift[l], cache_k[l], cache_v[l], state_wkv[l], p)
        tm = _pick(math.gcd(rows_p[3].shape[0], rows_s[3].shape[0]), POST_ROWS)
        yp, ys = _post(rows_p, rows_s, p, tm=tm, final=final)
        xp, xs = yp.reshape(xp.shape), ys.reshape(xs.shape)
        outs_p.append((kp, vp, sp, hp))
        outs_s.append((kn, vn, sn, hn))
    stack = lambda outs, i: jnp.stack([o[i] for o in outs])
    return (xp, xs,
            stack(outs_p, 0), stack(outs_p, 1), stack(outs_p, 2), stack(outs_p, 3),
            stack(outs_s, 0), stack(outs_s, 1), stack(outs_s, 2), stack(outs_s, 3))
```

```python
import functools
import math

import jax
import jax.numpy as jnp
from jax import lax
from jax.experimental import pallas as pl
from jax.experimental.pallas import tpu as pltpu

F32 = jnp.float32
BF16 = jnp.bfloat16

D_MODEL = 1024
HEAD_DIM = 64
ATT_HEADS = 8
KV_HEADS = 2
ATT_WIDTH = ATT_HEADS * HEAD_DIM
KV_WIDTH = KV_HEADS * HEAD_DIM
RWKV_HEADS = 8
RWKV_WIDTH = RWKV_HEADS * HEAD_DIM
WINDOW = 128
DECAY_LORA = 64
ICL_LORA = 64
GATE_LORA = 128
RWKV_BLOCK = 3 * RWKV_WIDTH + DECAY_LORA + ICL_LORA + GATE_LORA
ATT_PROJ = ATT_WIDTH + 2 * KV_WIDTH
PROJ_WIDTH = ATT_PROJ + RWKV_BLOCK
RMS_EPS = 1e-6
GN_EPS = 64e-5
L2_EPS = 1e-12
NEG_INF = -1e30
DECAY_SCALE = math.exp(-0.5)

V7X_VMEM_BYTES = 64 * 1024 * 1024
VMEM_LIMIT_BYTES = V7X_VMEM_BYTES * 3 // 4
LANES = 128

WKV_PROMPT_SEQS = 2
WKV_PROMPT_CHUNKS = 4
WKV_SAMPLE_SEQS = 8
ATTN_SAMPLE_SEQS = 16
WKV_PASSES = dict(gram=1, inverse=1, apply=1, state=3, carry=2)
BF16_ROWS = 16
POST_ROWS = 512
POST_SUB_ROWS = 256
INPROJ_ROWS = 512
INPROJ_SUB_ROWS = 256

NN = (((1,), (0,)), ((), ()))
NT = (((1,), (1,)), ((), ()))
TN = (((0,), (0,)), ((), ()))


def _dot(a, b, dn=NN):
    return lax.dot_general(a, b, dn, preferred_element_type=F32)


def _split2(x):
    hi = x.astype(BF16)
    lo = (x - hi.astype(F32)).astype(BF16)
    return hi, lo


def _split3(x):
    hi = x.astype(BF16)
    r1 = x - hi.astype(F32)
    mid = r1.astype(BF16)
    lo = (r1 - mid.astype(F32)).astype(BF16)
    return hi, mid, lo


def _seg_sum(x, ones_bd):
    hi, lo = _split2(x)
    slabs = [slice(i, i + LANES) for i in range(0, x.shape[1], LANES)]
    return jnp.concatenate([_dot(hi[:, s], ones_bd) + _dot(lo[:, s], ones_bd) for s in slabs], axis=1)


def _rms(x, g):
    return x * lax.rsqrt(jnp.mean(x * x, axis=-1, keepdims=True) + RMS_EPS) * g


def _sigmoid(z):
    return 1.0 / (1.0 + jnp.exp(-z))


def _const_spec(shape):
    return pl.BlockSpec(shape, lambda *_: (0,) * len(shape))


def _params(*sem):
    return pltpu.CompilerParams(dimension_semantics=sem, vmem_limit_bytes=VMEM_LIMIT_BYTES)


def _rms_rows_kernel(x_ref, g_ref, o_ref):
    o_ref[...] = _rms(x_ref[...], g_ref[...])


def _rms_rows(x, g):
    n, d = x.shape
    return pl.pallas_call(
        _rms_rows_kernel,
        out_shape=jax.ShapeDtypeStruct((n, d), F32),
        name="rms_rows",
    )(x, g.reshape(1, d))


def _inproj_kernel(x_ref, g_ref, w_ref, q_ref, k_ref, v_ref, f_ref, *, norm_rows):
    x = x_ref[...]
    row = lax.broadcasted_iota(jnp.int32, (x.shape[0], 1), 0) + pl.program_id(0) * x.shape[0]
    h = jnp.where(row < norm_rows, _rms(x, g_ref[...]), x)
    proj = _dot(h.astype(BF16), w_ref[...])
    q_ref[...] = proj[:, :ATT_WIDTH]
    k_ref[...] = proj[:, ATT_WIDTH:ATT_WIDTH + KV_WIDTH]
    v_ref[...] = proj[:, ATT_WIDTH + KV_WIDTH:ATT_PROJ]
    f_ref[...] = proj[:, ATT_PROJ:]


def _inproj(x, g_mix, w_in_bf, *, norm_rows, tm):
    n = x.shape[0]
    row = lambda w: pl.BlockSpec((tm, w), lambda i: (i, 0))
    return pl.pallas_call(
        functools.partial(_inproj_kernel, norm_rows=norm_rows),
        grid=(n // tm,),
        in_specs=[row(D_MODEL), _const_spec((1, D_MODEL)), _const_spec((D_MODEL, PROJ_WIDTH))],
        out_specs=[row(ATT_WIDTH), row(KV_WIDTH), row(KV_WIDTH), row(RWKV_BLOCK)],
        out_shape=[
            jax.ShapeDtypeStruct((n, ATT_WIDTH), F32),
            jax.ShapeDtypeStruct((n, KV_WIDTH), F32),
            jax.ShapeDtypeStruct((n, KV_WIDTH), F32),
            jax.ShapeDtypeStruct((n, RWKV_BLOCK), F32),
        ],
        compiler_params=_params("arbitrary"),
        name="inproj",
    )(x, g_mix.reshape(1, D_MODEL), w_in_bf)


N_PREP_PARAMS = 10
N_SCAN_COLS = 6
N_GATE_COLS = 2


def _prep_out_views(scan_ref, gate_ref, rows):
    w = RWKV_WIDTH
    return ([scan_ref.at[rows, c * w:(c + 1) * w] for c in range(N_SCAN_COLS)]
            + [gate_ref.at[rows, c * w:(c + 1) * w] for c in range(N_GATE_COLS)])


def _carried_prev(f, carry_ref):
    tm = f.shape[0]
    rolled = pltpu.roll(f, 1, 0)
    row = lax.broadcasted_iota(jnp.int32, f.shape, 0)
    prev = jnp.where(row == 0, carry_ref[0:1, :], rolled)
    carry_ref[0:1, :] = f[tm - 1:tm, :]
    return prev


def _prep_kernel(f_ref, p_ref, *refs, t):
    f = f_ref[...]
    tm = f.shape[0]
    before = jnp.broadcast_to(p_ref[...], (tm // t, t, f.shape[1])).reshape(tm, f.shape[1])
    row = lax.broadcasted_iota(jnp.int32, f.shape, 0)
    prev = jnp.where(row % t == 0, before, pltpu.roll(f, 1, 0))
    scan_ref, gate_ref = refs[N_PREP_PARAMS:N_PREP_PARAMS + 2]
    _prep_math(f, prev, refs[:N_PREP_PARAMS], _prep_out_views(scan_ref, gate_ref, slice(None)))


def _inproj_prep_kernel(sink_ref, x_ref, g_ref, w_ref, *refs, blocks_per_seq):
    att_ref, k_ref, v_ref = refs[N_PREP_PARAMS:N_PREP_PARAMS + 3]
    scan_ref, gate_ref = refs[N_PREP_PARAMS + 3:N_PREP_PARAMS + 5]
    carry_ref, kprev_ref, vprev_ref = refs[N_PREP_PARAMS + 5:]
    first = pl.program_id(0) % blocks_per_seq == 0

    @pl.when(first)
    def _():
        carry_ref[...] = jnp.zeros_like(carry_ref)
        kprev_ref[...] = jnp.zeros_like(kprev_ref)
        vprev_ref[...] = jnp.zeros_like(vprev_ref)

    tm = x_ref.shape[0]
    sub = min(tm, INPROJ_SUB_ROWS)
    parts = [slice(i * sub, (i + 1) * sub) for i in range(tm // sub)]
    proj = [_dot(_rms(x_ref[rows, :], g_ref[...]).astype(BF16), w_ref[...]) for rows in parts]
    for rows, pj in zip(parts, proj):
        k_ref[rows, :] = pj[:, ATT_WIDTH:ATT_WIDTH + KV_WIDTH]
        v_ref[rows, :] = pj[:, ATT_WIDTH + KV_WIDTH:ATT_PROJ]
    for rows, pj in zip(parts, proj):
        f = pj[:, ATT_PROJ:]
        _prep_math(f, _carried_prev(f, carry_ref), refs[:N_PREP_PARAMS], _prep_out_views(scan_ref, gate_ref, rows))

    def blk(i, c0, c1):
        r0 = i * WINDOW
        return proj[r0 // sub][r0 % sub:r0 % sub + WINDOW, c0:c1]

    nblk = tm // WINDOW
    kcol, vcol = (ATT_WIDTH, ATT_WIDTH + KV_WIDTH), (ATT_WIDTH + KV_WIDTH, ATT_PROJ)
    ops = [_kv_operands(kprev_ref[...], vprev_ref[...])] + [_kv_operands(blk(i, *kcol), blk(i, *vcol))
                                                           for i in range(nblk)]
    band, kj = _band_mask(WINDOW, WINDOW)
    for i in range(nblk):
        mask = band & (jnp.logical_not(first) | (kj >= WINDOW)) if i == 0 else band
        kv_ops = [tuple(jnp.concatenate([ops[i][kv][n], ops[i + 1][kv][n]], axis=0) for n in range(3))
                  for kv in range(KV_HEADS)]
        _attend(blk(i, 0, ATT_WIDTH).astype(BF16), kv_ops, mask, sink_ref,
                att_ref.at[i * WINDOW:(i + 1) * WINDOW, :])
    kprev_ref[...] = blk(nblk - 1, *kcol)
    vprev_ref[...] = blk(nblk - 1, *vcol)


def _prep_math(f, prev, params, outs):
    (mu_ref, w0_ref, w2_ref, a0_ref, a2_ref, g2_ref, kk_ref, ka_ref, rk_ref, ones_ref) = params
    (r_out, l_out, k_out, v_out, kk_out, kb_out, g_out, bo_out) = outs
    xs = f + (prev - f) * mu_ref[...]
    w3 = 3 * RWKV_WIDTH
    r = xs[:, :RWKV_WIDTH]
    k = xs[:, RWKV_WIDTH:2 * RWKV_WIDTH]
    v = xs[:, 2 * RWKV_WIDTH:w3]
    wl = xs[:, w3:w3 + DECAY_LORA]
    al = xs[:, w3 + DECAY_LORA:w3 + DECAY_LORA + ICL_LORA]
    gl = xs[:, w3 + DECAY_LORA + ICL_LORA:]
    log_decay = -DECAY_SCALE * _sigmoid(w0_ref[...] + _dot(jnp.tanh(wl).astype(BF16), w2_ref[...]))
    a = _sigmoid(a0_ref[...] + _dot(al.astype(BF16), a2_ref[...]))
    g = _dot(_sigmoid(gl).astype(BF16), g2_ref[...])
    ones_bd = ones_ref[...]
    kk = k * kk_ref[...]
    kk = kk * lax.rsqrt(_seg_sum(kk * kk, ones_bd) + L2_EPS)
    k2 = k * (1.0 + (a - 1.0) * ka_ref[...])
    r_out[...] = r
    l_out[...] = log_decay
    k_out[...] = k2
    v_out[...] = v
    kk_out[...] = kk
    kb_out[...] = kk * a
    g_out[...] = g
    bo_out[...] = _seg_sum(r * k2 * rk_ref[...], ones_bd) * v


def _prep_params(p):
    w = RWKV_WIDTH
    ins = [p["mu"], p["w0"], p["w2"], p["a0"], p["a2"], p["g2"], p["k_k"], p["k_a"], p["r_k"], p["ones_bd"]]
    specs = [_const_spec((1, RWKV_BLOCK)), _const_spec((1, w)), _const_spec((DECAY_LORA, w)),
             _const_spec((1, w)), _const_spec((ICL_LORA, w)), _const_spec((GATE_LORA, w)),
             _const_spec((1, w)), _const_spec((1, w)), _const_spec((1, w)),
             _const_spec((LANES, LANES))]
    assert len(ins) == N_PREP_PARAMS
    return ins, specs


def _inproj_prep(x, p, *, tm, blocks_per_seq):
    n = x.shape[0]
    assert tm % WINDOW == 0
    row = lambda w: pl.BlockSpec((tm, w), lambda i: (i, 0))
    w = RWKV_WIDTH
    prm, prm_specs = _prep_params(p)
    return pl.pallas_call(
        functools.partial(_inproj_prep_kernel, blocks_per_seq=blocks_per_seq),
        grid=(n // tm,),
        in_specs=[pl.BlockSpec(memory_space=pltpu.SMEM), row(D_MODEL), _const_spec((1, D_MODEL)),
                  _const_spec((D_MODEL, PROJ_WIDTH))] + prm_specs,
        out_specs=[row(ATT_WIDTH), row(KV_WIDTH), row(KV_WIDTH), row(N_SCAN_COLS * w), row(N_GATE_COLS * w)],
        out_shape=[jax.ShapeDtypeStruct((n, ATT_WIDTH), BF16), jax.ShapeDtypeStruct((n, KV_WIDTH), F32),
                   jax.ShapeDtypeStruct((n, KV_WIDTH), F32), jax.ShapeDtypeStruct((n, N_SCAN_COLS * w), F32),
                   jax.ShapeDtypeStruct((n, N_GATE_COLS * w), F32)],
        scratch_shapes=[pltpu.VMEM((8, RWKV_BLOCK), F32), pltpu.VMEM((WINDOW, KV_WIDTH), F32),
                        pltpu.VMEM((WINDOW, KV_WIDTH), F32)],
        compiler_params=_params("arbitrary"),
        name="inproj_prep_attn",
    )(p["sinks"], x, p["g_mix"].reshape(1, D_MODEL), p["w_in"], *prm)


def _prep(feat, feat_before, p, *, n, t, tm):
    assert tm % t == 0 and n % tm == 0
    row = lambda w: pl.BlockSpec((tm, w), lambda i: (i, 0))
    w = RWKV_WIDTH
    prm, prm_specs = _prep_params(p)
    return pl.pallas_call(
        functools.partial(_prep_kernel, t=t),
        grid=(n // tm,),
        in_specs=[row(RWKV_BLOCK), pl.BlockSpec((tm // t, 1, RWKV_BLOCK), lambda i: (i, 0, 0))] + prm_specs,
        out_specs=[row(N_SCAN_COLS * w), row(N_GATE_COLS * w)],
        out_shape=[jax.ShapeDtypeStruct((n, N_SCAN_COLS * w), F32), jax.ShapeDtypeStruct((n, N_GATE_COLS * w), F32)],
        compiler_params=_params("arbitrary"),
        name="rwkv_prep",
    )(feat, feat_before.reshape(n // t, 1, RWKV_BLOCK), *prm)


def _kv_operands(kblk, vblk):
    lo_m = lax.broadcasted_iota(jnp.int32, (1, LANES), 1) < HEAD_DIM
    k_sw = pltpu.roll(kblk, HEAD_DIM, 1)
    v_sw = pltpu.roll(vblk, HEAD_DIM, 1)
    ops = []
    for kv in range(KV_HEADS):
        own = lo_m if kv == 0 else ~lo_m
        kdup = jnp.where(own, kblk, k_sw).astype(BF16)
        v_own = jnp.where(own, vblk, 0.0).astype(BF16)
        v_oth = jnp.where(own, 0.0, v_sw).astype(BF16)
        ops.append((kdup,) + ((v_own, v_oth) if kv == 0 else (v_oth, v_own)))
    return ops


def _attend(q, kv_ops, mask, sink_ref, o_ref):
    nq = q.shape[0]
    lo_m = lax.broadcasted_iota(jnp.int32, (1, LANES), 1) < HEAD_DIM
    mask2 = jnp.concatenate([mask, mask], axis=0)
    rowi = lax.broadcasted_iota(jnp.int32, (2 * nq, 1), 0)
    for kv in range(KV_HEADS):
        kdup, v_lo, v_hi = kv_ops[kv]
        for pr in range(2):
            c0 = (kv * 2 + pr) * LANES
            qp = q[:, c0:c0 + LANES].astype(F32) * (HEAD_DIM ** -0.5)
            qs = jnp.concatenate([jnp.where(lo_m, qp, 0.0), jnp.where(lo_m, 0.0, qp)], axis=0)
            s = jnp.where(mask2, _dot(qs.astype(BF16), kdup, NT), NEG_INF)
            h0 = (kv * 2 + pr) * 2
            sink = jnp.where(rowi < nq, sink_ref[h0], sink_ref[h0 + 1])
            m = jnp.maximum(jnp.max(s, axis=-1, keepdims=True), sink)
            e = jnp.exp(s - m)
            rden = 1.0 / (jnp.sum(e, axis=-1, keepdims=True) + jnp.exp(sink - m))
            out = _dot(e[:nq].astype(BF16), v_lo) + _dot(e[nq:].astype(BF16), v_hi)
            out = out * jnp.where(lo_m, rden[:nq], rden[nq:])
            o_ref[:, c0:c0 + LANES] = out.astype(o_ref.dtype)


def _band_mask(nq, q0):
    qi = lax.broadcasted_iota(jnp.int32, (nq, 2 * WINDOW), 0) + q0
    kj = lax.broadcasted_iota(jnp.int32, (nq, 2 * WINDOW), 1)
    return (kj <= qi) & (qi - kj < WINDOW), kj


def _attn_sample_kernel(sink_ref, q_ref, kn_ref, vn_ref, ck_ref, cv_ref, o_ref, kw_ref, vw_ref, *, seqs, t):
    wb = ck_ref.shape[3]
    lane = lax.broadcasted_iota(jnp.int32, (1, LANES), 1)
    lo_m = lane < HEAD_DIM
    rowi = lax.broadcasted_iota(jnp.int32, (2 * t, 1), 0)
    top = rowi < t
    qpos = jnp.where(top, rowi, rowi - t)
    keyl = lax.broadcasted_iota(jnp.int32, (2 * t, LANES), 1)
    mask_c = keyl > qpos
    mask_n = keyl <= qpos
    zpad = jnp.zeros((LANES - t, KV_WIDTH), F32)
    zhalf = jnp.zeros((HEAD_DIM, LANES), BF16)
    rows = [slice(s * t, (s + 1) * t) for s in range(seqs)]
    k_t = [ck_ref[s].reshape(KV_WIDTH, wb) for s in range(seqs)]
    v_t = [cv_ref[s].reshape(KV_WIDTH, wb) for s in range(seqs)]
    kn_t = [jnp.concatenate([kn_ref[r, :], zpad], axis=0).T for r in rows]
    vn_t = [jnp.concatenate([vn_ref[r, :], zpad], axis=0).T for r in rows]
    for s in range(seqs):
        kw_ref[s] = pltpu.roll(jnp.where(lane < t, kn_t[s], k_t[s]), LANES - t, 1).reshape(KV_HEADS, HEAD_DIM, wb)
        vw_ref[s] = pltpu.roll(jnp.where(lane < t, vn_t[s], v_t[s]), LANES - t, 1).reshape(KV_HEADS, HEAD_DIM, wb)
    k_b, v_b, kn_b, vn_b = ([x.astype(BF16) for x in xs] for xs in (k_t, v_t, kn_t, vn_t))
    items = [(s, kv, pr) for s in range(seqs) for kv in range(KV_HEADS) for pr in range(2)]
    each = lambda f: [f(*it) for it in items]
    hs = lambda kv: slice(kv * HEAD_DIM, (kv + 1) * HEAD_DIM)
    dup = lambda x, kv: jnp.concatenate([x[hs(kv)], x[hs(kv)]], axis=0)
    low = lambda x, kv: jnp.concatenate([x[hs(kv)], zhalf], axis=0)
    high = lambda x, kv: jnp.concatenate([zhalf, x[hs(kv)]], axis=0)
    col0 = lambda kv, pr: (kv * 2 + pr) * LANES

    def stacked_query(s, kv, pr):
        qp = q_ref[rows[s], col0(kv, pr):col0(kv, pr) + LANES] * (HEAD_DIM ** -0.5)
        return jnp.concatenate([jnp.where(lo_m, qp, 0.0), jnp.where(lo_m, 0.0, qp)], axis=0).astype(BF16)

    qs = each(stacked_query)
    s_c = [jnp.where(mask_c, _dot(qs[i], dup(k_b[s], kv)), NEG_INF) for i, (s, kv, pr) in enumerate(items)]
    s_n = [jnp.where(mask_n, _dot(qs[i], dup(kn_b[s], kv)), NEG_INF) for i, (s, kv, pr) in enumerate(items)]
    half = lambda e, keep_top: jnp.where(top == keep_top, e, 0.0).astype(BF16)
    e_c, e_n, rden = [], [], []
    for i, (s, kv, pr) in enumerate(items):
        h0 = (kv * 2 + pr) * 2
        sink = jnp.where(top, sink_ref[h0], sink_ref[h0 + 1])
        m = jnp.maximum(jnp.maximum(jnp.max(s_c[i], axis=-1, keepdims=True),
                                    jnp.max(s_n[i], axis=-1, keepdims=True)), sink)
        ec = jnp.exp(s_c[i] - m)
        en = jnp.exp(s_n[i] - m)
        rden.append(1.0 / (jnp.sum(ec, axis=-1, keepdims=True) + jnp.sum(en, axis=-1, keepdims=True)
                           + jnp.exp(sink - m)))
        e_c.append((half(ec, True), half(ec, False)))
        e_n.append((half(en, True), half(en, False)))
    res = [(_dot(e_c[i][0], low(v_b[s], kv), NT) + _dot(e_c[i][1], high(v_b[s], kv), NT)
            + _dot(e_n[i][0], low(vn_b[s], kv), NT) + _dot(e_n[i][1], high(vn_b[s], kv), NT)) * rden[i]
           for i, (s, kv, pr) in enumerate(items)]
    for i, (s, kv, pr) in enumerate(items):
        o_ref[rows[s], col0(kv, pr):col0(kv, pr) + LANES] = res[i][:t] + res[i][t:]


def _attn_sample(q, k, v, cache_kt, cache_vt, sinks, *, t, seqs):
    b, _, _, wb = cache_kt.shape
    assert wb == LANES and t % 8 == 0 and t <= LANES
    n = b * t
    row = lambda w: pl.BlockSpec((seqs * t, w), lambda i: (i, 0))
    win = pl.BlockSpec((seqs, KV_HEADS, HEAD_DIM, wb), lambda i: (i, 0, 0, 0))
    return pl.pallas_call(
        functools.partial(_attn_sample_kernel, seqs=seqs, t=t),
        grid=(b // seqs,),
        in_specs=[pl.BlockSpec(memory_space=pltpu.SMEM), row(ATT_WIDTH), row(KV_WIDTH), row(KV_WIDTH), win, win],
        out_specs=[row(ATT_WIDTH), win, win],
        out_shape=[jax.ShapeDtypeStruct((n, ATT_WIDTH), F32),
                   jax.ShapeDtypeStruct((b, KV_HEADS, HEAD_DIM, wb), F32),
                   jax.ShapeDtypeStruct((b, KV_HEADS, HEAD_DIM, wb), F32)],
        compiler_params=_params("arbitrary"),
        name="attn_sample",
    )(sinks, q, k, v, cache_kt, cache_vt)


def _wkv_kernel(in_ref, s0_ref, o_ref, st_ref, z_ref, *, nseq, nchunks, rows, real):
    c = rows
    pw = LANES
    j = pl.program_id(1)
    npairs = RWKV_HEADS // 2
    pairs = range(npairs)

    @pl.when(j == 0)
    def _():
        for s in range(nseq):
            for p in pairs:
                z_ref[s * npairs + p] = jnp.concatenate([s0_ref[s, 2 * p], s0_ref[s, 2 * p + 1]], axis=0).T

    def pair_iota(nrow):
        return (lax.broadcasted_iota(jnp.int32, (nrow, pw), 0),
                lax.broadcasted_iota(jnp.int32, (nrow, pw), 1) & (HEAD_DIM - 1))

    row, col = pair_iota(c)
    lo = lax.broadcasted_iota(jnp.int32, (1, pw), 1) < HEAD_DIM
    lower2 = jnp.concatenate([row > col, row >= col], axis=0)
    eye = (row == col).astype(F32)
    krow, kcol = pair_iota(HEAD_DIM)
    eye_ch = (krow == kcol).astype(F32)
    ri = lax.broadcasted_iota(jnp.int32, (c, c), 0)
    tri = (ri >= lax.broadcasted_iota(jnp.int32, (c, c), 1)).astype(BF16)

    def bd(y):
        zero = jnp.zeros_like(y)
        parts = [jnp.where(lo, y, zero), jnp.where(lo, zero, y)]
        if y.shape[0] < HEAD_DIM:
            fill = jnp.zeros((HEAD_DIM - y.shape[0], pw), y.dtype)
            parts = [parts[0], fill, parts[1], fill]
        return jnp.concatenate(parts, axis=0)

    memo = {}

    def cached(x, tag, build):
        key = (id(x), tag)
        if key not in memo:
            memo[key] = (x, build())
        return memo[key][1]

    def split(x):
        return cached(x, "split", lambda: _split2(x))

    def lhs(x, passes, axis=1):
        hi, low = split(x)
        return cached(x, ("lhs", passes, axis), lambda: jnp.concatenate([hi, low, hi][:passes], axis=axis))

    def rhs(y, passes, axis=0):
        hi, low = split(y)

        def build():
            bh = bd(hi)
            return jnp.concatenate([bh, bh, bd(low)][:passes], axis=axis)
        return cached(y, ("rhs", passes, axis), build)

    def mm(x, y, group):
        return _dot(lhs(x, WKV_PASSES[group]), rhs(y, WKV_PASSES[group]))

    def mm_nt(x, y, group):
        return _dot(lhs(x, WKV_PASSES[group]), rhs(y, WKV_PASSES[group], 1), NT)

    def mm_tn(x, y, group):
        passes = WKV_PASSES[group]
        yh, yl = split(y)
        full = _dot(lhs(x, passes, 0), jnp.concatenate([yh, yh, yl][:passes], axis=0), TN)
        return jnp.where(lo, full[:HEAD_DIM], full[HEAD_DIM:])

    def prepare(keys, prep):
        for s, q in keys:
            tr = slice(q * real, (q + 1) * real)

            def tok(col):
                x = in_ref[s, tr, col * RWKV_WIDTH:(col + 1) * RWKV_WIDTH]
                return x if real == c else jnp.concatenate([x, jnp.zeros((c - real, x.shape[1]), x.dtype)], axis=0)

            lw = tok(1)
            cum = sum(_dot(tri, part) for part in _split3(lw))
            tot = cum[c - 1:c, :]
            en = jnp.exp(-cum)
            ed = jnp.exp(tot - cum)
            kk = tok(4)
            kb = tok(5)
            kx = tok(2)
            prep[s, q] = dict(a=-kk * jnp.exp(cum - lw), r=tok(0) * jnp.exp(cum), b=kb * en, k=kx * en,
                              bh=kb * ed, kh=kx * ed, v=tok(3), etot=jnp.exp(tot))
            yield

    def chunk_algebra(keys, prep, res):
        items = [(s, q, p) for s, q in keys for p in pairs]
        ps = lambda name, it: prep[it[0], it[1]][name][:, it[2] * pw:(it[2] + 1) * pw]
        each = lambda f: [f(i) for i in range(len(items))]
        pv = {name: each(lambda i: ps(name, items[i])) for name in ("a", "r", "b", "k", "bh", "kh", "v", "etot")}
        ar = each(lambda i: jnp.concatenate([pv["a"][i], pv["r"][i]], axis=0))
        g_b = each(lambda i: jnp.where(lower2, mm_nt(ar[i], pv["b"][i], "gram"), 0.0))
        yield
        g_k = each(lambda i: jnp.where(lower2, mm_nt(ar[i], pv["k"][i], "gram"), 0.0))
        a_rb = each(lambda i: g_b[i][c:])
        yield
        p_m = each(lambda i: g_b[i][:c])
        t_m = each(lambda i: eye + p_m[i])
        n = 2
        while n < c:
            p_m = each(lambda i: mm(p_m[i], p_m[i], "inverse"))
            yield
            t_m = each(lambda i: t_m[i] + mm(p_m[i], t_m[i], "inverse"))
            yield
            n *= 2
        gv = each(lambda i: mm(g_k[i], pv["v"][i], "apply"))
        av = each(lambda i: gv[i][:c])
        yield
        w_m = each(lambda i: mm(t_m[i], pv["a"][i], "apply"))
        yield
        u_m = each(lambda i: mm(t_m[i], av[i], "apply"))
        yield
        r_p = each(lambda i: pv["r"][i] + mm(a_rb[i], w_m[i], "apply"))
        yield
        o_i = each(lambda i: mm(a_rb[i], u_m[i], "apply") + gv[i][c:])
        yield
        m_m = each(lambda i: mm_tn(pv["bh"][i], w_m[i], "state") + eye_ch * pv["etot"][i])
        yield
        bk = each(lambda i: jnp.concatenate([pv["bh"][i], pv["kh"][i]], axis=0))
        uv = each(lambda i: jnp.concatenate([u_m[i], pv["v"][i]], axis=0))
        n_m = each(lambda i: mm_tn(bk[i], uv[i], "state"))
        yield
        for i, it in enumerate(items):
            res[it] = (jnp.concatenate([r_p[i], m_m[i]], axis=0), o_i[i], n_m[i])
        yield

    z = [z_ref[sp] for sp in range(nseq * npairs)]

    def carry(keys, res):
        for s, q in keys:
            for p in pairs:
                rm, o_i, n_m = res[s, q, p]
                sp = s * npairs + p
                oz = mm(rm, z[sp], "carry")
                o_ref[s, q * real:(q + 1) * real, p * pw:(p + 1) * pw] = (oz[:c] + o_i)[:real]
                z[sp] = oz[c:] + n_m
            yield

    def interleave(*gens):
        live = list(gens)
        while live:
            live = [g for g in live if next(g, StopIteration) is not StopIteration]

    keys = [(s, q) for q in range(nchunks) for s in range(nseq)]
    half = len(keys) // 2 if nchunks > 1 else len(keys)
    group_a, group_b = keys[:half], keys[half:]
    prep, res = {}, {}
    interleave(prepare(group_a, prep))
    interleave(chunk_algebra(group_a, prep, res), prepare(group_b, prep))
    interleave(chunk_algebra(group_b, prep, res), carry(group_a, res))
    interleave(carry(group_b, res))
    for sp in range(nseq * npairs):
        z_ref[sp] = z[sp]

    @pl.when(j == pl.num_programs(1) - 1)
    def _():
        for s in range(nseq):
            for p in pairs:
                zt = z_ref[s * npairs + p].T
                st_ref[s, 2 * p] = zt[:HEAD_DIM]
                st_ref[s, 2 * p + 1] = zt[HEAD_DIM:]


def _wkv(scan_in, s0, *, nseq, nchunks, rows, real=None):
    real = rows if real is None else real
    assert real == rows or nchunks == 1
    b, t, wide = scan_in.shape
    w = RWKV_WIDTH
    assert wide == N_SCAN_COLS * w
    blk = nchunks * real
    tok = lambda width: pl.BlockSpec((nseq, blk, width), lambda bi, j: (bi, j, 0))
    st = pl.BlockSpec((nseq, RWKV_HEADS, HEAD_DIM, HEAD_DIM), lambda bi, j: (bi, 0, 0, 0))
    return pl.pallas_call(
        functools.partial(_wkv_kernel, nseq=nseq, nchunks=nchunks, rows=rows, real=real),
        grid=(b // nseq, t // blk),
        in_specs=[tok(wide), st],
        out_specs=[tok(w), st],
        out_shape=[jax.ShapeDtypeStruct((b, t, w), F32),
                   jax.ShapeDtypeStruct((b, RWKV_HEADS, HEAD_DIM, HEAD_DIM), F32)],
        scratch_shapes=[pltpu.VMEM((nseq * RWKV_HEADS // 2, HEAD_DIM, LANES), F32)],
        compiler_params=_params("arbitrary", "arbitrary"),
        name="wkv_scan",
    )(scan_in, s0)


def _post_kernel(o_ref, gate_ref, att_ref, x_ref, gng_ref, gnb_ref, ones_ref, wo_ref, gf_ref,
                 wg_ref, wu_ref, wd_ref, gfin_ref, y_ref, *, final):
    tm = x_ref.shape[0]
    sub = min(tm, POST_SUB_ROWS)
    parts = [slice(i * sub, (i + 1) * sub) for i in range(tm // sub)]
    each = lambda f: [f(i) for i in range(len(parts))]
    ones_bd = ones_ref[...]
    o = each(lambda i: o_ref[parts[i], :])
    mean = each(lambda i: _seg_sum(o[i], ones_bd) * (1.0 / HEAD_DIM))
    oc = each(lambda i: o[i] - mean[i])
    var = each(lambda i: _seg_sum(oc[i] * oc[i], ones_bd) * (1.0 / HEAD_DIM))
    rw = each(lambda i: ((oc[i] * lax.rsqrt(var[i] + GN_EPS) * gng_ref[...] + gnb_ref[...]
                          + gate_ref[parts[i], RWKV_WIDTH:]) * gate_ref[parts[i], :RWKV_WIDTH]).astype(BF16))
    x1 = each(lambda i: x_ref[parts[i], :] + _dot(att_ref[parts[i], :].astype(BF16), wo_ref[:ATT_WIDTH, :])
              + _dot(rw[i], wo_ref[ATT_WIDTH:, :]))
    u = each(lambda i: _rms(x1[i], gf_ref[...]).astype(BF16))
    gate = each(lambda i: _dot(u[i], wg_ref[...]))
    up = each(lambda i: _dot(u[i], wu_ref[...]))
    hid = each(lambda i: (gate[i] * _sigmoid(gate[i]) * up[i]).astype(BF16))
    x2 = each(lambda i: x1[i] + _dot(hid[i], wd_ref[...]))
    for i in range(len(parts)):
        y_ref[parts[i], :] = _rms(x2[i], gfin_ref[...]) if final else x2[i]


def _post(o, gate, att, x, p, *, tm, final):
    n = x.shape[0]
    d_ff = p["w_gate"].shape[1]
    row = lambda w: pl.BlockSpec((tm, w), lambda i: (i, 0))
    once = lambda shape: pl.BlockSpec(shape, lambda i: (0,) * len(shape), pipeline_mode=pl.Buffered(1))
    w = RWKV_WIDTH
    return pl.pallas_call(
        functools.partial(_post_kernel, final=final),
        grid=(n // tm,),
        in_specs=[row(w), row(N_GATE_COLS * w), row(ATT_WIDTH), row(D_MODEL),
                  once((1, w)), once((1, w)), once((LANES, LANES)), once((ATT_WIDTH + w, D_MODEL)),
                  once((1, D_MODEL)),
                  once((D_MODEL, d_ff)), once((D_MODEL, d_ff)), once((d_ff, D_MODEL)), once((1, D_MODEL))],
        out_specs=row(D_MODEL),
        out_shape=jax.ShapeDtypeStruct((n, D_MODEL), F32),
        compiler_params=_params("arbitrary"),
        name="post_ffn",
    )(o, gate, att, x, p["gn_g"], p["gn_b"], p["ones_bd"], p["w_out"], p["g_ffn"],
      p["w_gate"], p["w_up"], p["w_down"], p["g_final"])


def _layer_params(l, g_mix, w_in, attn_sinks, rwkv_mu, w0, w2, a0, a2, g2, k_k, k_a, r_k, gn_g, gn_b,
                  w_out, g_ffn, w_gate, w_up, w_down, g_final):
    vec = lambda a: a.reshape(1, -1).astype(F32)
    hd = jnp.arange(LANES) // HEAD_DIM
    return dict(
        g_mix=g_mix[l], w_in=w_in[l].astype(BF16), sinks=attn_sinks[l].astype(F32),
        mu=vec(rwkv_mu[l]), w0=vec(w0[l]), w2=w2[l].astype(BF16), a0=vec(a0[l]), a2=a2[l].astype(BF16),
        g2=g2[l].astype(BF16), k_k=vec(k_k[l]), k_a=vec(k_a[l]), r_k=vec(r_k[l]),
        gn_g=vec(gn_g[l]), gn_b=vec(gn_b[l]), w_out=w_out[l].astype(BF16), g_ffn=vec(g_ffn[l]),
        w_gate=w_gate[l].astype(BF16), w_up=w_up[l].astype(BF16), w_down=w_down[l].astype(BF16),
        g_final=vec(g_final), ones_bd=(hd[:, None] == hd[None, :]).astype(BF16),
    )


def _pick(n, pref):
    t = pref
    while n % t:
        t //= 2
    return t


def _pick_rows(n, cap, unit):
    assert n % unit == 0
    return unit * max(d for d in range(1, cap // unit + 1) if (n // unit) % d == 0)


def _prompt_layer(x, p, final):
    b, t, d = x.shape
    n = b * t
    x2 = x.reshape(n, d)
    tm = _pick(t, INPROJ_ROWS)
    att, k, v, scan_in, gate = _inproj_prep(x2, p, tm=tm, blocks_per_seq=t // tm)
    s0 = jnp.zeros((b, RWKV_HEADS, HEAD_DIM, HEAD_DIM), F32)
    o, s_new = _wkv(scan_in.reshape(b, t, -1), s0, nseq=_pick(b, WKV_PROMPT_SEQS),
                    nchunks=_pick(t // HEAD_DIM, WKV_PROMPT_CHUNKS), rows=HEAD_DIM)
    y = _post(o.reshape(n, -1), gate, att, x2, p, tm=_pick(n, POST_ROWS), final=final)
    wp = min(WINDOW, t)
    k_win = k.reshape(b, t, KV_WIDTH)[:, t - wp:].reshape(b, wp, KV_HEADS, HEAD_DIM)
    v_win = v.reshape(b, t, KV_WIDTH)[:, t - wp:].reshape(b, wp, KV_HEADS, HEAD_DIM)
    shift = _rms_rows(x[:, -1], p["g_mix"])
    return y.reshape(b, t, d), k_win, v_win, s_new, shift


def _sample_layer(x, h_prev, k_buf, v_buf, s0, p, final):
    b, t, d = x.shape
    n = b * t
    x2 = x.reshape(n, d)
    rows_all = jnp.concatenate([x2, h_prev.astype(x2.dtype)], axis=0)
    q, k, v, feat_all = _inproj(rows_all, p["g_mix"], p["w_in"], norm_rows=n, tm=_pick_rows(n + b, 512, 8))
    scan_in, gate = _prep(feat_all, feat_all[n:], p, n=n, t=t, tm=_pick(n, 512))
    to_t = lambda a: jnp.transpose(a, (0, 2, 3, 1))
    from_t = lambda a: jnp.transpose(a, (0, 3, 1, 2))
    att, k_win, v_win = _attn_sample(q, k, v, to_t(k_buf), to_t(v_buf), p["sinks"], t=t, seqs=_pick(b, ATTN_SAMPLE_SEQS))
    tp = -(-t // BF16_ROWS) * BF16_ROWS
    assert tp <= HEAD_DIM and t % 8 == 0
    o, s_new = _wkv(scan_in.reshape(b, t, -1), s0, nseq=_pick(b, WKV_SAMPLE_SEQS), nchunks=1, rows=tp, real=t)
    y = _post(o.reshape(n, -1), gate, att, x2, p, tm=_pick(n, POST_ROWS), final=final)
    shift = _rms_rows(x[:, -1], p["g_mix"])
    return y.reshape(b, t, d), from_t(k_win), from_t(v_win), s_new, shift


def kernel(x_prompt, x_sample, cache_k, cache_v, state_wkv, state_shift, g_mix, w_in, attn_sinks, rwkv_mu, w0, w2,
           a0, a2, g2, k_k, k_a, r_k, gn_g, gn_b, w_out, g_ffn, w_gate, w_up, w_down, g_final):
    depth = w_in.shape[0]
    xp, xs = x_prompt, x_sample
    outs_p, outs_s = [], []
    for l in range(depth):
        p = _layer_params(l, g_mix, w_in, attn_sinks, rwkv_mu, w0, w2, a0, a2, g2, k_k, k_a, r_k, gn_g, gn_b,
                          w_out, g_ffn, w_gate, w_up, w_down, g_final)
        final = l == depth - 1
        xp, kp, vp, sp, hp = _prompt_layer(xp, p, final)
        xs, kn, vn, sn, hn = _sample_layer(xs, state_shift[l], cache_k[l], cache_v[l], state_wkv[l], p, final)
        outs_p.append((kp, vp, sp, hp))
        outs_s.append((kn, vn, sn, hn))
    stack = lambda outs, i: jnp.stack([o[i] for o in outs])
    return (xp, xs,
            stack(outs_p, 0), stack(outs_p, 1), stack(outs_p, 2), stack(outs_p, 3),
            stack(outs_s, 0), stack(outs_s, 1), stack(outs_s, 2), stack(outs_s, 3))
```
